```python
import math
import jax, jax.numpy as jnp
from jax import lax
import numpy as np

D_MODEL = 1024
BATCH = 4
SEQ = 8192
DEPTH = 2

HEAD_DIM = 64
ROPE_THETA = 500000.0
ROT_DIM = HEAD_DIM // 4
BLOCK = 128
EPS = 1e-6
NEG = -1e30

DILATED_PATTERNS = ((128, 1), (512, 4), (2048, 16))
A_HEADS_PER_GROUP = 4
A_HEADS = A_HEADS_PER_GROUP * len(DILATED_PATTERNS)
A_WIDTH = A_HEADS * HEAD_DIM
A_OUT = A_HEADS_PER_GROUP * HEAD_DIM

B_HEADS = 4
B_QK_WIDTH = B_HEADS * 2 * HEAD_DIM
B_V_DIM = 2 * HEAD_DIM
B_V_WIDTH = B_HEADS * B_V_DIM

POOL_WINDOWS = (2, 4, 8, 16)
C_GROUPS = len(POOL_WINDOWS)
C_GROUP_DIM = 128
C_WIDTH = C_GROUPS * C_GROUP_DIM

N_BRANCH = 3

OFF_QA = 0
OFF_KA = OFF_QA + A_WIDTH
OFF_VA = OFF_KA + A_WIDTH
OFF_QB = OFF_VA + A_WIDTH
OFF_KB = OFF_QB + B_QK_WIDTH
OFF_VB = OFF_KB + B_QK_WIDTH
OFF_C = OFF_VB + B_V_WIDTH
OFF_G = OFF_C + C_WIDTH
IN_COLS = OFF_G + N_BRANCH * D_MODEL

D_FF = 3584
N_EXPERTS = 8
TOP_K = 2
N_DENSE = (DEPTH + 1) // 2
N_MOE = DEPTH // 2

kernel_name = "hybrid_dilated_diff_pool_moe_block"


def rmsnorm(x, g):
    xf = x.astype(jnp.float32)
    y = xf * lax.rsqrt(jnp.mean(xf * xf, axis=-1, keepdims=True) + EPS)
    return (y * g.astype(jnp.float32)).astype(x.dtype)


def rope_tables(positions):
    inv_freq = ROPE_THETA ** (-jnp.arange(0, ROT_DIM, 2, dtype=jnp.float32) / ROT_DIM)
    ang = positions.astype(jnp.float32)[..., None] * inv_freq
    return jnp.cos(ang)[:, :, None, :], jnp.sin(ang)[:, :, None, :]


def apply_partial_rope(x, cos, sin):
    half = ROT_DIM // 2
    c = cos.astype(x.dtype)
    s = sin.astype(x.dtype)
    x1 = x[..., :half]
    x2 = x[..., half:ROT_DIM]
    return jnp.concatenate([x1 * c - x2 * s, x2 * c + x1 * s, x[..., ROT_DIM:]], axis=-1)


def dilated_window_attn(q, k, v, dil, span):
    assert span <= BLOCK
    B, S, H, Dh = q.shape
    L = S // dil
    nblk = -(-L // BLOCK)
    Lp = nblk * BLOCK

    def strided(t):
        t = t.astype(jnp.float32).reshape(B, L, dil, H, Dh).transpose(0, 2, 3, 1, 4)
        return jnp.pad(t, ((0, 0), (0, 0), (0, 0), (0, Lp - L), (0, 0)))

    qs, ks, vs = strided(q), strided(k), strided(v)

    def band(t):
        front = jnp.pad(t, ((0, 0), (0, 0), (0, 0), (BLOCK, 0), (0, 0)))
        prev = front[:, :, :, :Lp].reshape(B, dil, H, nblk, BLOCK, Dh)
        cur = t.reshape(B, dil, H, nblk, BLOCK, Dh)
        return jnp.concatenate([prev, cur], axis=-2)

    qb = qs.reshape(B, dil, H, nblk, BLOCK, Dh)
    kb, vb = band(ks), band(vs)
    s = jnp.einsum('brhnqd,brhnkd->brhnqk', qb, kb) * (Dh ** -0.5)

    a = jnp.arange(BLOCK)[:, None]
    j = jnp.arange(2 * BLOCK)[None, :]
    dist = a + BLOCK - j
    valid = (dist >= 0) & (dist <= span)
    blk = jnp.arange(nblk)[:, None, None]
    mask = valid[None] & ((blk > 0) | (j[None] >= BLOCK))
    s = jnp.where(mask, s, NEG)

    m = jnp.max(s, axis=-1, keepdims=True)
    p = jnp.exp(s - m)
    l = jnp.sum(p, axis=-1, keepdims=True)
    o = jnp.einsum('brhnqk,brhnkd->brhnqd', p, vb) / l
    lse = (m + jnp.log(l))[..., 0]

    o = o.reshape(B, dil, H, Lp, Dh)[:, :, :, :L].transpose(0, 3, 1, 2, 4).reshape(B, S, H, Dh)
    lse = lse.reshape(B, dil, H, Lp)[:, :, :, :L].transpose(0, 3, 1, 2).reshape(B, S, H)
    return o, lse


def dilated_mixture(q, k, v):
    B, S = q.shape[:2]
    outs, lses = [], []
    for g, (window, dil) in enumerate(DILATED_PATTERNS):
        sl = slice(g * A_HEADS_PER_GROUP, (g + 1) * A_HEADS_PER_GROUP)
        o, lse = dilated_window_attn(q[:, :, sl], k[:, :, sl], v[:, :, sl], dil, window // dil)
        outs.append(o)
        lses.append(lse)
    o = jnp.stack(outs, axis=0)
    w = jax.nn.softmax(jnp.stack(lses, axis=0), axis=0)
    out = jnp.sum(w[..., None] * o, axis=0)
    return out.reshape(B, S, A_OUT).astype(q.dtype)


def diff_attention(q, k, v, lam, subln_g, lambda_init):
    B, S, H, _, Dh = q.shape
    nq = S // BLOCK
    qb = q.reshape(B, nq, BLOCK, H, 2, Dh).transpose(1, 0, 2, 3, 4, 5)
    kf = k.astype(jnp.float32)
    vf = v.astype(jnp.float32)
    k_pos = jnp.arange(S)
    scale = Dh ** -0.5

    def one_block(args):
        qblk, i = args
        s = jnp.einsum('bqhcd,bkhcd->bhcqk', qblk.astype(jnp.float32), kf) * scale
        q_pos = i * BLOCK + jnp.arange(BLOCK)
        s = jnp.where(k_pos[None, :] <= q_pos[:, None], s, NEG)
        p = jax.nn.softmax(s, axis=-1)
        a = p[:, :, 0] - lam * p[:, :, 1]
        return jnp.einsum('bhqk,bkhe->bqhe', a, vf)

    o = lax.map(one_block, (qb, jnp.arange(nq)))
    o = o.transpose(1, 0, 2, 3, 4).reshape(B, S, H, 2 * Dh)
    o = rmsnorm(o, subln_g) * (1.0 - lambda_init)
    return o.reshape(B, S, H * 2 * Dh).astype(q.dtype)


def multiscale_pool(c, pool_w, pool_scale):
    B, S, _ = c.shape
    cg = c.astype(jnp.float32).reshape(B, S, C_GROUPS, C_GROUP_DIM)
    t = jnp.arange(S)
    outs = []
    for g, w in enumerate(POOL_WINDOWS):
        xg = cg[:, :, g]
        cs = jnp.cumsum(xg, axis=1)
        cs_shift = jnp.pad(cs, ((0, 0), (w, 0), (0, 0)))[:, :S]
        cnt = jnp.minimum(t + 1, w).astype(jnp.float32)[None, :, None]
        outs.append((cs - cs_shift) / cnt - xg)
    d = jnp.stack(outs, axis=2)
    y = jnp.einsum('bsgc,gce->bsge', d, pool_w.astype(jnp.float32)).reshape(B, S, C_WIDTH)
    return (y * pool_scale.astype(jnp.float32)).astype(c.dtype)


def swiglu(h, wg, wu, wd):
    return (jax.nn.silu(h @ wg) * (h @ wu)) @ wd


def moe_ffn(h, w_router, wg, wu, wd):
    logits = (h @ w_router).astype(jnp.float32)
    topv, topi = lax.top_k(logits, TOP_K)
    probs = jax.nn.softmax(topv, axis=-1)
    gate = jnp.sum(jax.nn.one_hot(topi, N_EXPERTS, dtype=jnp.float32) * probs[..., None], axis=-2)
    y = jnp.zeros_like(h)
    for e in range(N_EXPERTS):
        y = y + gate[..., e:e + 1].astype(h.dtype) * swiglu(h, wg[e], wu[e], wd[e])
    return y


def setup_inputs(seed: int = 0) -> dict:
    key = jax.random.key(seed)
    ks = jax.random.split(key, 24)
    f32 = jnp.float32
    nrm = lambda k, shape, fan: jax.random.normal(k, shape, f32) * (fan ** -0.5)
    x = jax.random.normal(ks[0], (BATCH, SEQ, D_MODEL), f32)
    offset = jax.random.randint(ks[1], (BATCH, 1), 0, 1024, dtype=jnp.int32)
    positions = offset + jnp.arange(SEQ, dtype=jnp.int32)[None, :]
    return {
        "x": x,
        "positions": positions,
        "norm1_g": 1.0 + 0.02 * jax.random.normal(ks[2], (DEPTH, D_MODEL), f32),
        "w_in": nrm(ks[3], (DEPTH, D_MODEL, IN_COLS), D_MODEL),
        "b_gate": 0.02 * jax.random.normal(ks[4], (DEPTH, N_BRANCH, D_MODEL), f32),
        "diff_lambda": 0.1 * jax.random.normal(ks[5], (DEPTH, 4, HEAD_DIM), f32),
        "diff_subln_g": 1.0 + 0.02 * jax.random.normal(ks[6], (DEPTH, B_V_DIM), f32),
        "pool_w": nrm(ks[7], (DEPTH, C_GROUPS, C_GROUP_DIM, C_GROUP_DIM), C_GROUP_DIM),
        "pool_scale": 1.0 + 0.02 * jax.random.normal(ks[8], (DEPTH, C_WIDTH), f32),
        "w_proj_a": nrm(ks[9], (DEPTH, A_OUT, D_MODEL), A_OUT),
        "w_proj_b": nrm(ks[10], (DEPTH, B_V_WIDTH, D_MODEL), B_V_WIDTH),
        "w_proj_c": nrm(ks[11], (DEPTH, C_WIDTH, D_MODEL), C_WIDTH),
        "w_out": nrm(ks[12], (DEPTH, D_MODEL, D_MODEL), D_MODEL),
        "norm2_g": 1.0 + 0.02 * jax.random.normal(ks[13], (DEPTH, D_MODEL), f32),
        "ffn_w_gate": nrm(ks[14], (N_DENSE, D_MODEL, D_FF), D_MODEL),
        "ffn_w_up": nrm(ks[15], (N_DENSE, D_MODEL, D_FF), D_MODEL),
        "ffn_w_down": nrm(ks[16], (N_DENSE, D_FF, D_MODEL), D_FF),
        "moe_router": nrm(ks[17], (N_MOE, D_MODEL, N_EXPERTS), D_MODEL),
        "moe_w_gate": nrm(ks[18], (N_MOE, N_EXPERTS, D_MODEL, D_FF), D_MODEL),
        "moe_w_up": nrm(ks[19], (N_MOE, N_EXPERTS, D_MODEL, D_FF), D_MODEL),
        "moe_w_down": nrm(ks[20], (N_MOE, N_EXPERTS, D_FF, D_MODEL), D_FF),
        "final_norm_g": 1.0 + 0.02 * jax.random.normal(ks[21], (D_MODEL,), f32),
    }


def reference(x, positions, norm1_g, w_in, b_gate, diff_lambda, diff_subln_g, pool_w,
              pool_scale, w_proj_a, w_proj_b, w_proj_c, w_out, norm2_g, ffn_w_gate,
              ffn_w_up, ffn_w_down, moe_router, moe_w_gate, moe_w_up, moe_w_down,
              final_norm_g):
    B, S, D = x.shape
    cos, sin = rope_tables(positions)
    for l in range(DEPTH):
        h = rmsnorm(x, norm1_g[l])
        z = h @ w_in[l]

        qa = apply_partial_rope(z[..., OFF_QA:OFF_KA].reshape(B, S, A_HEADS, HEAD_DIM), cos, sin)
        ka = apply_partial_rope(z[..., OFF_KA:OFF_VA].reshape(B, S, A_HEADS, HEAD_DIM), cos, sin)
        va = z[..., OFF_VA:OFF_QB].reshape(B, S, A_HEADS, HEAD_DIM)
        out_a = dilated_mixture(qa, ka, va)

        qb = apply_partial_rope(z[..., OFF_QB:OFF_KB].reshape(B, S, 2 * B_HEADS, HEAD_DIM), cos, sin)
        kb = apply_partial_rope(z[..., OFF_KB:OFF_VB].reshape(B, S, 2 * B_HEADS, HEAD_DIM), cos, sin)
        qb = qb.reshape(B, S, B_HEADS, 2, HEAD_DIM)
        kb = kb.reshape(B, S, B_HEADS, 2, HEAD_DIM)
        vb = z[..., OFF_VB:OFF_C].reshape(B, S, B_HEADS, B_V_DIM)
        lambda_init = 0.8 - 0.6 * math.exp(-0.3 * l)
        lp = diff_lambda[l].astype(jnp.float32)
        lam = jnp.exp(jnp.sum(lp[0] * lp[1])) - jnp.exp(jnp.sum(lp[2] * lp[3])) + lambda_init
        out_b = diff_attention(qb, kb, vb, lam, diff_subln_g[l], lambda_init)

        out_c = multiscale_pool(z[..., OFF_C:OFF_G], pool_w[l], pool_scale[l])

        gates = jax.nn.sigmoid(z[..., OFF_G:IN_COLS].reshape(B, S, N_BRANCH, D) + b_gate[l])
        mixed = (gates[:, :, 0] * (out_a @ w_proj_a[l])
                 + gates[:, :, 1] * (out_b @ w_proj_b[l])
                 + gates[:, :, 2] * (out_c @ w_proj_c[l]))
        x = x + mixed @ w_out[l]

        h2 = rmsnorm(x, norm2_g[l])
        if l % 2 == 0:
            i = l // 2
            x = x + swiglu(h2, ffn_w_gate[i], ffn_w_up[i], ffn_w_down[i])
        else:
            i = l // 2
            x = x + moe_ffn(h2, moe_router[i], moe_w_gate[i], moe_w_up[i], moe_w_down[i])
    return rmsnorm(x, final_norm_g)
```

```python
import functools
import math

import jax
import jax.numpy as jnp
from jax import lax
from jax.experimental import pallas as pl
from jax.experimental.pallas import tpu as pltpu

F32 = jnp.float32
BF16 = jnp.bfloat16

HEAD_DIM = 64
ROPE_THETA = 500000.0
ROT_DIM = HEAD_DIM // 4
ROT_HALF = ROT_DIM // 2
BLOCK = 128
EPS = 1e-6
NEG = -1e30

DILATED_PATTERNS = ((128, 1), (512, 4), (2048, 16))
A_GROUPS = len(DILATED_PATTERNS)
A_HEADS_PER_GROUP = 4
A_GROUP_WIDTH = A_HEADS_PER_GROUP * HEAD_DIM
A_WIDTH = A_GROUPS * A_GROUP_WIDTH

B_HEADS = 4
B_HEAD_WIDTH = 2 * HEAD_DIM
B_WIDTH = B_HEADS * B_HEAD_WIDTH
B_VEXT = 2 * B_HEAD_WIDTH

POOL_WINDOWS = (2, 4, 8, 16)
C_GROUP_DIM = 128
C_WIDTH = len(POOL_WINDOWS) * C_GROUP_DIM
POOL_HALO = 16

N_BRANCH = 3
N_EXPERTS = 8
TOP_K = 2

OFF_QA = 0
OFF_KA = OFF_QA + A_WIDTH
OFF_VA = OFF_KA + A_WIDTH
OFF_QB = OFF_VA + A_WIDTH
OFF_KB = OFF_QB + B_WIDTH
OFF_VB = OFF_KB + B_WIDTH
OFF_C = OFF_VB + B_WIDTH
OFF_G = OFF_C + C_WIDTH

LANES = 128
V7X_VMEM_BYTES = 64 * 1024 * 1024
VMEM_LIMIT = V7X_VMEM_BYTES - 8 * 1024 * 1024

QK_SCALE = HEAD_DIM ** -0.5


def _params(semantics):
    return pltpu.CompilerParams(dimension_semantics=semantics, vmem_limit_bytes=VMEM_LIMIT)


def _rms(x, g):
    return x * lax.rsqrt(jnp.mean(x * x, axis=-1, keepdims=True) + EPS) * g


def _dot(a, b):
    return jnp.dot(a, b, preferred_element_type=F32)


def _dot_nt(a, b):
    return lax.dot_general(a, b, (((1,), (1,)), ((), ())), preferred_element_type=F32)


def _qkv_kernel(x_ref, g_ref, w_ref, cos_ref, sinlo_ref, sinhi_ref,
                qa0, qa1, qa2, ka0, ka1, ka2, va0, va1, va2, qb, kb, vb):
    hb = _rms(x_ref[...], g_ref[...]).astype(BF16)
    cos = cos_ref[...]
    sinlo = sinlo_ref[...]
    sinhi = sinhi_ref[...]

    def rope(z):
        return (z * cos + pltpu.roll(z, LANES - ROT_HALF, 1) * sinlo
                + pltpu.roll(z, ROT_HALF, 1) * sinhi)

    def project(off, width, out_ref, rotary, scale):
        for c in range(0, width, LANES):
            z = _dot(hb, w_ref[:, off + c:off + c + LANES])
            if rotary:
                z = rope(z)
            if scale != 1.0:
                z = z * scale
            out_ref[:, c:c + LANES] = z.astype(out_ref.dtype)

    for g, (q_ref, k_ref, v_ref) in enumerate(((qa0, ka0, va0), (qa1, ka1, va1), (qa2, ka2, va2))):
        project(OFF_QA + g * A_GROUP_WIDTH, A_GROUP_WIDTH, q_ref, True, QK_SCALE)
        project(OFF_KA + g * A_GROUP_WIDTH, A_GROUP_WIDTH, k_ref, True, 1.0)
        project(OFF_VA + g * A_GROUP_WIDTH, A_GROUP_WIDTH, v_ref, False, 1.0)
    project(OFF_QB, B_WIDTH, qb, True, QK_SCALE)
    project(OFF_KB, B_WIDTH, kb, True, 1.0)
    lane = lax.broadcasted_iota(jnp.int32, (x_ref.shape[0], B_HEAD_WIDTH), 1)
    ones_col = jnp.where(lane == 0, 1.0, 0.0).astype(vb.dtype)
    for h in range(B_HEADS):
        z = _dot(hb, w_ref[:, OFF_VB + h * B_HEAD_WIDTH:OFF_VB + (h + 1) * B_HEAD_WIDTH])
        vb[:, h * B_VEXT:h * B_VEXT + B_HEAD_WIDTH] = z.astype(vb.dtype)
        vb[:, h * B_VEXT + B_HEAD_WIDTH:(h + 1) * B_VEXT] = ones_col


def _qkv_proj(x2, g, w_qkv, cos_t, sinlo_t, sinhi_t, tq):
    t, d = x2.shape
    row = lambda i: (i, 0)
    const = lambda i: (0, 0)
    widths = [A_GROUP_WIDTH] * 9 + [B_WIDTH, B_WIDTH, B_HEADS * B_VEXT]
    return pl.pallas_call(
        _qkv_kernel,
        grid=(t // tq,),
        in_specs=[
            pl.BlockSpec((tq, d), row),
            pl.BlockSpec((1, d), const),
            pl.BlockSpec(w_qkv.shape, const, pipeline_mode=pl.Buffered(1)),
            pl.BlockSpec((tq, LANES), row),
            pl.BlockSpec((tq, LANES), row),
            pl.BlockSpec((tq, LANES), row),
        ],
        out_specs=[pl.BlockSpec((tq, w), row) for w in widths],
        out_shape=[jax.ShapeDtypeStruct((t, w), BF16) for w in widths],
        compiler_params=_params(("arbitrary",)),
        name="qkv_proj",
    )(x2, g, w_qkv, cos_t, sinlo_t, sinhi_t)


def _dilated_kernel(q_ref, kp_ref, kc_ref, vp_ref, vc_ref, o_ref, lse_ref, *, nsub):
    n = pl.program_id(2)
    a = lax.broadcasted_iota(jnp.int32, (BLOCK, 2 * BLOCK), 0)
    j = lax.broadcasted_iota(jnp.int32, (BLOCK, 2 * BLOCK), 1)
    band = (j >= a) & (j <= a + BLOCK)
    band_first = band & ((j >= BLOCK) | (n > 0))
    for sb in range(nsub):
        rows = slice(sb * BLOCK, (sb + 1) * BLOCK)
        if sb == 0:
            kcat = jnp.concatenate([kp_ref[...], kc_ref[rows, :]], axis=0)
            vcat = jnp.concatenate([vp_ref[...], vc_ref[rows, :]], axis=0)
            mask = band_first
        else:
            kcat = kc_ref[(sb - 1) * BLOCK:(sb + 1) * BLOCK, :]
            vcat = vc_ref[(sb - 1) * BLOCK:(sb + 1) * BLOCK, :]
            mask = band
        q = q_ref[rows, :]
        for h in range(A_HEADS_PER_GROUP):
            cols = slice(h * HEAD_DIM, (h + 1) * HEAD_DIM)
            s = _dot_nt(q[:, cols], kcat[:, cols])
            s = jnp.where(mask, s, NEG)
            m = jnp.max(s, axis=-1, keepdims=True)
            p = jnp.exp(s - m)
            l = jnp.sum(p, axis=-1, keepdims=True)
            o = _dot(p.astype(BF16), vcat[:, cols]) / l
            o_ref[rows, cols] = o.astype(o_ref.dtype)
            lse_ref[rows, cols] = jnp.broadcast_to(m + jnp.log(l), (BLOCK, HEAD_DIM))


def _dilated_attn(q, k, v, b, s, dil, qrows):
    w = A_GROUP_WIDTH
    l = s // dil
    qrows = min(qrows, l)
    nsub = qrows // BLOCK
    view = lambda t: t.reshape(b, l, dil * w)
    cur = lambda bi, r, n: (bi, n, r)
    prev = lambda bi, r, n: (bi, jnp.maximum(n * nsub - 1, 0), r)
    o, lse = pl.pallas_call(
        functools.partial(_dilated_kernel, nsub=nsub),
        grid=(b, dil, l // qrows),
        in_specs=[
            pl.BlockSpec((None, qrows, w), cur),
            pl.BlockSpec((None, BLOCK, w), prev),
            pl.BlockSpec((None, qrows, w), cur),
            pl.BlockSpec((None, BLOCK, w), prev),
            pl.BlockSpec((None, qrows, w), cur),
        ],
        out_specs=[pl.BlockSpec((None, qrows, w), cur), pl.BlockSpec((None, qrows, w), cur)],
        out_shape=[jax.ShapeDtypeStruct((b, l, dil * w), BF16),
                   jax.ShapeDtypeStruct((b, l, dil * w), F32)],
        compiler_params=_params(("arbitrary", "arbitrary", "arbitrary")),
        name=f"dilated_attn_d{dil}",
    )(view(q), view(k), view(k), view(v), view(v))
    return o.reshape(b * s, w), lse.reshape(b * s, w)


def _diff_kernel(lam_ref, g_ref, q_ref, k_ref, v_ref, o_ref, acc_ref, m_ref, *, tq, lambda_init):
    i = pl.program_id(2)
    q = q_ref[...]
    q1 = q[:, :HEAD_DIM]
    q2 = q[:, HEAD_DIM:]
    acc_ref[...] = jnp.zeros_like(acc_ref)
    m_ref[...] = jnp.full_like(m_ref, NEG)

    def chunk(jc, diagonal):
        start = pl.multiple_of(jc * tq, tq)
        k = k_ref[pl.ds(start, tq), :]
        v = v_ref[pl.ds(start, tq), :]
        s1 = _dot_nt(q1, k[:, :HEAD_DIM])
        s2 = _dot_nt(q2, k[:, HEAD_DIM:])
        if diagonal:
            r = lax.broadcasted_iota(jnp.int32, (tq, tq), 0)
            c = lax.broadcasted_iota(jnp.int32, (tq, tq), 1)
            causal = c <= r
            s1 = jnp.where(causal, s1, NEG)
            s2 = jnp.where(causal, s2, NEG)
        s = jnp.concatenate([s1, s2], axis=0)
        m_old = m_ref[...]
        m_new = jnp.maximum(m_old, jnp.max(s, axis=-1, keepdims=True))
        alpha = jnp.exp(m_old - m_new)
        p = jnp.exp(s - m_new).astype(BF16)
        acc_ref[...] = acc_ref[...] * alpha + _dot(p, v)
        m_ref[...] = m_new

    def body(jc, carry):
        chunk(jc, False)
        return carry

    lax.fori_loop(0, i, body, 0)
    chunk(i, True)

    lp = lam_ref[...]
    lam = (jnp.exp(jnp.sum(lp[0:1] * lp[1:2], axis=-1, keepdims=True))
           - jnp.exp(jnp.sum(lp[2:3] * lp[3:4], axis=-1, keepdims=True)) + lambda_init)
    acc = acc_ref[...]
    o1 = acc[:tq, :B_HEAD_WIDTH] / acc[:tq, B_HEAD_WIDTH:B_HEAD_WIDTH + 1]
    o2 = acc[tq:, :B_HEAD_WIDTH] / acc[tq:, B_HEAD_WIDTH:B_HEAD_WIDTH + 1]
    o = o1 - lam * o2
    o_ref[...] = (_rms(o, g_ref[...]) * (1.0 - lambda_init)).astype(o_ref.dtype)


def _diff_attn(qb, kb, vb, lam_p, subln_g, b, s, tq, lambda_init):
    hw = B_HEAD_WIDTH
    q3 = qb.reshape(b, s, B_WIDTH)
    k3 = kb.reshape(b, s, B_WIDTH)
    v3 = vb.reshape(b, s, B_HEADS * B_VEXT)
    out = pl.pallas_call(
        functools.partial(_diff_kernel, tq=tq, lambda_init=lambda_init),
        grid=(b, B_HEADS, s // tq),
        in_specs=[
            pl.BlockSpec(lam_p.shape, lambda bi, h, i: (0, 0)),
            pl.BlockSpec((1, hw), lambda bi, h, i: (0, 0)),
            pl.BlockSpec((None, tq, hw), lambda bi, h, i: (bi, i, h)),
            pl.BlockSpec((None, s, hw), lambda bi, h, i: (bi, 0, h)),
            pl.BlockSpec((None, s, B_VEXT), lambda bi, h, i: (bi, 0, h)),
        ],
        out_specs=pl.BlockSpec((None, tq, hw), lambda bi, h, i: (bi, i, h)),
        out_shape=jax.ShapeDtypeStruct((b, s, B_WIDTH), BF16),
        scratch_shapes=[pltpu.VMEM((2 * tq, B_VEXT), F32), pltpu.VMEM((2 * tq, 1), F32)],
        compiler_params=_params(("arbitrary", "arbitrary", "arbitrary")),
        name="diff_attn",
    )(lam_p, subln_g, q3, k3, v3)
    return out.reshape(b * s, B_WIDTH)


def _merge_kernel(*refs, tq, tiles_per_seq, route):
    (x_ref, g1_ref, wc_ref, wg_ref, bg_ref, oa0, oa1, oa2, ls0, ls1, ls2, ob_ref,
     pw_ref, ps_ref, wpa_ref, wpb_ref, wpc_ref, wo_ref, g2_ref) = refs[:19]
    if route:
        wrh_ref, wrl_ref = refs[19:21]
        xo_ref, h2_ref, ri_ref, rp_ref, cnt_ref, zc_ref, carry_ref = refs[21:]
    else:
        xo_ref, h2_ref, zc_ref = refs[19:]
    step = pl.program_id(0)
    seq_tile = step % tiles_per_seq

    x = x_ref[...]
    hb = _rms(x, g1_ref[...]).astype(BF16)

    @pl.when(seq_tile == 0)
    def _():
        zc_ref[0:POOL_HALO, :] = jnp.zeros((POOL_HALO, C_WIDTH), F32)

    zc_ref[POOL_HALO:POOL_HALO + tq, :] = _dot(hb, wc_ref[...])
    pos = seq_tile * tq + lax.broadcasted_iota(jnp.int32, (tq, C_GROUP_DIM), 0)
    pooled = []
    for gi, win in enumerate(POOL_WINDOWS):
        cols = slice(gi * C_GROUP_DIM, (gi + 1) * C_GROUP_DIM)
        tok = zc_ref[POOL_HALO:POOL_HALO + tq, cols]
        tot = tok
        for back in range(1, win):
            tot = tot + zc_ref[POOL_HALO - back:POOL_HALO - back + tq, cols]
        cnt = jnp.minimum(pos + 1, win).astype(F32)
        d = tot / cnt - tok
        pooled.append(_dot(d.astype(BF16), pw_ref[gi]))
    out_c = (jnp.concatenate(pooled, axis=-1) * ps_ref[...]).astype(BF16)
    zc_ref[0:POOL_HALO, :] = zc_ref[tq:tq + POOL_HALO, :]

    l0, l1, l2 = ls0[...], ls1[...], ls2[...]
    lm = jnp.maximum(jnp.maximum(l0, l1), l2)
    e0, e1, e2 = jnp.exp(l0 - lm), jnp.exp(l1 - lm), jnp.exp(l2 - lm)
    out_a = ((e0 * oa0[...].astype(F32) + e1 * oa1[...].astype(F32) + e2 * oa2[...].astype(F32))
             / (e0 + e1 + e2)).astype(BF16)

    d = x.shape[-1]
    branches = (_dot(out_a, wpa_ref[...]), _dot(ob_ref[...], wpb_ref[...]), _dot(out_c, wpc_ref[...]))
    mixed = None
    for bi, proj in enumerate(branches):
        zg = _dot(hb, wg_ref[:, bi * d:(bi + 1) * d]) + bg_ref[:, bi * d:(bi + 1) * d]
        term = jax.nn.sigmoid(zg) * proj
        mixed = term if mixed is None else mixed + term
    xn = x + _dot(mixed.astype(BF16), wo_ref[...])
    xo_ref[...] = xn
    h2 = _rms(xn, g2_ref[...])
    h2_ref[...] = h2.astype(h2_ref.dtype)

    if route:
        h_hi = h2.astype(BF16)
        h_lo = (h2 - h_hi.astype(F32)).astype(BF16)
        logits = _dot(h_hi, wrh_ref[...]) + (_dot(h_lo, wrh_ref[...]) + _dot(h_hi, wrl_ref[...]))
        lane = lax.broadcasted_iota(jnp.int32, (tq, LANES), 1)
        lane_f = lane.astype(F32)
        logits = jnp.where(lane < N_EXPERTS, logits, -jnp.inf)
        m1 = jnp.max(logits, axis=-1, keepdims=True)
        i1 = jnp.min(jnp.where(logits == m1, lane_f, float(LANES)), axis=-1, keepdims=True)
        rest = jnp.where(lane_f == i1, -jnp.inf, logits)
        m2 = jnp.max(rest, axis=-1, keepdims=True)
        i2 = jnp.min(jnp.where(rest == m2, lane_f, float(LANES)), axis=-1, keepdims=True)
        e21 = jnp.exp(m2 - m1)
        p1 = 1.0 / (1.0 + e21)
        p2 = e21 * p1
        sel1 = lane_f == i1
        sel2 = lane_f == i2
        chosen = jnp.where(sel1 | sel2, 1.0, 0.0)

        @pl.when(step == 0)
        def _():
            carry_ref[...] = jnp.zeros_like(carry_ref)

        r = lax.broadcasted_iota(jnp.int32, (tq, tq), 0)
        c = lax.broadcasted_iota(jnp.int32, (tq, tq), 1)
        before = _dot(jnp.where(c < r, 1.0, 0.0).astype(BF16), chosen.astype(BF16)) + carry_ref[...]
        rank1 = jnp.sum(jnp.where(sel1, before, 0.0), axis=-1, keepdims=True)
        rank2 = jnp.sum(jnp.where(sel2, before, 0.0), axis=-1, keepdims=True)
        carry_ref[...] = carry_ref[...] + jnp.sum(chosen, axis=0, keepdims=True)
        packed = jnp.where(lane == 0, i1, jnp.where(lane == 1, i2,
                           jnp.where(lane == 2, rank1, jnp.where(lane == 3, rank2, 0.0))))
        ri_ref[...] = packed.astype(jnp.int32)
        rp_ref[...] = jnp.where(lane == 0, p1, jnp.where(lane == 1, p2, 0.0))
        cnt_ref[...] = jnp.broadcast_to(carry_ref[...], cnt_ref.shape)


def _merge(x2, g1, w_c, w_g, b_g, oa, lse, ob, pool_w, pool_scale, wpa, wpb, wpc, wo, g2,
           router, s, tq, h2_dtype):
    t, d = x2.shape
    route = router is not None
    row = lambda i: (i, 0)
    const = lambda i: (0, 0)
    resident = lambda a: pl.BlockSpec(a.shape, lambda i: (0,) * a.ndim, pipeline_mode=pl.Buffered(1))
    args = [x2, g1, w_c, w_g, b_g, *oa, *lse, ob, pool_w, pool_scale, wpa, wpb, wpc, wo, g2]
    in_specs = [pl.BlockSpec((tq, d), row), pl.BlockSpec((1, d), const), resident(w_c), resident(w_g),
                pl.BlockSpec(b_g.shape, const)]
    in_specs += [pl.BlockSpec((tq, A_GROUP_WIDTH), row)] * 6
    in_specs += [pl.BlockSpec((tq, B_WIDTH), row), resident(pool_w), pl.BlockSpec(pool_scale.shape, const),
                 resident(wpa), resident(wpb), resident(wpc), resident(wo), pl.BlockSpec((1, d), const)]
    out_specs = [pl.BlockSpec((tq, d), row), pl.BlockSpec((tq, d), row)]
    out_shape = [jax.ShapeDtypeStruct((t, d), F32), jax.ShapeDtypeStruct((t, d), h2_dtype)]
    scratch = [pltpu.VMEM((POOL_HALO + tq, C_WIDTH), F32)]
    if route:
        args += list(router)
        in_specs += [resident(router[0]), resident(router[1])]
        out_specs += [pl.BlockSpec((tq, LANES), row), pl.BlockSpec((tq, LANES), row),
                      pl.BlockSpec((8, LANES), const)]
        out_shape += [jax.ShapeDtypeStruct((t, LANES), jnp.int32), jax.ShapeDtypeStruct((t, LANES), F32),
                      jax.ShapeDtypeStruct((8, LANES), F32)]
        scratch += [pltpu.VMEM((1, LANES), F32)]
    return pl.pallas_call(
        functools.partial(_merge_kernel, tq=tq, tiles_per_seq=s // tq, route=route),
        grid=(t // tq,),
        in_specs=in_specs,
        out_specs=out_specs,
        out_shape=out_shape,
        scratch_shapes=scratch,
        compiler_params=_params(("arbitrary",)),
        name="merge_route" if route else "merge",
    )(*args)


def _swiglu_kernel(te_ref, tv_ref, ts_ref, *refs, residual):
    if residual:
        x_ref, res_ref, wg_ref, wu_ref, wd_ref, o_ref, xb_ref, acc_ref = refs
    else:
        x_ref, wg_ref, wu_ref, wd_ref, o_ref, xb_ref, acc_ref = refs
    j = pl.program_id(0)
    c = pl.program_id(1)

    @pl.when((tv_ref[j] == 0) & (c == 0))
    def _():
        o_ref[...] = jnp.zeros_like(o_ref)

    @pl.when(tv_ref[j] > 0)
    def _():
        @pl.when(c == 0)
        def _():
            xb_ref[...] = x_ref[...].astype(BF16)
            acc_ref[...] = jnp.zeros_like(acc_ref)

        xb = xb_ref[...]
        gate = _dot(xb, wg_ref[...])
        up = _dot(xb, wu_ref[...])
        mid = (gate * jax.nn.sigmoid(gate) * up).astype(BF16)
        acc_ref[...] += _dot(mid, wd_ref[...])

        @pl.when(c == pl.num_programs(1) - 1)
        def _():
            if residual:
                o_ref[...] = res_ref[...] + acc_ref[...]
            else:
                o_ref[...] = acc_ref[...]


def _grouped_swiglu(xs, res, wg, wu, wd, tile_expert, tile_valid, tile_src, tm, cf):
    n, d = xs.shape
    ff = wg.shape[-1]
    residual = res is not None
    xmap = lambda j, c, te, tv, ts: (ts[j], 0)
    in_specs = [pl.BlockSpec((tm, d), xmap)]
    args = [xs]
    if residual:
        in_specs.append(pl.BlockSpec((tm, d), xmap))
        args.append(res)
    in_specs += [
        pl.BlockSpec((None, d, cf), lambda j, c, te, tv, ts: (te[j], 0, c * tv[j])),
        pl.BlockSpec((None, d, cf), lambda j, c, te, tv, ts: (te[j], 0, c * tv[j])),
        pl.BlockSpec((None, cf, d), lambda j, c, te, tv, ts: (te[j], c * tv[j], 0)),
    ]
    args += [wg, wu, wd]
    return pl.pallas_call(
        functools.partial(_swiglu_kernel, residual=residual),
        grid_spec=pltpu.PrefetchScalarGridSpec(
            num_scalar_prefetch=3,
            grid=(n // tm, ff // cf),
            in_specs=in_specs,
            out_specs=pl.BlockSpec((tm, d), lambda j, c, te, tv, ts: (j, 0)),
            scratch_shapes=[pltpu.VMEM((tm, d), BF16), pltpu.VMEM((tm, d), F32)],
        ),
        out_shape=jax.ShapeDtypeStruct((n, d), F32),
        compiler_params=_params(("arbitrary", "arbitrary")),
        name="grouped_swiglu_res" if residual else "grouped_swiglu",
    )(tile_expert, tile_valid, tile_src, *args)


def _dispatch_kernel(zt_ref, slot_ref, h_ref, xs_ref, zero_ref, sem, zsem, *, tq, tm):
    @pl.when(pl.program_id(0) == 0)
    def _():
        zero_ref[...] = jnp.zeros_like(zero_ref)
        for z in range(zt_ref.shape[0]):
            start = pl.multiple_of(zt_ref[z] * tm, tm)
            fill = pltpu.make_async_copy(zero_ref, xs_ref.at[pl.ds(start, tm), :], zsem.at[0])
            fill.start()
            fill.wait()

    def row_copy(r, k):
        return pltpu.make_async_copy(h_ref.at[pl.ds(r, 1), :],
                                     xs_ref.at[pl.ds(slot_ref[0, k, r], 1), :], sem.at[k])

    def issue(r, carry):
        row_copy(r, 0).start()
        row_copy(r, 1).start()
        return carry

    lax.fori_loop(0, tq, issue, 0)
    for k in range(TOP_K):
        pltpu.make_async_copy(h_ref, xs_ref.at[pl.ds(0, tq), :], sem.at[k]).wait()


def _moe_dispatch(h2, slots, zero_tiles, nslot, tq, tm):
    t, d = h2.shape
    return pl.pallas_call(
        functools.partial(_dispatch_kernel, tq=tq, tm=tm),
        grid_spec=pltpu.PrefetchScalarGridSpec(
            num_scalar_prefetch=1,
            grid=(t // tq,),
            in_specs=[
                pl.BlockSpec((1, TOP_K, tq), lambda i, zt: (i, 0, 0), memory_space=pltpu.SMEM),
                pl.BlockSpec((tq, d), lambda i, zt: (i, 0)),
            ],
            out_specs=pl.BlockSpec(memory_space=pl.ANY),
            scratch_shapes=[pltpu.VMEM((tm, d), h2.dtype), pltpu.SemaphoreType.DMA((TOP_K,)),
                            pltpu.SemaphoreType.DMA((1,))],
        ),
        out_shape=jax.ShapeDtypeStruct((nslot, d), h2.dtype),
        compiler_params=_params(("arbitrary",)),
        name="moe_dispatch",
    )(zero_tiles, slots, h2)


def _combine_kernel(slot_ref, x_ref, rp_ref, g_ref, ys_ref, o_ref, y_ref, sem, *, tq, normalize):
    def row_copy(r, k):
        return pltpu.make_async_copy(ys_ref.at[pl.ds(slot_ref[0, k, r], 1), :],
                                     y_ref.at[k, pl.ds(r, 1), :], sem.at[k])

    def issue(r, carry):
        row_copy(r, 0).start()
        row_copy(r, 1).start()
        return carry

    lax.fori_loop(0, tq, issue, 0)
    for k in range(TOP_K):
        pltpu.make_async_copy(ys_ref.at[pl.ds(0, tq), :], y_ref.at[k], sem.at[k]).wait()
    rp = rp_ref[...]
    xn = x_ref[...] + rp[:, 0:1] * y_ref[0] + rp[:, 1:2] * y_ref[1]
    o_ref[...] = _rms(xn, g_ref[...]) if normalize else xn


def _moe_combine(x2, route_p, slots, ys, g, tq, normalize):
    t, d = x2.shape
    return pl.pallas_call(
        functools.partial(_combine_kernel, tq=tq, normalize=normalize),
        grid=(t // tq,),
        in_specs=[
            pl.BlockSpec((1, TOP_K, tq), lambda i: (i, 0, 0), memory_space=pltpu.SMEM),
            pl.BlockSpec((tq, d), lambda i: (i, 0)),
            pl.BlockSpec((tq, LANES), lambda i: (i, 0)),
            pl.BlockSpec((1, d), lambda i: (0, 0)),
            pl.BlockSpec(memory_space=pl.ANY),
        ],
        out_specs=pl.BlockSpec((tq, d), lambda i: (i, 0)),
        out_shape=jax.ShapeDtypeStruct((t, d), F32),
        scratch_shapes=[pltpu.VMEM((TOP_K, tq, d), F32), pltpu.SemaphoreType.DMA((TOP_K,))],
        compiler_params=_params(("arbitrary",)),
        name="moe_combine",
    )(slots, x2, route_p, g, ys)


def _final_norm_kernel(x_ref, g_ref, o_ref):
    o_ref[...] = _rms(x_ref[...], g_ref[...])


def _final_norm(x2, g, tq):
    t, d = x2.shape
    return pl.pallas_call(
        _final_norm_kernel,
        grid=(t // tq,),
        in_specs=[pl.BlockSpec((tq, d), lambda i: (i, 0)), pl.BlockSpec((1, d), lambda i: (0, 0))],
        out_specs=pl.BlockSpec((tq, d), lambda i: (i, 0)),
        out_shape=jax.ShapeDtypeStruct((t, d), F32),
        compiler_params=_params(("arbitrary",)),
        name="final_norm",
    )(x2, g)


def _rope_lane_tables(positions):
    inv_freq = ROPE_THETA ** (-jnp.arange(0, ROT_DIM, 2, dtype=F32) / ROT_DIM)
    ang = positions.astype(F32).reshape(-1, 1) * inv_freq
    cos, sin = jnp.cos(ang), jnp.sin(ang)
    t = ang.shape[0]
    rest = HEAD_DIM - ROT_DIM
    z8 = jnp.zeros((t, ROT_HALF), F32)
    cos_h = jnp.concatenate([cos, cos, jnp.ones((t, rest), F32)], axis=-1)
    lo_h = jnp.concatenate([-sin, z8, jnp.zeros((t, rest), F32)], axis=-1)
    hi_h = jnp.concatenate([z8, sin, jnp.zeros((t, rest), F32)], axis=-1)
    rep = LANES // HEAD_DIM
    return jnp.tile(cos_h, (1, rep)), jnp.tile(lo_h, (1, rep)), jnp.tile(hi_h, (1, rep))


def _moe_plan(route_i, counts, t, tm):
    cnt = counts[0, :N_EXPERTS].astype(jnp.int32)
    padded = ((cnt + tm - 1) // tm) * tm
    ends = jnp.cumsum(padded)
    offs = ends - padded
    slot = offs[route_i[:, 0:TOP_K]] + route_i[:, TOP_K:2 * TOP_K]
    ntile = (TOP_K * t) // tm + N_EXPERTS
    starts = jnp.arange(ntile, dtype=jnp.int32) * tm
    valid = (starts < ends[-1]).astype(jnp.int32)
    last = jnp.maximum(ends[-1] // tm - 1, 0)
    src = jnp.minimum(jnp.arange(ntile, dtype=jnp.int32), last)
    expert = jnp.minimum(jnp.searchsorted(ends, src * tm, side="right"), N_EXPERTS - 1).astype(jnp.int32)
    last_tile = jnp.where(padded > 0, ends // tm - 1, ntile - 1).astype(jnp.int32)
    tail = jnp.arange((TOP_K * t) // tm, ntile, dtype=jnp.int32)
    return slot, expert, valid, src, jnp.concatenate([last_tile, tail]), ntile * tm


def kernel(x, positions, norm1_g, w_in, b_gate, diff_lambda, diff_subln_g, pool_w, pool_scale,
           w_proj_a, w_proj_b, w_proj_c, w_out, norm2_g, ffn_w_gate, ffn_w_up, ffn_w_down,
           moe_router, moe_w_gate, moe_w_up, moe_w_down, final_norm_g):
    b, s, d = x.shape
    t = b * s
    depth = w_in.shape[0]
    tq = min(512, s)
    tm = min(512, s)
    cf = 512
    x2 = x.reshape(t, d)
    cos_t, sinlo_t, sinhi_t = _rope_lane_tables(positions)
    out = None
    for l in range(depth):
        w_l = w_in[l]
        qkv = _qkv_proj(x2, norm1_g[l].reshape(1, d), w_l[:, :OFF_C].astype(BF16),
                        cos_t, sinlo_t, sinhi_t, tq)
        qa, ka, va = qkv[0:3], qkv[3:6], qkv[6:9]
        qb, kb, vb = qkv[9:12]
        oa, lse = [], []
        for g, (window, dil) in enumerate(DILATED_PATTERNS):
            assert window // dil == BLOCK
            o_g, lse_g = _dilated_attn(qa[g], ka[g], va[g], b, s, dil, 512)
            oa.append(o_g)
            lse.append(lse_g)
        lambda_init = 0.8 - 0.6 * math.exp(-0.3 * l)
        ob = _diff_attn(qb, kb, vb, diff_lambda[l], diff_subln_g[l].reshape(1, B_HEAD_WIDTH),
                        b, s, tq, lambda_init)

        dense = l % 2 == 0
        router = None
        if not dense:
            wr = jnp.zeros((d, LANES), F32).at[:, :N_EXPERTS].set(moe_router[l // 2])
            wr_hi = wr.astype(BF16)
            router = (wr_hi, (wr - wr_hi.astype(F32)).astype(BF16))
        merged = _merge(
            x2, norm1_g[l].reshape(1, d), w_l[:, OFF_C:OFF_G].astype(BF16), w_l[:, OFF_G:].astype(BF16),
            b_gate[l].reshape(1, N_BRANCH * d), oa, lse, ob, pool_w[l].astype(BF16),
            pool_scale[l].reshape(1, C_WIDTH), w_proj_a[l].astype(BF16), w_proj_b[l].astype(BF16),
            w_proj_c[l].astype(BF16), w_out[l].astype(BF16), norm2_g[l].reshape(1, d),
            router, s, tq, BF16 if dense else F32)
        if dense:
            xn, h2 = merged
            i = l // 2
            ntile = t // tm
            ident = jnp.arange(ntile, dtype=jnp.int32)
            x2 = _grouped_swiglu(h2, xn, ffn_w_gate[i:i + 1].astype(BF16), ffn_w_up[i:i + 1].astype(BF16),
                                 ffn_w_down[i:i + 1].astype(BF16), jnp.zeros((ntile,), jnp.int32),
                                 jnp.ones((ntile,), jnp.int32), ident, tm, cf)
            out = None
        else:
            xn, h2, route_i, route_p, counts = merged
            i = l // 2
            slot, expert, valid, src, zero_tiles, nslot = _moe_plan(route_i, counts, t, tm)
            slots = slot.reshape(t // tq, tq, TOP_K).transpose(0, 2, 1)
            xs = _moe_dispatch(h2, slots, zero_tiles, nslot, tq, tm)
            ys = _grouped_swiglu(xs, None, moe_w_gate[i].astype(BF16), moe_w_up[i].astype(BF16),
                                 moe_w_down[i].astype(BF16), expert, valid, src, tm, cf)
            last = l == depth - 1
            res = _moe_combine(xn, route_p, slots, ys, final_norm_g.reshape(1, d), tq, last)
            if last:
                out = res
            else:
                x2 = res
    if out is None:
        out = _final_norm(x2, final_norm_g.reshape(1, d), tq)
    return out.reshape(b, s, d)
```

```python
import functools
import math

import jax
import jax.numpy as jnp
from jax import lax
from jax.experimental import pallas as pl
from jax.experimental.pallas import tpu as pltpu

F32 = jnp.float32
BF16 = jnp.bfloat16

HEAD_DIM = 64
ROPE_THETA = 500000.0
ROT_DIM = HEAD_DIM // 4
ROT_HALF = ROT_DIM // 2
BLOCK = 128
EPS = 1e-6
NEG = -1e30

DILATED_PATTERNS = ((128, 1), (512, 4), (2048, 16))
A_GROUPS = len(DILATED_PATTERNS)
A_HEADS_PER_GROUP = 4
A_GROUP_WIDTH = A_HEADS_PER_GROUP * HEAD_DIM
A_WIDTH = A_GROUPS * A_GROUP_WIDTH

B_HEADS = 4
B_HEAD_WIDTH = 2 * HEAD_DIM
B_WIDTH = B_HEADS * B_HEAD_WIDTH
B_VEXT = 2 * B_HEAD_WIDTH

POOL_WINDOWS = (2, 4, 8, 16)
C_GROUP_DIM = 128
C_WIDTH = len(POOL_WINDOWS) * C_GROUP_DIM
POOL_HALO = 16

N_BRANCH = 3
N_EXPERTS = 8
TOP_K = 2

OFF_QA = 0
OFF_KA = OFF_QA + A_WIDTH
OFF_VA = OFF_KA + A_WIDTH
OFF_QB = OFF_VA + A_WIDTH
OFF_KB = OFF_QB + B_WIDTH
OFF_VB = OFF_KB + B_WIDTH
OFF_C = OFF_VB + B_WIDTH
OFF_G = OFF_C + C_WIDTH

LANES = 128
V7X_VMEM_BYTES = 64 * 1024 * 1024
VMEM_LIMIT = V7X_VMEM_BYTES - 8 * 1024 * 1024

QK_SCALE = HEAD_DIM ** -0.5
LOG2_E = math.log2(math.e)


def _params(semantics):
    return pltpu.CompilerParams(dimension_semantics=semantics, vmem_limit_bytes=VMEM_LIMIT)


def _rms(x, g):
    return x * lax.rsqrt(jnp.mean(x * x, axis=-1, keepdims=True) + EPS) * g


def _dot(a, b):
    return jnp.dot(a, b, preferred_element_type=F32)


def _dot_nt(a, b):
    return lax.dot_general(a, b, (((1,), (1,)), ((), ())), preferred_element_type=F32)


def _qkv_kernel(x_ref, g_ref, w_ref, cos_ref, sinlo_ref, sinhi_ref,
                qa0, qa1, qa2, ka0, ka1, ka2, va0, va1, va2, qb, kb, vb):
    hb = _rms(x_ref[...], g_ref[...]).astype(BF16)
    cos = cos_ref[...]
    sinlo = sinlo_ref[...]
    sinhi = sinhi_ref[...]

    def rope(z):
        return (z * cos + pltpu.roll(z, LANES - ROT_HALF, 1) * sinlo
                + pltpu.roll(z, ROT_HALF, 1) * sinhi)

    def project(off, width, out_ref, rotary, scale):
        for c in range(0, width, LANES):
            z = _dot(hb, w_ref[:, off + c:off + c + LANES])
            if rotary:
                z = rope(z)
            if scale != 1.0:
                z = z * scale
            out_ref[:, c:c + LANES] = z.astype(out_ref.dtype)

    for g, (q_ref, k_ref, v_ref) in enumerate(((qa0, ka0, va0), (qa1, ka1, va1), (qa2, ka2, va2))):
        project(OFF_QA + g * A_GROUP_WIDTH, A_GROUP_WIDTH, q_ref, True, QK_SCALE)
        project(OFF_KA + g * A_GROUP_WIDTH, A_GROUP_WIDTH, k_ref, True, 1.0)
        project(OFF_VA + g * A_GROUP_WIDTH, A_GROUP_WIDTH, v_ref, False, 1.0)
    project(OFF_QB, B_WIDTH, qb, True, QK_SCALE * LOG2_E)
    project(OFF_KB, B_WIDTH, kb, True, 1.0)
    lane = lax.broadcasted_iota(jnp.int32, (x_ref.shape[0], B_HEAD_WIDTH), 1)
    ones_col = jnp.where(lane == 0, 1.0, 0.0).astype(vb.dtype)
    for h in range(B_HEADS):
        z = _dot(hb, w_ref[:, OFF_VB + h * B_HEAD_WIDTH:OFF_VB + (h + 1) * B_HEAD_WIDTH])
        vb[:, h * B_VEXT:h * B_VEXT + B_HEAD_WIDTH] = z.astype(vb.dtype)
        vb[:, h * B_VEXT + B_HEAD_WIDTH:(h + 1) * B_VEXT] = ones_col


def _qkv_proj(x2, g, w_qkv, cos_t, sinlo_t, sinhi_t, tq):
    t, d = x2.shape
    row = lambda i: (i, 0)
    const = lambda i: (0, 0)
    widths = [A_GROUP_WIDTH] * 9 + [B_WIDTH, B_WIDTH, B_HEADS * B_VEXT]
    return pl.pallas_call(
        _qkv_kernel,
        grid=(t // tq,),
        in_specs=[
            pl.BlockSpec((tq, d), row),
            pl.BlockSpec((1, d), const),
            pl.BlockSpec(w_qkv.shape, const, pipeline_mode=pl.Buffered(1)),
            pl.BlockSpec((tq, LANES), row),
            pl.BlockSpec((tq, LANES), row),
            pl.BlockSpec((tq, LANES), row),
        ],
        out_specs=[pl.BlockSpec((tq, w), row) for w in widths],
        out_shape=[jax.ShapeDtypeStruct((t, w), BF16) for w in widths],
        compiler_params=_params(("arbitrary",)),
        name="qkv_proj",
    )(x2, g, w_qkv, cos_t, sinlo_t, sinhi_t)


def _dilated_kernel(q_ref, kp_ref, kc_ref, vp_ref, vc_ref, o_ref, lse_ref, *, nsub):
    n = pl.program_id(2)
    a = lax.broadcasted_iota(jnp.int32, (BLOCK, 2 * BLOCK), 0)
    j = lax.broadcasted_iota(jnp.int32, (BLOCK, 2 * BLOCK), 1)
    band = (j >= a) & (j <= a + BLOCK)
    band_first = band & ((j >= BLOCK) | (n > 0))
    for sb in range(nsub):
        rows = slice(sb * BLOCK, (sb + 1) * BLOCK)
        if sb == 0:
            kcat = jnp.concatenate([kp_ref[...], kc_ref[rows, :]], axis=0)
            vcat = jnp.concatenate([vp_ref[...], vc_ref[rows, :]], axis=0)
            mask = band_first
        else:
            kcat = kc_ref[(sb - 1) * BLOCK:(sb + 1) * BLOCK, :]
            vcat = vc_ref[(sb - 1) * BLOCK:(sb + 1) * BLOCK, :]
            mask = band
        q = q_ref[rows, :]
        for h in range(A_HEADS_PER_GROUP):
            cols = slice(h * HEAD_DIM, (h + 1) * HEAD_DIM)
            s = _dot_nt(q[:, cols], kcat[:, cols])
            s = jnp.where(mask, s, NEG)
            m = jnp.max(s, axis=-1, keepdims=True)
            p = jnp.exp(s - m)
            l = jnp.sum(p, axis=-1, keepdims=True)
            o = _dot(p.astype(BF16), vcat[:, cols]) / l
            o_ref[rows, cols] = o.astype(o_ref.dtype)
            lse_ref[rows, cols] = jnp.broadcast_to(m + jnp.log(l), (BLOCK, HEAD_DIM))


def _dilated_attn(q, k, v, b, s, dil, qrows):
    w = A_GROUP_WIDTH
    l = s // dil
    qrows = min(qrows, l)
    nsub = qrows // BLOCK
    view = lambda t: t.reshape(b, l, dil * w)
    cur = lambda bi, r, n: (bi, n, r)
    prev = lambda bi, r, n: (bi, jnp.maximum(n * nsub - 1, 0), r)
    o, lse = pl.pallas_call(
        functools.partial(_dilated_kernel, nsub=nsub),
        grid=(b, dil, l // qrows),
        in_specs=[
            pl.BlockSpec((None, qrows, w), cur),
            pl.BlockSpec((None, BLOCK, w), prev),
            pl.BlockSpec((None, qrows, w), cur),
            pl.BlockSpec((None, BLOCK, w), prev),
            pl.BlockSpec((None, qrows, w), cur),
        ],
        out_specs=[pl.BlockSpec((None, qrows, w), cur), pl.BlockSpec((None, qrows, w), cur)],
        out_shape=[jax.ShapeDtypeStruct((b, l, dil * w), BF16),
                   jax.ShapeDtypeStruct((b, l, dil * w), F32)],
        compiler_params=_params(("arbitrary", "arbitrary", "arbitrary")),
        name=f"dilated_attn_d{dil}",
    )(view(q), view(k), view(k), view(v), view(v))
    return o.reshape(b * s, w), lse.reshape(b * s, w)


def _diff_kernel(lam_ref, g_ref, q_ref, k_ref, v_ref, o_ref, acc_ref, m_ref, sa_ref, sb_ref,
                 *, tq, rows, lambda_init):
    i = pl.program_id(2)
    tk = tq // 2
    q = q_ref[...]
    acc_ref[...] = jnp.zeros_like(acc_ref)
    m_ref[...] = jnp.full_like(m_ref, NEG)
    streams = [(mp, rb) for rb in range(tq // rows) for mp in range(2)]

    def visible(rb, diag):
        if diag is None:
            return tk
        return max(0, min(tk, (rb + 1) * rows - diag * tk))

    def scores(jc, dst_ref, diag=None):
        k = k_ref[pl.ds(pl.multiple_of(jc * tk, tk), tk), :]
        for mp, rb in streams:
            if visible(rb, diag) == 0:
                continue
            cols = slice(mp * HEAD_DIM, (mp + 1) * HEAD_DIM)
            rs = slice(rb * rows, (rb + 1) * rows)
            dst_ref[mp, rs, :] = _dot_nt(q[rs, cols], k[:, cols])

    def consume(jc, src_ref, diag=None):
        start = pl.multiple_of(jc * tk, tk)
        for mp, rb in streams:
            vis = visible(rb, diag)
            if vis == 0:
                continue
            rs = slice(rb * rows, (rb + 1) * rows)
            s = src_ref[mp, rs, :vis]
            if diag is not None and diag * tk + vis > rb * rows + 1:
                r = lax.broadcasted_iota(jnp.int32, (rows, vis), 0) + rb * rows
                c = lax.broadcasted_iota(jnp.int32, (rows, vis), 1) + diag * tk
                s = jnp.where(c <= r, s, NEG)
            slabs = [s[:, n * LANES:(n + 1) * LANES] for n in range(vis // LANES)]
            smax = slabs[0]
            for sl in slabs[1:]:
                smax = jnp.maximum(smax, sl)
            m_old = m_ref[mp, rs]
            m_new = jnp.maximum(m_old, jnp.max(smax, axis=-1, keepdims=True))
            alpha = jnp.exp2(m_old - m_new)
            p = jnp.concatenate([jnp.exp2(sl - m_new) for sl in slabs], axis=1).astype(BF16)
            pv = _dot(p, v_ref[pl.ds(start, vis), :])
            acc = acc_ref[mp, rs]
            acc_ref[mp, rs] = jnp.concatenate(
                [acc[:, n * LANES:(n + 1) * LANES] * alpha for n in range(B_VEXT // LANES)], axis=1) + pv
            m_ref[mp, rs] = m_new

    scores(0, sa_ref)

    def pair(t, carry):
        jc = 2 * t
        scores(jc + 1, sb_ref)
        consume(jc, sa_ref)
        scores(jc + 2, sa_ref)
        consume(jc + 1, sb_ref)
        return carry

    lax.fori_loop(0, i, pair, 0)
    scores(2 * i + 1, sb_ref, diag=1)
    consume(2 * i, sa_ref, diag=0)
    consume(2 * i + 1, sb_ref, diag=1)

    lp = lam_ref[...]
    lam = (jnp.exp(jnp.sum(lp[0:1] * lp[1:2], axis=-1, keepdims=True))
           - jnp.exp(jnp.sum(lp[2:3] * lp[3:4], axis=-1, keepdims=True)) + lambda_init)
    acc1 = acc_ref[0]
    acc2 = acc_ref[1]
    o1 = acc1[:, :B_HEAD_WIDTH] / acc1[:, B_HEAD_WIDTH:B_HEAD_WIDTH + 1]
    o2 = acc2[:, :B_HEAD_WIDTH] / acc2[:, B_HEAD_WIDTH:B_HEAD_WIDTH + 1]
    o = o1 - lam * o2
    o_ref[...] = (_rms(o, g_ref[...]) * (1.0 - lambda_init)).astype(o_ref.dtype)


def _diff_attn(qb, kb, vb, lam_p, subln_g, b, s, tq, lambda_init):
    hw = B_HEAD_WIDTH
    q3 = qb.reshape(b, s, B_WIDTH)
    k3 = kb.reshape(b, s, B_WIDTH)
    v3 = vb.reshape(b, s, B_HEADS * B_VEXT)
    out = pl.pallas_call(
        functools.partial(_diff_kernel, tq=tq, rows=min(256, tq // 2), lambda_init=lambda_init),
        grid=(b, B_HEADS, s // tq),
        in_specs=[
            pl.BlockSpec(lam_p.shape, lambda bi, h, i: (0, 0)),
            pl.BlockSpec((1, hw), lambda bi, h, i: (0, 0)),
            pl.BlockSpec((None, tq, hw), lambda bi, h, i: (bi, i, h)),
            pl.BlockSpec((None, s, hw), lambda bi, h, i: (bi, 0, h)),
            pl.BlockSpec((None, s, B_VEXT), lambda bi, h, i: (bi, 0, h)),
        ],
        out_specs=pl.BlockSpec((None, tq, hw), lambda bi, h, i: (bi, i, h)),
        out_shape=jax.ShapeDtypeStruct((b, s, B_WIDTH), BF16),
        scratch_shapes=[pltpu.VMEM((2, tq, B_VEXT), F32), pltpu.VMEM((2, tq, LANES), F32),
                        pltpu.VMEM((2, tq, tq // 2), F32), pltpu.VMEM((2, tq, tq // 2), F32)],
        compiler_params=_params(("arbitrary", "arbitrary", "arbitrary")),
        name="diff_attn",
    )(lam_p, subln_g, q3, k3, v3)
    return out.reshape(b * s, B_WIDTH)


def _merge_kernel(*refs, tq, tiles_per_seq, route):
    (x_ref, g1_ref, wc_ref, wg_ref, bg_ref, oa0, oa1, oa2, ls0, ls1, ls2, ob_ref,
     pw_ref, ps_ref, wpa_ref, wpb_ref, wpc_ref, wo_ref, g2_ref) = refs[:19]
    if route:
        wrh_ref, wrl_ref = refs[19:21]
        xo_ref, h2_ref, ri_ref, rp_ref, cnt_ref, zc_ref, carry_ref = refs[21:]
    else:
        xo_ref, h2_ref, zc_ref = refs[19:]
    step = pl.program_id(0)
    seq_tile = step % tiles_per_seq

    x = x_ref[...]
    hb = _rms(x, g1_ref[...]).astype(BF16)

    @pl.when(seq_tile == 0)
    def _():
        zc_ref[0:POOL_HALO, :] = jnp.zeros((POOL_HALO, C_WIDTH), F32)

    zc_ref[POOL_HALO:POOL_HALO + tq, :] = _dot(hb, wc_ref[...])
    pos = seq_tile * tq + lax.broadcasted_iota(jnp.int32, (tq, C_GROUP_DIM), 0)
    pooled = []
    for gi, win in enumerate(POOL_WINDOWS):
        cols = slice(gi * C_GROUP_DIM, (gi + 1) * C_GROUP_DIM)
        tok = zc_ref[POOL_HALO:POOL_HALO + tq, cols]
        tot = tok
        for back in range(1, win):
            tot = tot + zc_ref[POOL_HALO - back:POOL_HALO - back + tq, cols]
        cnt = jnp.minimum(pos + 1, win).astype(F32)
        d = tot / cnt - tok
        pooled.append(_dot(d.astype(BF16), pw_ref[gi]))
    out_c = (jnp.concatenate(pooled, axis=-1) * ps_ref[...]).astype(BF16)
    zc_ref[0:POOL_HALO, :] = zc_ref[tq:tq + POOL_HALO, :]

    l0, l1, l2 = ls0[...], ls1[...], ls2[...]
    lm = jnp.maximum(jnp.maximum(l0, l1), l2)
    e0, e1, e2 = jnp.exp(l0 - lm), jnp.exp(l1 - lm), jnp.exp(l2 - lm)
    out_a = ((e0 * oa0[...].astype(F32) + e1 * oa1[...].astype(F32) + e2 * oa2[...].astype(F32))
             / (e0 + e1 + e2)).astype(BF16)

    d = x.shape[-1]
    branches = (_dot(out_a, wpa_ref[...]), _dot(ob_ref[...], wpb_ref[...]), _dot(out_c, wpc_ref[...]))
    mixed = None
    for bi, proj in enumerate(branches):
        zg = _dot(hb, wg_ref[:, bi * d:(bi + 1) * d]) + bg_ref[:, bi * d:(bi + 1) * d]
        term = jax.nn.sigmoid(zg) * proj
        mixed = term if mixed is None else mixed + term
    xn = x + _dot(mixed.astype(BF16), wo_ref[...])
    xo_ref[...] = xn
    h2 = _rms(xn, g2_ref[...])
    h2_ref[...] = h2.astype(h2_ref.dtype)

    if route:
        h_hi = h2.astype(BF16)
        h_lo = (h2 - h_hi.astype(F32)).astype(BF16)
        logits = _dot(h_hi, wrh_ref[...]) + (_dot(h_lo, wrh_ref[...]) + _dot(h_hi, wrl_ref[...]))
        lane = lax.broadcasted_iota(jnp.int32, (tq, LANES), 1)
        lane_f = lane.astype(F32)
        logits = jnp.where(lane < N_EXPERTS, logits, -jnp.inf)
        m1 = jnp.max(logits, axis=-1, keepdims=True)
        i1 = jnp.min(jnp.where(logits == m1, lane_f, float(LANES)), axis=-1, keepdims=True)
        rest = jnp.where(lane_f == i1, -jnp.inf, logits)
        m2 = jnp.max(rest, axis=-1, keepdims=True)
        i2 = jnp.min(jnp.where(rest == m2, lane_f, float(LANES)), axis=-1, keepdims=True)
        e21 = jnp.exp(m2 - m1)
        p1 = 1.0 / (1.0 + e21)
        p2 = e21 * p1
        sel1 = lane_f == i1
        sel2 = lane_f == i2
        chosen = jnp.where(sel1 | sel2, 1.0, 0.0)

        @pl.when(step == 0)
        def _():
            carry_ref[...] = jnp.zeros_like(carry_ref)

        r = lax.broadcasted_iota(jnp.int32, (tq, tq), 0)
        c = lax.broadcasted_iota(jnp.int32, (tq, tq), 1)
        before = _dot(jnp.where(c < r, 1.0, 0.0).astype(BF16), chosen.astype(BF16)) + carry_ref[...]
        rank1 = jnp.sum(jnp.where(sel1, before, 0.0), axis=-1, keepdims=True)
        rank2 = jnp.sum(jnp.where(sel2, before, 0.0), axis=-1, keepdims=True)
        carry_ref[...] = carry_ref[...] + jnp.sum(chosen, axis=0, keepdims=True)
        packed = jnp.where(lane == 0, i1, jnp.where(lane == 1, i2,
                           jnp.where(lane == 2, rank1, jnp.where(lane == 3, rank2, 0.0))))
        ri_ref[...] = packed.astype(jnp.int32)
        rp_ref[...] = jnp.where(lane == 0, p1, jnp.where(lane == 1, p2, 0.0))
        cnt_ref[...] = jnp.broadcast_to(carry_ref[...], cnt_ref.shape)


def _merge(x2, g1, w_c, w_g, b_g, oa, lse, ob, pool_w, pool_scale, wpa, wpb, wpc, wo, g2,
           router, s, tq, h2_dtype):
    t, d = x2.shape
    route = router is not None
    row = lambda i: (i, 0)
    const = lambda i: (0, 0)
    resident = lambda a: pl.BlockSpec(a.shape, lambda i: (0,) * a.ndim, pipeline_mode=pl.Buffered(1))
    args = [x2, g1, w_c, w_g, b_g, *oa, *lse, ob, pool_w, pool_scale, wpa, wpb, wpc, wo, g2]
    in_specs = [pl.BlockSpec((tq, d), row), pl.BlockSpec((1, d), const), resident(w_c), resident(w_g),
                pl.BlockSpec(b_g.shape, const)]
    in_specs += [pl.BlockSpec((tq, A_GROUP_WIDTH), row)] * 6
    in_specs += [pl.BlockSpec((tq, B_WIDTH), row), resident(pool_w), pl.BlockSpec(pool_scale.shape, const),
                 resident(wpa), resident(wpb), resident(wpc), resident(wo), pl.BlockSpec((1, d), const)]
    out_specs = [pl.BlockSpec((tq, d), row), pl.BlockSpec((tq, d), row)]
    out_shape = [jax.ShapeDtypeStruct((t, d), F32), jax.ShapeDtypeStruct((t, d), h2_dtype)]
    scratch = [pltpu.VMEM((POOL_HALO + tq, C_WIDTH), F32)]
    if route:
        args += list(router)
        in_specs += [resident(router[0]), resident(router[1])]
        out_specs += [pl.BlockSpec((tq, LANES), row), pl.BlockSpec((tq, LANES), row),
                      pl.BlockSpec((8, LANES), const)]
        out_shape += [jax.ShapeDtypeStruct((t, LANES), jnp.int32), jax.ShapeDtypeStruct((t, LANES), F32),
                      jax.ShapeDtypeStruct((8, LANES), F32)]
        scratch += [pltpu.VMEM((1, LANES), F32)]
    return pl.pallas_call(
        functools.partial(_merge_kernel, tq=tq, tiles_per_seq=s // tq, route=route),
        grid=(t // tq,),
        in_specs=in_specs,
        out_specs=out_specs,
        out_shape=out_shape,
        scratch_shapes=scratch,
        compiler_params=_params(("arbitrary",)),
        name="merge_route" if route else "merge",
    )(*args)


def _swiglu_kernel(te_ref, tv_ref, ts_ref, *refs, residual):
    if residual:
        x_ref, res_ref, wg_ref, wu_ref, wd_ref, o_ref, xb_ref, acc_ref = refs
    else:
        x_ref, wg_ref, wu_ref, wd_ref, o_ref, xb_ref, acc_ref = refs
    j = pl.program_id(0)
    c = pl.program_id(1)

    @pl.when((tv_ref[j] == 0) & (c == 0))
    def _():
        o_ref[...] = jnp.zeros_like(o_ref)

    @pl.when(tv_ref[j] > 0)
    def _():
        @pl.when(c == 0)
        def _():
            xb_ref[...] = x_ref[...].astype(BF16)
            acc_ref[...] = jnp.zeros_like(acc_ref)

        xb = xb_ref[...]
        gate = _dot(xb, wg_ref[...])
        up = _dot(xb, wu_ref[...])
        mid = (gate * jax.nn.sigmoid(gate) * up).astype(BF16)
        acc_ref[...] += _dot(mid, wd_ref[...])

        @pl.when(c == pl.num_programs(1) - 1)
        def _():
            if residual:
                o_ref[...] = res_ref[...] + acc_ref[...]
            else:
                o_ref[...] = acc_ref[...]


def _grouped_swiglu(xs, res, wg, wu, wd, tile_expert, tile_valid, tile_src, tm, cf):
    n, d = xs.shape
    ff = wg.shape[-1]
    residual = res is not None
    xmap = lambda j, c, te, tv, ts: (ts[j], 0)
    in_specs = [pl.BlockSpec((tm, d), xmap)]
    args = [xs]
    if residual:
        in_specs.append(pl.BlockSpec((tm, d), xmap))
        args.append(res)
    in_specs += [
        pl.BlockSpec((None, d, cf), lambda j, c, te, tv, ts: (te[j], 0, c * tv[j])),
        pl.BlockSpec((None, d, cf), lambda j, c, te, tv, ts: (te[j], 0, c * tv[j])),
        pl.BlockSpec((None, cf, d), lambda j, c, te, tv, ts: (te[j], c * tv[j], 0)),
    ]
    args += [wg, wu, wd]
    return pl.pallas_call(
        functools.partial(_swiglu_kernel, residual=residual),
        grid_spec=pltpu.PrefetchScalarGridSpec(
            num_scalar_prefetch=3,
            grid=(n // tm, ff // cf),
            in_specs=in_specs,
            out_specs=pl.BlockSpec((tm, d), lambda j, c, te, tv, ts: (j, 0)),
            scratch_shapes=[pltpu.VMEM((tm, d), BF16), pltpu.VMEM((tm, d), F32)],
        ),
        out_shape=jax.ShapeDtypeStruct((n, d), F32),
        compiler_params=_params(("arbitrary", "arbitrary")),
        name="grouped_swiglu_res" if residual else "grouped_swiglu",
    )(tile_expert, tile_valid, tile_src, *args)


def _dispatch_kernel(zt_ref, slot_ref, h_ref, xs_ref, zero_ref, sem, zsem, *, tq, tm):
    @pl.when(pl.program_id(0) == 0)
    def _():
        zero_ref[...] = jnp.zeros_like(zero_ref)
        for z in range(zt_ref.shape[0]):
            start = pl.multiple_of(zt_ref[z] * tm, tm)
            fill = pltpu.make_async_copy(zero_ref, xs_ref.at[pl.ds(start, tm), :], zsem.at[0])
            fill.start()
            fill.wait()

    def row_copy(r, k):
        return pltpu.make_async_copy(h_ref.at[pl.ds(r, 1), :],
                                     xs_ref.at[pl.ds(slot_ref[0, k, r], 1), :], sem.at[k])

    def issue(r, carry):
        row_copy(r, 0).start()
        row_copy(r, 1).start()
        return carry

    lax.fori_loop(0, tq, issue, 0)
    for k in range(TOP_K):
        pltpu.make_async_copy(h_ref, xs_ref.at[pl.ds(0, tq), :], sem.at[k]).wait()


def _moe_dispatch(h2, slots, zero_tiles, nslot, tq, tm):
    t, d = h2.shape
    return pl.pallas_call(
        functools.partial(_dispatch_kernel, tq=tq, tm=tm),
        grid_spec=pltpu.PrefetchScalarGridSpec(
            num_scalar_prefetch=1,
            grid=(t // tq,),
            in_specs=[
                pl.BlockSpec((1, TOP_K, tq), lambda i, zt: (i, 0, 0), memory_space=pltpu.SMEM),
                pl.BlockSpec((tq, d), lambda i, zt: (i, 0)),
            ],
            out_specs=pl.BlockSpec(memory_space=pl.ANY),
            scratch_shapes=[pltpu.VMEM((tm, d), h2.dtype), pltpu.SemaphoreType.DMA((TOP_K,)),
                            pltpu.SemaphoreType.DMA((1,))],
        ),
        out_shape=jax.ShapeDtypeStruct((nslot, d), h2.dtype),
        compiler_params=_params(("arbitrary",)),
        name="moe_dispatch",
    )(zero_tiles, slots, h2)


def _combine_kernel(slot_ref, x_ref, rp_ref, g_ref, ys_ref, o_ref, y_ref, sem, *, tq, normalize):
    def row_copy(r, k):
        return pltpu.make_async_copy(ys_ref.at[pl.ds(slot_ref[0, k, r], 1), :],
                                     y_ref.at[k, pl.ds(r, 1), :], sem.at[k])

    def issue(r, carry):
        row_copy(r, 0).start()
        row_copy(r, 1).start()
        return carry

    lax.fori_loop(0, tq, issue, 0)
    for k in range(TOP_K):
        pltpu.make_async_copy(ys_ref.at[pl.ds(0, tq), :], y_ref.at[k], sem.at[k]).wait()
    rp = rp_ref[...]
    xn = x_ref[...] + rp[:, 0:1] * y_ref[0] + rp[:, 1:2] * y_ref[1]
    o_ref[...] = _rms(xn, g_ref[...]) if normalize else xn


def _moe_combine(x2, route_p, slots, ys, g, tq, normalize):
    t, d = x2.shape
    return pl.pallas_call(
        functools.partial(_combine_kernel, tq=tq, normalize=normalize),
        grid=(t // tq,),
        in_specs=[
            pl.BlockSpec((1, TOP_K, tq), lambda i: (i, 0, 0), memory_space=pltpu.SMEM),
            pl.BlockSpec((tq, d), lambda i: (i, 0)),
            pl.BlockSpec((tq, LANES), lambda i: (i, 0)),
            pl.BlockSpec((1, d), lambda i: (0, 0)),
            pl.BlockSpec(memory_space=pl.ANY),
        ],
        out_specs=pl.BlockSpec((tq, d), lambda i: (i, 0)),
        out_shape=jax.ShapeDtypeStruct((t, d), F32),
        scratch_shapes=[pltpu.VMEM((TOP_K, tq, d), F32), pltpu.SemaphoreType.DMA((TOP_K,))],
        compiler_params=_params(("arbitrary",)),
        name="moe_combine",
    )(slots, x2, route_p, g, ys)


def _final_norm_kernel(x_ref, g_ref, o_ref):
    o_ref[...] = _rms(x_ref[...], g_ref[...])


def _final_norm(x2, g, tq):
    t, d = x2.shape
    return pl.pallas_call(
        _final_norm_kernel,
        grid=(t // tq,),
        in_specs=[pl.BlockSpec((tq, d), lambda i: (i, 0)), pl.BlockSpec((1, d), lambda i: (0, 0))],
        out_specs=pl.BlockSpec((tq, d), lambda i: (i, 0)),
        out_shape=jax.ShapeDtypeStruct((t, d), F32),
        compiler_params=_params(("arbitrary",)),
        name="final_norm",
    )(x2, g)


def _rope_lane_tables(positions):
    inv_freq = ROPE_THETA ** (-jnp.arange(0, ROT_DIM, 2, dtype=F32) / ROT_DIM)
    ang = positions.astype(F32).reshape(-1, 1) * inv_freq
    cos, sin = jnp.cos(ang), jnp.sin(ang)
    t = ang.shape[0]
    rest = HEAD_DIM - ROT_DIM
    z8 = jnp.zeros((t, ROT_HALF), F32)
    cos_h = jnp.concatenate([cos, cos, jnp.ones((t, rest), F32)], axis=-1)
    lo_h = jnp.concatenate([-sin, z8, jnp.zeros((t, rest), F32)], axis=-1)
    hi_h = jnp.concatenate([z8, sin, jnp.zeros((t, rest), F32)], axis=-1)
    rep = LANES // HEAD_DIM
    return jnp.tile(cos_h, (1, rep)), jnp.tile(lo_h, (1, rep)), jnp.tile(hi_h, (1, rep))


def _moe_plan(route_i, counts, t, tm):
    cnt = counts[0, :N_EXPERTS].astype(jnp.int32)
    padded = ((cnt + tm - 1) // tm) * tm
    ends = jnp.cumsum(padded)
    offs = ends - padded
    slot = offs[route_i[:, 0:TOP_K]] + route_i[:, TOP_K:2 * TOP_K]
    ntile = (TOP_K * t) // tm + N_EXPERTS
    starts = jnp.arange(ntile, dtype=jnp.int32) * tm
    valid = (starts < ends[-1]).astype(jnp.int32)
    last = jnp.maximum(ends[-1] // tm - 1, 0)
    src = jnp.minimum(jnp.arange(ntile, dtype=jnp.int32), last)
    expert = jnp.sum(((src * tm)[:, None] >= ends[None, :]).astype(jnp.int32), axis=1)
    expert = jnp.minimum(expert, N_EXPERTS - 1)
    last_tile = jnp.where(padded > 0, ends // tm - 1, ntile - 1).astype(jnp.int32)
    tail = jnp.arange((TOP_K * t) // tm, ntile, dtype=jnp.int32)
    return slot, expert, valid, src, jnp.concatenate([last_tile, tail]), ntile * tm


def kernel(x, positions, norm1_g, w_in, b_gate, diff_lambda, diff_subln_g, pool_w, pool_scale,
           w_proj_a, w_proj_b, w_proj_c, w_out, norm2_g, ffn_w_gate, ffn_w_up, ffn_w_down,
           moe_router, moe_w_gate, moe_w_up, moe_w_down, final_norm_g):
    b, s, d = x.shape
    t = b * s
    depth = w_in.shape[0]
    tq = min(512, s)
    tm = min(512, s)
    cf = 512
    x2 = x.reshape(t, d)
    cos_t, sinlo_t, sinhi_t = _rope_lane_tables(positions)
    out = None
    for l in range(depth):
        w_l = w_in[l]
        qkv = _qkv_proj(x2, norm1_g[l].reshape(1, d), w_l[:, :OFF_C].astype(BF16),
                        cos_t, sinlo_t, sinhi_t, tq)
        qa, ka, va = qkv[0:3], qkv[3:6], qkv[6:9]
        qb, kb, vb = qkv[9:12]
        oa, lse = [], []
        for g, (window, dil) in enumerate(DILATED_PATTERNS):
            assert window // dil == BLOCK
            o_g, lse_g = _dilated_attn(qa[g], ka[g], va[g], b, s, dil, 512)
            oa.append(o_g)
            lse.append(lse_g)
        lambda_init = 0.8 - 0.6 * math.exp(-0.3 * l)
        ob = _diff_attn(qb, kb, vb, diff_lambda[l], diff_subln_g[l].reshape(1, B_HEAD_WIDTH),
                        b, s, min(1024, s), lambda_init)

        dense = l % 2 == 0
        router = None
        if not dense:
            wr = jnp.zeros((d, LANES), F32).at[:, :N_EXPERTS].set(moe_router[l // 2])
            wr_hi = wr.astype(BF16)
            router = (wr_hi, (wr - wr_hi.astype(F32)).astype(BF16))
        merged = _merge(
            x2, norm1_g[l].reshape(1, d), w_l[:, OFF_C:OFF_G].astype(BF16), w_l[:, OFF_G:].astype(BF16),
            b_gate[l].reshape(1, N_BRANCH * d), oa, lse, ob, pool_w[l].astype(BF16),
            pool_scale[l].reshape(1, C_WIDTH), w_proj_a[l].astype(BF16), w_proj_b[l].astype(BF16),
            w_proj_c[l].astype(BF16), w_out[l].astype(BF16), norm2_g[l].reshape(1, d),
            router, s, tq, BF16 if dense else F32)
        if dense:
            xn, h2 = merged
            i = l // 2
            ntile = t // tm
            ident = jnp.arange(ntile, dtype=jnp.int32)
            x2 = _grouped_swiglu(h2, xn, ffn_w_gate[i:i + 1].astype(BF16), ffn_w_up[i:i + 1].astype(BF16),
                                 ffn_w_down[i:i + 1].astype(BF16), jnp.zeros((ntile,), jnp.int32),
                                 jnp.ones((ntile,), jnp.int32), ident, tm, cf)
            out = None
        else:
            xn, h2, route_i, route_p, counts = merged
            i = l // 2
            slot, expert, valid, src, zero_tiles, nslot = _moe_plan(route_i, counts, t, tm)
            slots = slot.reshape(t // tq, tq, TOP_K).transpose(0, 2, 1)
            xs = _moe_dispatch(h2, slots, zero_tiles, nslot, tq, tm)
            ys = _grouped_swiglu(xs, None, moe_w_gate[i].astype(BF16), moe_w_up[i].astype(BF16),
                                 moe_w_down[i].astype(BF16), expert, valid, src, tm, cf)
            last = l == depth - 1
            res = _moe_combine(xn, route_p, slots, ys, final_norm_g.reshape(1, d), tq, last)
            if last:
                out = res
            else:
                x2 = res
    if out is None:
        out = _final_norm(x2, final_norm_g.reshape(1, d), tq)
    return out.reshape(b, s, d)
```

```python
import functools
import math

import jax
import jax.numpy as jnp
from jax import lax
from jax.experimental import pallas as pl
from jax.experimental.pallas import tpu as pltpu

F32 = jnp.float32
BF16 = jnp.bfloat16

HEAD_DIM = 64
ROPE_THETA = 500000.0
ROT_DIM = HEAD_DIM // 4
ROT_HALF = ROT_DIM // 2
BLOCK = 128
EPS = 1e-6
NEG = -1e30

DILATED_PATTERNS = ((128, 1), (512, 4), (2048, 16))
A_GROUPS = len(DILATED_PATTERNS)
A_HEADS_PER_GROUP = 4
A_GROUP_WIDTH = A_HEADS_PER_GROUP * HEAD_DIM
A_WIDTH = A_GROUPS * A_GROUP_WIDTH

B_HEADS = 4
B_HEAD_WIDTH = 2 * HEAD_DIM
B_WIDTH = B_HEADS * B_HEAD_WIDTH
B_VEXT = 2 * B_HEAD_WIDTH

POOL_WINDOWS = (2, 4, 8, 16)
C_GROUP_DIM = 128
C_WIDTH = len(POOL_WINDOWS) * C_GROUP_DIM
POOL_HALO = 16

N_BRANCH = 3
N_EXPERTS = 8
TOP_K = 2

OFF_QA = 0
OFF_KA = OFF_QA + A_WIDTH
OFF_VA = OFF_KA + A_WIDTH
OFF_QB = OFF_VA + A_WIDTH
OFF_KB = OFF_QB + B_WIDTH
OFF_VB = OFF_KB + B_WIDTH
OFF_C = OFF_VB + B_WIDTH
OFF_G = OFF_C + C_WIDTH

LANES = 128
MXU_WIDTH = 256
V7X_VMEM_BYTES = 64 * 1024 * 1024
VMEM_LIMIT = V7X_VMEM_BYTES - 8 * 1024 * 1024

QK_SCALE = HEAD_DIM ** -0.5
LOG2_E = math.log2(math.e)


def _params(semantics):
    return pltpu.CompilerParams(dimension_semantics=semantics, vmem_limit_bytes=VMEM_LIMIT)


def _rms(x, g):
    return x * lax.rsqrt(jnp.mean(x * x, axis=-1, keepdims=True) + EPS) * g


def _dot(a, b):
    return jnp.dot(a, b, preferred_element_type=F32)


def _dot_nt(a, b):
    return lax.dot_general(a, b, (((1,), (1,)), ((), ())), preferred_element_type=F32)


def _qkv_kernel(x_ref, g_ref, w_ref, cos_ref, sinlo_ref, sinhi_ref,
                qa0, qa1, qa2, ka0, ka1, ka2, va0, va1, va2, qb, kb, vb, zs_ref):
    tq = x_ref.shape[0]
    hb = _rms(x_ref[...], g_ref[...]).astype(BF16)
    cos = cos_ref[...]
    sinlo = sinlo_ref[...]
    sinhi = sinhi_ref[...]

    def rope(z):
        return (z * cos + pltpu.roll(z, LANES - ROT_HALF, 1) * sinlo
                + pltpu.roll(z, ROT_HALF, 1) * sinhi)

    def project(off, width, out_ref, rotary, scale, dil=1):
        for c in range(0, width, MXU_WIDTH):
            zz = _dot(hb, w_ref[:, off + c:off + c + MXU_WIDTH])
            for p in range(0, MXU_WIDTH, LANES):
                z = zz[:, p:p + LANES]
                if rotary:
                    z = rope(z)
                if scale != 1.0:
                    z = z * scale
                if dil == 1:
                    out_ref[:, c + p:c + p + LANES] = z.astype(out_ref.dtype)
                else:
                    zs_ref[(c + p) // LANES] = z
        if dil > 1:
            for r in range(dil):
                for sl in range(width // LANES):
                    out_ref[:, r * width + sl * LANES:r * width + (sl + 1) * LANES] = (
                        zs_ref[sl, pl.ds(r, tq // dil, stride=dil), :].astype(out_ref.dtype))

    for g, (q_ref, k_ref, v_ref) in enumerate(((qa0, ka0, va0), (qa1, ka1, va1), (qa2, ka2, va2))):
        dil = DILATED_PATTERNS[g][1]
        project(OFF_QA + g * A_GROUP_WIDTH, A_GROUP_WIDTH, q_ref, True, QK_SCALE * LOG2_E, dil)
        project(OFF_KA + g * A_GROUP_WIDTH, A_GROUP_WIDTH, k_ref, True, 1.0, dil)
        project(OFF_VA + g * A_GROUP_WIDTH, A_GROUP_WIDTH, v_ref, False, 1.0, dil)
    project(OFF_QB, B_WIDTH, qb, True, QK_SCALE * LOG2_E)
    project(OFF_KB, B_WIDTH, kb, True, 1.0)
    lane = lax.broadcasted_iota(jnp.int32, (x_ref.shape[0], B_HEAD_WIDTH), 1)
    ones_col = jnp.where(lane == 0, 1.0, 0.0).astype(vb.dtype)
    heads_per_dot = MXU_WIDTH // B_HEAD_WIDTH
    for h0 in range(0, B_HEADS, heads_per_dot):
        zz = _dot(hb, w_ref[:, OFF_VB + h0 * B_HEAD_WIDTH:OFF_VB + (h0 + heads_per_dot) * B_HEAD_WIDTH])
        for hh in range(heads_per_dot):
            h = h0 + hh
            z = zz[:, hh * B_HEAD_WIDTH:(hh + 1) * B_HEAD_WIDTH]
            vb[:, h * B_VEXT:h * B_VEXT + B_HEAD_WIDTH] = z.astype(vb.dtype)
            vb[:, h * B_VEXT + B_HEAD_WIDTH:(h + 1) * B_VEXT] = ones_col


def _qkv_proj(x2, g, w_qkv, cos_t, sinlo_t, sinhi_t, tq):
    t, d = x2.shape
    row = lambda i: (i, 0)
    const = lambda i: (0, 0)
    dils = [dil for _ in range(3) for (_, dil) in DILATED_PATTERNS]
    shapes = [(t // dil, dil * A_GROUP_WIDTH, tq // dil) for dil in dils]
    shapes += [(t, B_WIDTH, tq), (t, B_WIDTH, tq), (t, B_HEADS * B_VEXT, tq)]
    return pl.pallas_call(
        _qkv_kernel,
        grid=(t // tq,),
        in_specs=[
            pl.BlockSpec((tq, d), row),
            pl.BlockSpec((1, d), const),
            pl.BlockSpec(w_qkv.shape, const, pipeline_mode=pl.Buffered(1)),
            pl.BlockSpec((tq, LANES), row),
            pl.BlockSpec((tq, LANES), row),
            pl.BlockSpec((tq, LANES), row),
        ],
        out_specs=[pl.BlockSpec((rows, w), row) for (_, w, rows) in shapes],
        out_shape=[jax.ShapeDtypeStruct((n, w), BF16) for (n, w, _) in shapes],
        scratch_shapes=[pltpu.VMEM((A_GROUP_WIDTH // LANES, tq, LANES), F32)],
        compiler_params=_params(("arbitrary",)),
        name="qkv_proj",
    )(x2, g, w_qkv, cos_t, sinlo_t, sinhi_t)


def _dilated_kernel(q_ref, kp_ref, kc_ref, vp_ref, vc_ref, o_ref, lse_ref, *, nsub):
    n = pl.program_id(2)
    a = lax.broadcasted_iota(jnp.int32, (BLOCK, 2 * BLOCK), 0)
    j = lax.broadcasted_iota(jnp.int32, (BLOCK, 2 * BLOCK), 1)
    band = (j >= a) & (j <= a + BLOCK)
    band_first = band & ((j >= BLOCK) | (n > 0))
    low_k = lax.broadcasted_iota(jnp.int32, (2 * BLOCK, LANES), 1) < HEAD_DIM
    low_o = lax.broadcasted_iota(jnp.int32, (BLOCK, LANES), 1) < HEAD_DIM
    ones_slab = jnp.ones((2 * BLOCK, LANES), BF16)
    for sb in range(nsub):
        rows = slice(sb * BLOCK, (sb + 1) * BLOCK)
        if sb == 0:
            kcat = jnp.concatenate([kp_ref[...], kc_ref[rows, :]], axis=0)
            vcat = jnp.concatenate([vp_ref[...], vc_ref[rows, :]], axis=0)
            mask = band_first
        else:
            kcat = kc_ref[(sb - 1) * BLOCK:(sb + 1) * BLOCK, :]
            vcat = vc_ref[(sb - 1) * BLOCK:(sb + 1) * BLOCK, :]
            mask = band
        q = q_ref[rows, :]
        for hp in range(A_HEADS_PER_GROUP // 2):
            pair = slice(hp * LANES, (hp + 1) * LANES)
            q_pair = q[:, pair]
            k_pair = kcat[:, pair]
            v_ext = jnp.concatenate([vcat[:, pair], ones_slab], axis=1)
            o_half, lse_half = [], []
            for half in range(2):
                k_h = jnp.where(low_k if half == 0 else ~low_k, k_pair, jnp.zeros_like(k_pair))
                s = jnp.where(mask, _dot_nt(q_pair, k_h), NEG)
                m = jnp.max(jnp.maximum(s[:, :LANES], s[:, LANES:]), axis=-1, keepdims=True)
                p = jnp.concatenate([jnp.exp2(s[:, :LANES] - m), jnp.exp2(s[:, LANES:] - m)], axis=1)
                pv = _dot(p.astype(BF16), v_ext)
                l = pv[:, LANES:]
                o_half.append(pv[:, :LANES] / l)
                lse_half.append(m + jnp.log2(l))
            o_ref[rows, pair] = jnp.where(low_o, o_half[0], o_half[1]).astype(o_ref.dtype)
            lse_ref[rows, pair] = jnp.where(low_o, lse_half[0], lse_half[1])


def _dilated_attn(q, k, v, b, s, dil, qrows):
    w = A_GROUP_WIDTH
    l = s // dil
    qrows = min(qrows, l)
    nsub = qrows // BLOCK
    view = lambda t: t.reshape(b, l, dil * w)
    cur = lambda bi, r, n: (bi, n, r)
    prev = lambda bi, r, n: (bi, jnp.maximum(n * nsub - 1, 0), r)
    o, lse = pl.pallas_call(
        functools.partial(_dilated_kernel, nsub=nsub),
        grid=(b, dil, l // qrows),
        in_specs=[
            pl.BlockSpec((None, qrows, w), cur),
            pl.BlockSpec((None, BLOCK, w), prev),
            pl.BlockSpec((None, qrows, w), cur),
            pl.BlockSpec((None, BLOCK, w), prev),
            pl.BlockSpec((None, qrows, w), cur),
        ],
        out_specs=[pl.BlockSpec((None, qrows, w), cur), pl.BlockSpec((None, qrows, w), cur)],
        out_shape=[jax.ShapeDtypeStruct((b, l, dil * w), BF16),
                   jax.ShapeDtypeStruct((b, l, dil * w), F32)],
        compiler_params=_params(("arbitrary", "arbitrary", "arbitrary")),
        name=f"dilated_attn_d{dil}",
    )(view(q), view(k), view(k), view(v), view(v))
    return o.reshape(b * l, dil * w), lse.reshape(b * l, dil * w)


def _diff_kernel(lam_ref, g_ref, q_ref, k_ref, v_ref, o_ref, acc_ref, m_ref, sa_ref, sb_ref,
                 *, tq, rows, lambda_init):
    i = pl.program_id(2)
    tk = tq // 2
    q = q_ref[...]
    acc_ref[...] = jnp.zeros_like(acc_ref)
    m_ref[...] = jnp.full_like(m_ref, NEG)
    streams = [(mp, rb) for rb in range(tq // rows) for mp in range(2)]

    def visible(rb, diag):
        if diag is None:
            return tk
        return max(0, min(tk, (rb + 1) * rows - diag * tk))

    def scores(jc, dst_ref, diag=None):
        k = k_ref[pl.ds(pl.multiple_of(jc * tk, tk), tk), :]
        for mp, rb in streams:
            if visible(rb, diag) == 0:
                continue
            cols = slice(mp * HEAD_DIM, (mp + 1) * HEAD_DIM)
            rs = slice(rb * rows, (rb + 1) * rows)
            dst_ref[mp, rs, :] = _dot_nt(q[rs, cols], k[:, cols])

    def consume(jc, src_ref, diag=None):
        start = pl.multiple_of(jc * tk, tk)
        for mp, rb in streams:
            vis = visible(rb, diag)
            if vis == 0:
                continue
            rs = slice(rb * rows, (rb + 1) * rows)
            s = src_ref[mp, rs, :vis]
            if diag is not None and diag * tk + vis > rb * rows + 1:
                r = lax.broadcasted_iota(jnp.int32, (rows, vis), 0) + rb * rows
                c = lax.broadcasted_iota(jnp.int32, (rows, vis), 1) + diag * tk
                s = jnp.where(c <= r, s, NEG)
            slabs = [s[:, n * LANES:(n + 1) * LANES] for n in range(vis // LANES)]
            smax = slabs[0]
            for sl in slabs[1:]:
                smax = jnp.maximum(smax, sl)
            m_old = m_ref[mp, rs]
            m_new = jnp.maximum(m_old, jnp.max(smax, axis=-1, keepdims=True))
            alpha = jnp.exp2(m_old - m_new)
            p = jnp.concatenate([jnp.exp2(sl - m_new) for sl in slabs], axis=1).astype(BF16)
            pv = _dot(p, v_ref[pl.ds(start, vis), :])
            acc = acc_ref[mp, rs]
            acc_ref[mp, rs] = jnp.concatenate(
                [acc[:, n * LANES:(n + 1) * LANES] * alpha for n in range(B_VEXT // LANES)], axis=1) + pv
            m_ref[mp, rs] = m_new

    scores(0, sa_ref)

    def pair(t, carry):
        jc = 2 * t
        scores(jc + 1, sb_ref)
        consume(jc, sa_ref)
        scores(jc + 2, sa_ref)
        consume(jc + 1, sb_ref)
        return carry

    lax.fori_loop(0, i, pair, 0)
    scores(2 * i + 1, sb_ref, diag=1)
    consume(2 * i, sa_ref, diag=0)
    consume(2 * i + 1, sb_ref, diag=1)

    lp = lam_ref[...]
    lam = (jnp.exp(jnp.sum(lp[0:1] * lp[1:2], axis=-1, keepdims=True))
           - jnp.exp(jnp.sum(lp[2:3] * lp[3:4], axis=-1, keepdims=True)) + lambda_init)
    acc1 = acc_ref[0]
    acc2 = acc_ref[1]
    o1 = acc1[:, :B_HEAD_WIDTH] / acc1[:, B_HEAD_WIDTH:B_HEAD_WIDTH + 1]
    o2 = acc2[:, :B_HEAD_WIDTH] / acc2[:, B_HEAD_WIDTH:B_HEAD_WIDTH + 1]
    o = o1 - lam * o2
    o_ref[...] = (_rms(o, g_ref[...]) * (1.0 - lambda_init)).astype(o_ref.dtype)


def _diff_attn(qb, kb, vb, lam_p, subln_g, b, s, tq, lambda_init):
    hw = B_HEAD_WIDTH
    q3 = qb.reshape(b, s, B_WIDTH)
    k3 = kb.reshape(b, s, B_WIDTH)
    v3 = vb.reshape(b, s, B_HEADS * B_VEXT)
    out = pl.pallas_call(
        functools.partial(_diff_kernel, tq=tq, rows=min(256, tq // 2), lambda_init=lambda_init),
        grid=(b, B_HEADS, s // tq),
        in_specs=[
            pl.BlockSpec(lam_p.shape, lambda bi, h, i: (0, 0)),
            pl.BlockSpec((1, hw), lambda bi, h, i: (0, 0)),
            pl.BlockSpec((None, tq, hw), lambda bi, h, i: (bi, i, h)),
            pl.BlockSpec((None, s, hw), lambda bi, h, i: (bi, 0, h)),
            pl.BlockSpec((None, s, B_VEXT), lambda bi, h, i: (bi, 0, h)),
        ],
        out_specs=pl.BlockSpec((None, tq, hw), lambda bi, h, i: (bi, i, h)),
        out_shape=jax.ShapeDtypeStruct((b, s, B_WIDTH), BF16),
        scratch_shapes=[pltpu.VMEM((2, tq, B_VEXT), F32), pltpu.VMEM((2, tq, LANES), F32),
                        pltpu.VMEM((2, tq, tq // 2), F32), pltpu.VMEM((2, tq, tq // 2), F32)],
        compiler_params=_params(("arbitrary", "arbitrary", "arbitrary")),
        name="diff_attn",
    )(lam_p, subln_g, q3, k3, v3)
    return out.reshape(b * s, B_WIDTH)


def _merge_kernel(*refs, tq, tiles_per_seq, route):
    (x_ref, g1_ref, wc_ref, wg_ref, bg_ref, oa0, oa1, oa2, ls0, ls1, ls2, ob_ref,
     pw_ref, ps_ref, wpa_ref, wpb_ref, wpc_ref, wo_ref, g2_ref) = refs[:19]
    if route:
        wrh_ref, wrl_ref = refs[19:21]
        xo_ref, h2_ref, ri_ref, rp_ref, cnt_ref, zc_ref, tm_ref, carry_ref = refs[21:]
    else:
        xo_ref, h2_ref, zc_ref, tm_ref = refs[19:]
    step = pl.program_id(0)
    seq_tile = step % tiles_per_seq

    x = x_ref[...]
    hb = _rms(x, g1_ref[...]).astype(BF16)

    @pl.when(seq_tile == 0)
    def _():
        zc_ref[0:POOL_HALO, :] = jnp.zeros((POOL_HALO, C_WIDTH), F32)

    zc_ref[POOL_HALO:POOL_HALO + tq, :] = _dot(hb, wc_ref[...])
    pos = seq_tile * tq + lax.broadcasted_iota(jnp.int32, (tq, C_GROUP_DIM), 0)
    pooled = []
    for gi, win in enumerate(POOL_WINDOWS):
        cols = slice(gi * C_GROUP_DIM, (gi + 1) * C_GROUP_DIM)
        tok = zc_ref[POOL_HALO:POOL_HALO + tq, cols]
        tot = tok
        for back in range(1, win):
            tot = tot + zc_ref[POOL_HALO - back:POOL_HALO - back + tq, cols]
        cnt = jnp.minimum(pos + 1, win).astype(F32)
        d = tot / cnt - tok
        pooled.append(_dot(d.astype(BF16), pw_ref[gi]))
    out_c = (jnp.concatenate(pooled, axis=-1) * ps_ref[...]).astype(BF16)
    zc_ref[0:POOL_HALO, :] = zc_ref[tq:tq + POOL_HALO, :]

    def token_major(src_ref, slot, dil):
        if dil == 1:
            return src_ref[...].astype(F32)
        slabs = A_GROUP_WIDTH // LANES
        for r in range(dil):
            for sl in range(slabs):
                col = r * A_GROUP_WIDTH + sl * LANES
                tm_ref[slot * slabs + sl, pl.ds(r, tq // dil, stride=dil), :] = (
                    src_ref[:, col:col + LANES].astype(F32))
        return jnp.concatenate([tm_ref[slot * slabs + sl] for sl in range(slabs)], axis=1)

    dils = [dil for (_, dil) in DILATED_PATTERNS]
    l0, l1, l2 = (token_major(ref, gi, dil) for gi, (ref, dil) in enumerate(zip((ls0, ls1, ls2), dils)))
    o0, o1, o2 = (token_major(ref, A_GROUPS + gi, dil)
                  for gi, (ref, dil) in enumerate(zip((oa0, oa1, oa2), dils)))
    lm = jnp.maximum(jnp.maximum(l0, l1), l2)
    e0, e1, e2 = jnp.exp2(l0 - lm), jnp.exp2(l1 - lm), jnp.exp2(l2 - lm)
    out_a = ((e0 * o0 + e1 * o1 + e2 * o2) / (e0 + e1 + e2)).astype(BF16)

    d = x.shape[-1]
    branches = (_dot(out_a, wpa_ref[...]), _dot(ob_ref[...], wpb_ref[...]), _dot(out_c, wpc_ref[...]))
    mixed = None
    for bi, proj in enumerate(branches):
        zg = _dot(hb, wg_ref[:, bi * d:(bi + 1) * d]) + bg_ref[:, bi * d:(bi + 1) * d]
        term = jax.nn.sigmoid(zg) * proj
        mixed = term if mixed is None else mixed + term
    xn = x + _dot(mixed.astype(BF16), wo_ref[...])
    xo_ref[...] = xn
    h2 = _rms(xn, g2_ref[...])
    h2_ref[...] = h2.astype(h2_ref.dtype)

    if route:
        h_hi = h2.astype(BF16)
        h_lo = (h2 - h_hi.astype(F32)).astype(BF16)
        logits = _dot(h_hi, wrh_ref[...]) + (_dot(h_lo, wrh_ref[...]) + _dot(h_hi, wrl_ref[...]))
        lane = lax.broadcasted_iota(jnp.int32, (tq, LANES), 1)
        lane_f = lane.astype(F32)
        logits = jnp.where(lane < N_EXPERTS, logits, -jnp.inf)
        m1 = jnp.max(logits, axis=-1, keepdims=True)
        i1 = jnp.min(jnp.where(logits == m1, lane_f, float(LANES)), axis=-1, keepdims=True)
        rest = jnp.where(lane_f == i1, -jnp.inf, logits)
        m2 = jnp.max(rest, axis=-1, keepdims=True)
        i2 = jnp.min(jnp.where(rest == m2, lane_f, float(LANES)), axis=-1, keepdims=True)
        e21 = jnp.exp(m2 - m1)
        p1 = 1.0 / (1.0 + e21)
        p2 = e21 * p1
        sel1 = lane_f == i1
        sel2 = lane_f == i2
        chosen = jnp.where(sel1 | sel2, 1.0, 0.0)

        @pl.when(step == 0)
        def _():
            carry_ref[...] = jnp.zeros_like(carry_ref)

        r = lax.broadcasted_iota(jnp.int32, (tq, tq), 0)
        c = lax.broadcasted_iota(jnp.int32, (tq, tq), 1)
        before = _dot(jnp.where(c < r, 1.0, 0.0).astype(BF16), chosen.astype(BF16)) + carry_ref[...]
        rank1 = jnp.sum(jnp.where(sel1, before, 0.0), axis=-1, keepdims=True)
        rank2 = jnp.sum(jnp.where(sel2, before, 0.0), axis=-1, keepdims=True)
        carry_ref[...] = carry_ref[...] + jnp.sum(chosen, axis=0, keepdims=True)
        packed = jnp.where(lane == 0, i1, jnp.where(lane == 1, i2,
                           jnp.where(lane == 2, rank1, jnp.where(lane == 3, rank2, 0.0))))
        ri_ref[...] = packed.astype(jnp.int32)
        rp_ref[...] = jnp.where(lane == 0, p1, jnp.where(lane == 1, p2, 0.0))
        cnt_ref[...] = jnp.broadcast_to(carry_ref[...], cnt_ref.shape)


def _merge(x2, g1, w_c, w_g, b_g, oa, lse, ob, pool_w, pool_scale, wpa, wpb, wpc, wo, g2,
           router, s, tq, h2_dtype):
    t, d = x2.shape
    route = router is not None
    row = lambda i: (i, 0)
    const = lambda i: (0, 0)
    resident = lambda a: pl.BlockSpec(a.shape, lambda i: (0,) * a.ndim, pipeline_mode=pl.Buffered(1))
    args = [x2, g1, w_c, w_g, b_g, *oa, *lse, ob, pool_w, pool_scale, wpa, wpb, wpc, wo, g2]
    in_specs = [pl.BlockSpec((tq, d), row), pl.BlockSpec((1, d), const), resident(w_c), resident(w_g),
                pl.BlockSpec(b_g.shape, const)]
    in_specs += [pl.BlockSpec((tq // dil, dil * A_GROUP_WIDTH), row) for (_, dil) in DILATED_PATTERNS] * 2
    in_specs += [pl.BlockSpec((tq, B_WIDTH), row), resident(pool_w), pl.BlockSpec(pool_scale.shape, const),
                 resident(wpa), resident(wpb), resident(wpc), resident(wo), pl.BlockSpec((1, d), const)]
    out_specs = [pl.BlockSpec((tq, d), row), pl.BlockSpec((tq, d), row)]
    out_shape = [jax.ShapeDtypeStruct((t, d), F32), jax.ShapeDtypeStruct((t, d), h2_dtype)]
    scratch = [pltpu.VMEM((POOL_HALO + tq, C_WIDTH), F32), pltpu.VMEM((2 * A_GROUPS * (A_GROUP_WIDTH // LANES), tq, LANES), F32)]
    if route:
        args += list(router)
        in_specs += [resident(router[0]), resident(router[1])]
        out_specs += [pl.BlockSpec((tq, LANES), row), pl.BlockSpec((tq, LANES), row),
                      pl.BlockSpec((8, LANES), const)]
        out_shape += [jax.ShapeDtypeStruct((t, LANES), jnp.int32), jax.ShapeDtypeStruct((t, LANES), F32),
                      jax.ShapeDtypeStruct((8, LANES), F32)]
        scratch += [pltpu.VMEM((1, LANES), F32)]
    return pl.pallas_call(
        functools.partial(_merge_kernel, tq=tq, tiles_per_seq=s // tq, route=route),
        grid=(t // tq,),
        in_specs=in_specs,
        out_specs=out_specs,
        out_shape=out_shape,
        scratch_shapes=scratch,
        compiler_params=_params(("arbitrary",)),
        name="merge_route" if route else "merge",
    )(*args)


def _swiglu_kernel(te_ref, tv_ref, ts_ref, *refs, residual):
    if residual:
        x_ref, res_ref, wg_ref, wu_ref, wd_ref, o_ref, xb_ref, acc_ref = refs
    else:
        x_ref, wg_ref, wu_ref, wd_ref, o_ref, xb_ref, acc_ref = refs
    j = pl.program_id(0)
    c = pl.program_id(1)

    @pl.when((tv_ref[j] == 0) & (c == 0))
    def _():
        o_ref[...] = jnp.zeros_like(o_ref)

    @pl.when(tv_ref[j] > 0)
    def _():
        @pl.when(c == 0)
        def _():
            xb_ref[...] = x_ref[...].astype(BF16)
            acc_ref[...] = jnp.zeros_like(acc_ref)

        xb = xb_ref[...]
        gate = _dot(xb, wg_ref[...])
        up = _dot(xb, wu_ref[...])
        mid = (gate * jax.nn.sigmoid(gate) * up).astype(BF16)
        acc_ref[...] += _dot(mid, wd_ref[...])

        @pl.when(c == pl.num_programs(1) - 1)
        def _():
            if residual:
                o_ref[...] = res_ref[...] + acc_ref[...]
            else:
                o_ref[...] = acc_ref[...]


def _grouped_swiglu(xs, res, wg, wu, wd, tile_expert, tile_valid, tile_src, tm, cf):
    n, d = xs.shape
    ff = wg.shape[-1]
    residual = res is not None
    xmap = lambda j, c, te, tv, ts: (ts[j], 0)
    in_specs = [pl.BlockSpec((tm, d), xmap)]
    args = [xs]
    if residual:
        in_specs.append(pl.BlockSpec((tm, d), xmap))
        args.append(res)
    in_specs += [
        pl.BlockSpec((None, d, cf), lambda j, c, te, tv, ts: (te[j], 0, c * tv[j])),
        pl.BlockSpec((None, d, cf), lambda j, c, te, tv, ts: (te[j], 0, c * tv[j])),
        pl.BlockSpec((None, cf, d), lambda j, c, te, tv, ts: (te[j], c * tv[j], 0)),
    ]
    args += [wg, wu, wd]
    return pl.pallas_call(
        functools.partial(_swiglu_kernel, residual=residual),
        grid_spec=pltpu.PrefetchScalarGridSpec(
            num_scalar_prefetch=3,
            grid=(n // tm, ff // cf),
            in_specs=in_specs,
            out_specs=pl.BlockSpec((tm, d), lambda j, c, te, tv, ts: (j, 0)),
            scratch_shapes=[pltpu.VMEM((tm, d), BF16), pltpu.VMEM((tm, d), F32)],
        ),
        out_shape=jax.ShapeDtypeStruct((n, d), F32),
        compiler_params=_params(("arbitrary", "arbitrary")),
        name="grouped_swiglu_res" if residual else "grouped_swiglu",
    )(tile_expert, tile_valid, tile_src, *args)


def _dispatch_kernel(zt_ref, slot_ref, h_ref, xs_ref, zero_ref, sem, zsem, *, tq, tm):
    @pl.when(pl.program_id(0) == 0)
    def _():
        zero_ref[...] = jnp.zeros_like(zero_ref)
        for z in range(zt_ref.shape[0]):
            start = pl.multiple_of(zt_ref[z] * tm, tm)
            fill = pltpu.make_async_copy(zero_ref, xs_ref.at[pl.ds(start, tm), :], zsem.at[0])
            fill.start()
            fill.wait()

    def row_copy(r, k):
        return pltpu.make_async_copy(h_ref.at[pl.ds(r, 1), :],
                                     xs_ref.at[pl.ds(slot_ref[0, k, r], 1), :], sem.at[k])

    def issue(r, carry):
        row_copy(r, 0).start()
        row_copy(r, 1).start()
        return carry

    lax.fori_loop(0, tq, issue, 0)
    for k in range(TOP_K):
        pltpu.make_async_copy(h_ref, xs_ref.at[pl.ds(0, tq), :], sem.at[k]).wait()


def _moe_dispatch(h2, slots, zero_tiles, nslot, tq, tm):
    t, d = h2.shape
    return pl.pallas_call(
        functools.partial(_dispatch_kernel, tq=tq, tm=tm),
        grid_spec=pltpu.PrefetchScalarGridSpec(
            num_scalar_prefetch=1,
            grid=(t // tq,),
            in_specs=[
                pl.BlockSpec((1, TOP_K, tq), lambda i, zt: (i, 0, 0), memory_space=pltpu.SMEM),
                pl.BlockSpec((tq, d), lambda i, zt: (i, 0)),
            ],
            out_specs=pl.BlockSpec(memory_space=pl.ANY),
            scratch_shapes=[pltpu.VMEM((tm, d), h2.dtype), pltpu.SemaphoreType.DMA((TOP_K,)),
                            pltpu.SemaphoreType.DMA((1,))],
        ),
        out_shape=jax.ShapeDtypeStruct((nslot, d), h2.dtype),
        compiler_params=_params(("arbitrary",)),
        name="moe_dispatch",
    )(zero_tiles, slots, h2)


def _combine_kernel(slot_ref, x_ref, rp_ref, g_ref, ys_ref, o_ref, y_ref, sem, *, tq, normalize):
    def row_copy(r, k):
        return pltpu.make_async_copy(ys_ref.at[pl.ds(slot_ref[0, k, r], 1), :],
                                     y_ref.at[k, pl.ds(r, 1), :], sem.at[k])

    def issue(r, carry):
        row_copy(r, 0).start()
        row_copy(r, 1).start()
        return carry

    lax.fori_loop(0, tq, issue, 0)
    for k in range(TOP_K):
        pltpu.make_async_copy(ys_ref.at[pl.ds(0, tq), :], y_ref.at[k], sem.at[k]).wait()
    rp = rp_ref[...]
    xn = x_ref[...] + rp[:, 0:1] * y_ref[0] + rp[:, 1:2] * y_ref[1]
    o_ref[...] = _rms(xn, g_ref[...]) if normalize else xn


def _moe_combine(x2, route_p, slots, ys, g, tq, normalize):
    t, d = x2.shape
    return pl.pallas_call(
        functools.partial(_combine_kernel, tq=tq, normalize=normalize),
        grid=(t // tq,),
        in_specs=[
            pl.BlockSpec((1, TOP_K, tq), lambda i: (i, 0, 0), memory_space=pltpu.SMEM),
            pl.BlockSpec((tq, d), lambda i: (i, 0)),
            pl.BlockSpec((tq, LANES), lambda i: (i, 0)),
            pl.BlockSpec((1, d), lambda i: (0, 0)),
            pl.BlockSpec(memory_space=pl.ANY),
        ],
        out_specs=pl.BlockSpec((tq, d), lambda i: (i, 0)),
        out_shape=jax.ShapeDtypeStruct((t, d), F32),
        scratch_shapes=[pltpu.VMEM((TOP_K, tq, d), F32), pltpu.SemaphoreType.DMA((TOP_K,))],
        compiler_params=_params(("arbitrary",)),
        name="moe_combine",
    )(slots, x2, route_p, g, ys)


def _final_norm_kernel(x_ref, g_ref, o_ref):
    o_ref[...] = _rms(x_ref[...], g_ref[...])


def _final_norm(x2, g, tq):
    t, d = x2.shape
    return pl.pallas_call(
        _final_norm_kernel,
        grid=(t // tq,),
        in_specs=[pl.BlockSpec((tq, d), lambda i: (i, 0)), pl.BlockSpec((1, d), lambda i: (0, 0))],
        out_specs=pl.BlockSpec((tq, d), lambda i: (i, 0)),
        out_shape=jax.ShapeDtypeStruct((t, d), F32),
        compiler_params=_params(("arbitrary",)),
        name="final_norm",
    )(x2, g)


def _rope_lane_tables(positions):
    inv_freq = ROPE_THETA ** (-jnp.arange(0, ROT_DIM, 2, dtype=F32) / ROT_DIM)
    ang = positions.astype(F32).reshape(-1, 1) * inv_freq
    cos, sin = jnp.cos(ang), jnp.sin(ang)
    t = ang.shape[0]
    rest = HEAD_DIM - ROT_DIM
    z8 = jnp.zeros((t, ROT_HALF), F32)
    cos_h = jnp.concatenate([cos, cos, jnp.ones((t, rest), F32)], axis=-1)
    lo_h = jnp.concatenate([-sin, z8, jnp.zeros((t, rest), F32)], axis=-1)
    hi_h = jnp.concatenate([z8, sin, jnp.zeros((t, rest), F32)], axis=-1)
    rep = LANES // HEAD_DIM
    return jnp.tile(cos_h, (1, rep)), jnp.tile(lo_h, (1, rep)), jnp.tile(hi_h, (1, rep))


def _moe_plan(route_i, counts, t, tm):
    cnt = counts[0, :N_EXPERTS].astype(jnp.int32)
    padded = ((cnt + tm - 1) // tm) * tm
    ends = jnp.cumsum(padded)
    offs = ends - padded
    slot = offs[route_i[:, 0:TOP_K]] + route_i[:, TOP_K:2 * TOP_K]
    ntile = (TOP_K * t) // tm + N_EXPERTS
    starts = jnp.arange(ntile, dtype=jnp.int32) * tm
    valid = (starts < ends[-1]).astype(jnp.int32)
    last = jnp.maximum(ends[-1] // tm - 1, 0)
    src = jnp.minimum(jnp.arange(ntile, dtype=jnp.int32), last)
    expert = jnp.sum(((src * tm)[:, None] >= ends[None, :]).astype(jnp.int32), axis=1)
    expert = jnp.minimum(expert, N_EXPERTS - 1)
    last_tile = jnp.where(padded > 0, ends // tm - 1, ntile - 1).astype(jnp.int32)
    tail = jnp.arange((TOP_K * t) // tm, ntile, dtype=jnp.int32)
    return slot, expert, valid, src, jnp.concatenate([last_tile, tail]), ntile * tm


def kernel(x, positions, norm1_g, w_in, b_gate, diff_lambda, diff_subln_g, pool_w, pool_scale,
           w_proj_a, w_proj_b, w_proj_c, w_out, norm2_g, ffn_w_gate, ffn_w_up, ffn_w_down,
           moe_router, moe_w_gate, moe_w_up, moe_w_down, final_norm_g):
    b, s, d = x.shape
    t = b * s
    depth = w_in.shape[0]
    tq = min(512, s)
    tm = min(512, s)
    ff = ffn_w_gate.shape[-1]
    cf = ff // 2 if ff % 512 == 0 and ff >= 1024 else ff
    x2 = x.reshape(t, d)
    cos_t, sinlo_t, sinhi_t = _rope_lane_tables(positions)
    out = None
    for l in range(depth):
        w_l = w_in[l]
        qkv = _qkv_proj(x2, norm1_g[l].reshape(1, d), w_l[:, :OFF_C].astype(BF16),
                        cos_t, sinlo_t, sinhi_t, tq)
        qa, ka, va = qkv[0:3], qkv[3:6], qkv[6:9]
        qb, kb, vb = qkv[9:12]
        oa, lse = [], []
        for g, (window, dil) in enumerate(DILATED_PATTERNS):
            assert window // dil == BLOCK
            o_g, lse_g = _dilated_attn(qa[g], ka[g], va[g], b, s, dil, 512)
            oa.append(o_g)
            lse.append(lse_g)
        lambda_init = 0.8 - 0.6 * math.exp(-0.3 * l)
        ob = _diff_attn(qb, kb, vb, diff_lambda[l], diff_subln_g[l].reshape(1, B_HEAD_WIDTH),
                        b, s, min(1024, s), lambda_init)

        dense = l % 2 == 0
        router = None
        if not dense:
            wr = jnp.zeros((d, LANES), F32).at[:, :N_EXPERTS].set(moe_router[l // 2])
            wr_hi = wr.astype(BF16)
            router = (wr_hi, (wr - wr_hi.astype(F32)).astype(BF16))
        merged = _merge(
            x2, norm1_g[l].reshape(1, d), w_l[:, OFF_C:OFF_G].astype(BF16), w_l[:, OFF_G:].astype(BF16),
            b_gate[l].reshape(1, N_BRANCH * d), oa, lse, ob, pool_w[l].astype(BF16),
            pool_scale[l].reshape(1, C_WIDTH), w_proj_a[l].astype(BF16), w_proj_b[l].astype(BF16),
            w_proj_c[l].astype(BF16), w_out[l].astype(BF16), norm2_g[l].reshape(1, d),
            router, s, tq, BF16 if dense else F32)
        if dense:
            xn, h2 = merged
            i = l // 2
            ntile = t // tm
            ident = jnp.arange(ntile, dtype=jnp.int32)
            x2 = _grouped_swiglu(h2, xn, ffn_w_gate[i:i + 1].astype(BF16), ffn_w_up[i:i + 1].astype(BF16),
                                 ffn_w_down[i:i + 1].astype(BF16), jnp.zeros((ntile,), jnp.int32),
                                 jnp.ones((ntile,), jnp.int32), ident, tm, cf)
            out = None
        else:
            xn, h2, route_i, route_p, counts = merged
            i = l // 2
            slot, expert, valid, src, zero_tiles, nslot = _moe_plan(route_i, counts, t, tm)
            slots = slot.reshape(t // tq, tq, TOP_K).transpose(0, 2, 1)
            xs = _moe_dispatch(h2, slots, zero_tiles, nslot, tq, tm)
            ys = _grouped_swiglu(xs, None, moe_w_gate[i].astype(BF16), moe_w_up[i].astype(BF16),
                                 moe_w_down[i].astype(BF16), expert, valid, src, tm, cf)
            last = l == depth - 1
            res = _moe_combine(xn, route_p, slots, ys, final_norm_g.reshape(1, d), tq, last)
            if last:
                out = res
            else:
                x2 = res
    if out is None:
        out = _final_norm(x2, final_norm_g.reshape(1, d), tq)
    return out.reshape(b, s, d)
```

```python
import functools
import math

import jax
import jax.numpy as jnp
from jax import lax
from jax.experimental import pallas as pl
from jax.experimental.pallas import tpu as pltpu

F32 = jnp.float32
BF16 = jnp.bfloat16

HEAD_DIM = 64
ROPE_THETA = 500000.0
ROT_DIM = HEAD_DIM // 4
ROT_HALF = ROT_DIM // 2
BLOCK = 128
EPS = 1e-6
NEG = -1e30

DILATED_PATTERNS = ((128, 1), (512, 4), (2048, 16))
A_GROUPS = len(DILATED_PATTERNS)
A_HEADS_PER_GROUP = 4
A_GROUP_WIDTH = A_HEADS_PER_GROUP * HEAD_DIM
A_WIDTH = A_GROUPS * A_GROUP_WIDTH

B_HEADS = 4
B_HEAD_WIDTH = 2 * HEAD_DIM
B_WIDTH = B_HEADS * B_HEAD_WIDTH
B_VEXT = 2 * B_HEAD_WIDTH

POOL_WINDOWS = (2, 4, 8, 16)
C_GROUP_DIM = 128
C_WIDTH = len(POOL_WINDOWS) * C_GROUP_DIM
POOL_HALO = 16

N_BRANCH = 3
N_EXPERTS = 8
TOP_K = 2

OFF_QA = 0
OFF_KA = OFF_QA + A_WIDTH
OFF_VA = OFF_KA + A_WIDTH
OFF_QB = OFF_VA + A_WIDTH
OFF_KB = OFF_QB + B_WIDTH
OFF_VB = OFF_KB + B_WIDTH
OFF_C = OFF_VB + B_WIDTH
OFF_G = OFF_C + C_WIDTH

LANES = 128
MXU_WIDTH = 256
ROW_TILE = 8
ISSUE_UNROLL = 8
V7X_VMEM_BYTES = 64 * 1024 * 1024
VMEM_LIMIT = V7X_VMEM_BYTES - 8 * 1024 * 1024

QK_SCALE = HEAD_DIM ** -0.5
LOG2_E = math.log2(math.e)


def _params(semantics):
    return pltpu.CompilerParams(dimension_semantics=semantics, vmem_limit_bytes=VMEM_LIMIT)


def _rms(x, g):
    return x * lax.rsqrt(jnp.mean(x * x, axis=-1, keepdims=True) + EPS) * g


def _dot(a, b):
    return jnp.dot(a, b, preferred_element_type=F32)


def _dot_nt(a, b):
    return lax.dot_general(a, b, (((1,), (1,)), ((), ())), preferred_element_type=F32)


def _qkv_kernel(x_ref, g_ref, w_ref, cos_ref, sinlo_ref, sinhi_ref,
                qa0, qa1, qa2, ka0, ka1, ka2, va0, va1, va2, qb, kb, vb, zs_ref):
    tq = x_ref.shape[0]
    hb = _rms(x_ref[...], g_ref[...]).astype(BF16)
    cos = cos_ref[...]
    sinlo = sinlo_ref[...]
    sinhi = sinhi_ref[...]

    def rope(z):
        return (z * cos + pltpu.roll(z, LANES - ROT_HALF, 1) * sinlo
                + pltpu.roll(z, ROT_HALF, 1) * sinhi)

    def project(off, width, out_ref, rotary, scale, dil=1):
        for c in range(0, width, MXU_WIDTH):
            zz = _dot(hb, w_ref[:, off + c:off + c + MXU_WIDTH])
            for p in range(0, MXU_WIDTH, LANES):
                z = zz[:, p:p + LANES]
                if rotary:
                    z = rope(z)
                if scale != 1.0:
                    z = z * scale
                if dil == 1:
                    out_ref[:, c + p:c + p + LANES] = z.astype(out_ref.dtype)
                else:
                    zs_ref[(c + p) // LANES] = z
        if dil > 1:
            for r in range(dil):
                for sl in range(width // LANES):
                    out_ref[:, r * width + sl * LANES:r * width + (sl + 1) * LANES] = (
                        zs_ref[sl, pl.ds(r, tq // dil, stride=dil), :].astype(out_ref.dtype))

    for g, (q_ref, k_ref, v_ref) in enumerate(((qa0, ka0, va0), (qa1, ka1, va1), (qa2, ka2, va2))):
        dil = DILATED_PATTERNS[g][1]
        project(OFF_QA + g * A_GROUP_WIDTH, A_GROUP_WIDTH, q_ref, True, QK_SCALE * LOG2_E, dil)
        project(OFF_KA + g * A_GROUP_WIDTH, A_GROUP_WIDTH, k_ref, True, 1.0, dil)
        project(OFF_VA + g * A_GROUP_WIDTH, A_GROUP_WIDTH, v_ref, False, 1.0, dil)
    project(OFF_QB, B_WIDTH, qb, True, QK_SCALE * LOG2_E)
    project(OFF_KB, B_WIDTH, kb, True, 1.0)
    lane = lax.broadcasted_iota(jnp.int32, (x_ref.shape[0], B_HEAD_WIDTH), 1)
    ones_col = jnp.where(lane == 0, 1.0, 0.0).astype(vb.dtype)
    heads_per_dot = MXU_WIDTH // B_HEAD_WIDTH
    for h0 in range(0, B_HEADS, heads_per_dot):
        zz = _dot(hb, w_ref[:, OFF_VB + h0 * B_HEAD_WIDTH:OFF_VB + (h0 + heads_per_dot) * B_HEAD_WIDTH])
        for hh in range(heads_per_dot):
            h = h0 + hh
            z = zz[:, hh * B_HEAD_WIDTH:(hh + 1) * B_HEAD_WIDTH]
            vb[:, h * B_VEXT:h * B_VEXT + B_HEAD_WIDTH] = z.astype(vb.dtype)
            vb[:, h * B_VEXT + B_HEAD_WIDTH:(h + 1) * B_VEXT] = ones_col


def _qkv_proj(x2, g, w_qkv, cos_t, sinlo_t, sinhi_t, tq):
    t, d = x2.shape
    row = lambda i: (i, 0)
    const = lambda i: (0, 0)
    dils = [dil for _ in range(3) for (_, dil) in DILATED_PATTERNS]
    shapes = [(t // dil, dil * A_GROUP_WIDTH, tq // dil) for dil in dils]
    shapes += [(t, B_WIDTH, tq), (t, B_WIDTH, tq), (t, B_HEADS * B_VEXT, tq)]
    return pl.pallas_call(
        _qkv_kernel,
        grid=(t // tq,),
        in_specs=[
            pl.BlockSpec((tq, d), row),
            pl.BlockSpec((1, d), const),
            pl.BlockSpec(w_qkv.shape, const, pipeline_mode=pl.Buffered(1)),
            pl.BlockSpec((tq, LANES), row),
            pl.BlockSpec((tq, LANES), row),
            pl.BlockSpec((tq, LANES), row),
        ],
        out_specs=[pl.BlockSpec((rows, w), row) for (_, w, rows) in shapes],
        out_shape=[jax.ShapeDtypeStruct((n, w), BF16) for (n, w, _) in shapes],
        scratch_shapes=[pltpu.VMEM((A_GROUP_WIDTH // LANES, tq, LANES), F32)],
        compiler_params=_params(("arbitrary",)),
        name="qkv_proj",
    )(x2, g, w_qkv, cos_t, sinlo_t, sinhi_t)


def _dilated_kernel(q_ref, kp_ref, kc_ref, vp_ref, vc_ref, o_ref, lse_ref, *, nsub):
    n = pl.program_id(2)
    a = lax.broadcasted_iota(jnp.int32, (BLOCK, 2 * BLOCK), 0)
    j = lax.broadcasted_iota(jnp.int32, (BLOCK, 2 * BLOCK), 1)
    band = (j >= a) & (j <= a + BLOCK)
    band_first = band & ((j >= BLOCK) | (n > 0))
    low_k = lax.broadcasted_iota(jnp.int32, (2 * BLOCK, LANES), 1) < HEAD_DIM
    low_o = lax.broadcasted_iota(jnp.int32, (BLOCK, LANES), 1) < HEAD_DIM
    ones_slab = jnp.ones((2 * BLOCK, LANES), BF16)
    for sb in range(nsub):
        rows = slice(sb * BLOCK, (sb + 1) * BLOCK)
        if sb == 0:
            kcat = jnp.concatenate([kp_ref[...], kc_ref[rows, :]], axis=0)
            vcat = jnp.concatenate([vp_ref[...], vc_ref[rows, :]], axis=0)
            mask = band_first
        else:
            kcat = kc_ref[(sb - 1) * BLOCK:(sb + 1) * BLOCK, :]
            vcat = vc_ref[(sb - 1) * BLOCK:(sb + 1) * BLOCK, :]
            mask = band
        q = q_ref[rows, :]
        for hp in range(A_HEADS_PER_GROUP // 2):
            pair = slice(hp * LANES, (hp + 1) * LANES)
            q_pair = q[:, pair]
            k_pair = kcat[:, pair]
            v_ext = jnp.concatenate([vcat[:, pair], ones_slab], axis=1)
            o_half, lse_half = [], []
            for half in range(2):
                k_h = jnp.where(low_k if half == 0 else ~low_k, k_pair, jnp.zeros_like(k_pair))
                s = jnp.where(mask, _dot_nt(q_pair, k_h), NEG)
                m = jnp.max(jnp.maximum(s[:, :LANES], s[:, LANES:]), axis=-1, keepdims=True)
                p = jnp.concatenate([jnp.exp2(s[:, :LANES] - m), jnp.exp2(s[:, LANES:] - m)], axis=1)
                pv = _dot(p.astype(BF16), v_ext)
                l = pv[:, LANES:]
                o_half.append(pv[:, :LANES] / l)
                lse_half.append(m + jnp.log2(l))
            o_ref[rows, pair] = jnp.where(low_o, o_half[0], o_half[1]).astype(o_ref.dtype)
            lse_ref[rows, pair] = jnp.where(low_o, lse_half[0], lse_half[1])


def _dilated_attn(q, k, v, b, s, dil, qrows):
    w = A_GROUP_WIDTH
    l = s // dil
    qrows = min(qrows, l)
    nsub = qrows // BLOCK
    view = lambda t: t.reshape(b, l, dil * w)
    cur = lambda bi, r, n: (bi, n, r)
    prev = lambda bi, r, n: (bi, jnp.maximum(n * nsub - 1, 0), r)
    o, lse = pl.pallas_call(
        functools.partial(_dilated_kernel, nsub=nsub),
        grid=(b, dil, l // qrows),
        in_specs=[
            pl.BlockSpec((None, qrows, w), cur),
            pl.BlockSpec((None, BLOCK, w), prev),
            pl.BlockSpec((None, qrows, w), cur),
            pl.BlockSpec((None, BLOCK, w), prev),
            pl.BlockSpec((None, qrows, w), cur),
        ],
        out_specs=[pl.BlockSpec((None, qrows, w), cur), pl.BlockSpec((None, qrows, w), cur)],
        out_shape=[jax.ShapeDtypeStruct((b, l, dil * w), BF16),
                   jax.ShapeDtypeStruct((b, l, dil * w), F32)],
        compiler_params=_params(("arbitrary", "arbitrary", "arbitrary")),
        name=f"dilated_attn_d{dil}",
    )(view(q), view(k), view(k), view(v), view(v))
    return o.reshape(b * l, dil * w), lse.reshape(b * l, dil * w)


def _diff_kernel(lam_ref, g_ref, q_ref, k_ref, v_ref, o_ref, acc_ref, m_ref, sa_ref, sb_ref,
                 *, tq, rows, lambda_init):
    i = pl.program_id(2)
    tk = tq // 2
    q = q_ref[...]
    acc_ref[...] = jnp.zeros_like(acc_ref)
    m_ref[...] = jnp.full_like(m_ref, NEG)
    streams = [(mp, rb) for rb in range(tq // rows) for mp in range(2)]

    def visible(rb, diag):
        if diag is None:
            return tk
        return max(0, min(tk, (rb + 1) * rows - diag * tk))

    def scores(jc, dst_ref, diag=None):
        k = k_ref[pl.ds(pl.multiple_of(jc * tk, tk), tk), :]
        for mp, rb in streams:
            if visible(rb, diag) == 0:
                continue
            cols = slice(mp * HEAD_DIM, (mp + 1) * HEAD_DIM)
            rs = slice(rb * rows, (rb + 1) * rows)
            dst_ref[mp, rs, :] = _dot_nt(q[rs, cols], k[:, cols])

    def consume(jc, src_ref, diag=None):
        start = pl.multiple_of(jc * tk, tk)
        for mp, rb in streams:
            vis = visible(rb, diag)
            if vis == 0:
                continue
            rs = slice(rb * rows, (rb + 1) * rows)
            s = src_ref[mp, rs, :vis]
            if diag is not None and diag * tk + vis > rb * rows + 1:
                r = lax.broadcasted_iota(jnp.int32, (rows, vis), 0) + rb * rows
                c = lax.broadcasted_iota(jnp.int32, (rows, vis), 1) + diag * tk
                s = jnp.where(c <= r, s, NEG)
            slabs = [s[:, n * LANES:(n + 1) * LANES] for n in range(vis // LANES)]
            smax = slabs[0]
            for sl in slabs[1:]:
                smax = jnp.maximum(smax, sl)
            m_old = m_ref[mp, rs]
            m_new = jnp.maximum(m_old, jnp.max(smax, axis=-1, keepdims=True))
            alpha = jnp.exp2(m_old - m_new)
            p = jnp.concatenate([jnp.exp2(sl - m_new) for sl in slabs], axis=1).astype(BF16)
            pv = _dot(p, v_ref[pl.ds(start, vis), :])
            acc = acc_ref[mp, rs]
            acc_ref[mp, rs] = jnp.concatenate(
                [acc[:, n * LANES:(n + 1) * LANES] * alpha for n in range(B_VEXT // LANES)], axis=1) + pv
            m_ref[mp, rs] = m_new

    scores(0, sa_ref)

    def pair(t, carry):
        jc = 2 * t
        scores(jc + 1, sb_ref)
        consume(jc, sa_ref)
        scores(jc + 2, sa_ref)
        consume(jc + 1, sb_ref)
        return carry

    lax.fori_loop(0, i, pair, 0)
    scores(2 * i + 1, sb_ref, diag=1)
    consume(2 * i, sa_ref, diag=0)
    consume(2 * i + 1, sb_ref, diag=1)

    lp = lam_ref[...]
    lam = (jnp.exp(jnp.sum(lp[0:1] * lp[1:2], axis=-1, keepdims=True))
           - jnp.exp(jnp.sum(lp[2:3] * lp[3:4], axis=-1, keepdims=True)) + lambda_init)
    acc1 = acc_ref[0]
    acc2 = acc_ref[1]
    o1 = acc1[:, :B_HEAD_WIDTH] / acc1[:, B_HEAD_WIDTH:B_HEAD_WIDTH + 1]
    o2 = acc2[:, :B_HEAD_WIDTH] / acc2[:, B_HEAD_WIDTH:B_HEAD_WIDTH + 1]
    o = o1 - lam * o2
    o_ref[...] = (_rms(o, g_ref[...]) * (1.0 - lambda_init)).astype(o_ref.dtype)


def _diff_attn(qb, kb, vb, lam_p, subln_g, b, s, tq, lambda_init):
    hw = B_HEAD_WIDTH
    q3 = qb.reshape(b, s, B_WIDTH)
    k3 = kb.reshape(b, s, B_WIDTH)
    v3 = vb.reshape(b, s, B_HEADS * B_VEXT)
    out = pl.pallas_call(
        functools.partial(_diff_kernel, tq=tq, rows=min(256, tq // 2), lambda_init=lambda_init),
        grid=(b, B_HEADS, s // tq),
        in_specs=[
            pl.BlockSpec(lam_p.shape, lambda bi, h, i: (0, 0)),
            pl.BlockSpec((1, hw), lambda bi, h, i: (0, 0)),
            pl.BlockSpec((None, tq, hw), lambda bi, h, i: (bi, i, h)),
            pl.BlockSpec((None, s, hw), lambda bi, h, i: (bi, 0, h)),
            pl.BlockSpec((None, s, B_VEXT), lambda bi, h, i: (bi, 0, h)),
        ],
        out_specs=pl.BlockSpec((None, tq, hw), lambda bi, h, i: (bi, i, h)),
        out_shape=jax.ShapeDtypeStruct((b, s, B_WIDTH), BF16),
        scratch_shapes=[pltpu.VMEM((2, tq, B_VEXT), F32), pltpu.VMEM((2, tq, LANES), F32),
                        pltpu.VMEM((2, tq, tq // 2), F32), pltpu.VMEM((2, tq, tq // 2), F32)],
        compiler_params=_params(("arbitrary", "arbitrary", "arbitrary")),
        name="diff_attn",
    )(lam_p, subln_g, q3, k3, v3)
    return out.reshape(b * s, B_WIDTH)


def _merge_kernel(*refs, tq, rows, tiles_per_seq, route):
    (x_ref, g1_ref, wc_ref, wg_ref, bg_ref, oa0, oa1, oa2, ls0, ls1, ls2, ob_ref,
     pw_ref, ps_ref, wpa_ref, wpb_ref, wpc_ref, wo_ref, g2_ref) = refs[:19]
    if route:
        wrh_ref, wrl_ref = refs[19:21]
        xo_ref, h2_ref, ri_ref, rp_ref, cnt_ref, zc_ref, tm_ref, carry_ref = refs[21:]
    else:
        xo_ref, h2_ref, zc_ref, tm_ref = refs[19:]
    step = pl.program_id(0)
    seq_tile = step % tiles_per_seq
    d = x_ref.shape[-1]
    dils = [dil for (_, dil) in DILATED_PATTERNS]
    slabs = A_GROUP_WIDTH // LANES

    @pl.when(seq_tile == 0)
    def _():
        zc_ref[0:POOL_HALO, :] = jnp.zeros((POOL_HALO, C_WIDTH), F32)

    if route:
        @pl.when(step == 0)
        def _():
            carry_ref[...] = jnp.zeros_like(carry_ref)

    for part in range(tq // rows):
        r0 = part * rows
        rs = slice(r0, r0 + rows)
        x = x_ref[rs, :]
        hb = _rms(x, g1_ref[...]).astype(BF16)

        zc_ref[POOL_HALO + r0:POOL_HALO + r0 + rows, :] = _dot(hb, wc_ref[...])
        pos = seq_tile * tq + r0 + lax.broadcasted_iota(jnp.int32, (rows, C_GROUP_DIM), 0)
        pooled = []
        for gi, win in enumerate(POOL_WINDOWS):
            cols = slice(gi * C_GROUP_DIM, (gi + 1) * C_GROUP_DIM)
            tok = zc_ref[POOL_HALO + r0:POOL_HALO + r0 + rows, cols]
            tot = tok
            for back in range(1, win):
                tot = tot + zc_ref[POOL_HALO + r0 - back:POOL_HALO + r0 - back + rows, cols]
            cnt = jnp.minimum(pos + 1, win).astype(F32)
            dmean = tot / cnt - tok
            pooled.append(_dot(dmean.astype(BF16), pw_ref[gi]))
        out_c = (jnp.concatenate(pooled, axis=-1) * ps_ref[...]).astype(BF16)

        def token_major(src_ref, slot, dil):
            if dil == 1:
                return src_ref[rs, :].astype(F32)
            src_rows = slice(r0 // dil, (r0 + rows) // dil)
            for r in range(dil):
                for sl in range(slabs):
                    col = r * A_GROUP_WIDTH + sl * LANES
                    tm_ref[slot * slabs + sl, pl.ds(r0 + r, rows // dil, stride=dil), :] = (
                        src_ref[src_rows, col:col + LANES].astype(F32))
            return jnp.concatenate([tm_ref[slot * slabs + sl, rs, :] for sl in range(slabs)], axis=1)

        l0, l1, l2 = (token_major(ref, gi, dil) for gi, (ref, dil) in enumerate(zip((ls0, ls1, ls2), dils)))
        o0, o1, o2 = (token_major(ref, A_GROUPS + gi, dil)
                      for gi, (ref, dil) in enumerate(zip((oa0, oa1, oa2), dils)))
        lm = jnp.maximum(jnp.maximum(l0, l1), l2)
        e0, e1, e2 = jnp.exp2(l0 - lm), jnp.exp2(l1 - lm), jnp.exp2(l2 - lm)
        out_a = ((e0 * o0 + e1 * o1 + e2 * o2) / (e0 + e1 + e2)).astype(BF16)

        branches = (_dot(out_a, wpa_ref[...]), _dot(ob_ref[rs, :], wpb_ref[...]), _dot(out_c, wpc_ref[...]))
        mixed = None
        for bi, proj in enumerate(branches):
            zg = _dot(hb, wg_ref[:, bi * d:(bi + 1) * d]) + bg_ref[:, bi * d:(bi + 1) * d]
            term = jax.nn.sigmoid(zg) * proj
            mixed = term if mixed is None else mixed + term
        xn = x + _dot(mixed.astype(BF16), wo_ref[...])
        xo_ref[rs, :] = xn
        h2 = _rms(xn, g2_ref[...])
        if route:
            for cb in range(d // LANES):
                h2_ref[pl.ds(r0 * ROW_TILE + cb, rows, stride=ROW_TILE), :] = h2[:, cb * LANES:(cb + 1) * LANES]
        else:
            h2_ref[rs, :] = h2.astype(h2_ref.dtype)

        if route:
            h_hi = h2.astype(BF16)
            h_lo = (h2 - h_hi.astype(F32)).astype(BF16)
            logits = _dot(h_hi, wrh_ref[...]) + (_dot(h_lo, wrh_ref[...]) + _dot(h_hi, wrl_ref[...]))
            lane = lax.broadcasted_iota(jnp.int32, (rows, LANES), 1)
            lane_f = lane.astype(F32)
            logits = jnp.where(lane < N_EXPERTS, logits, -jnp.inf)
            m1 = jnp.max(logits, axis=-1, keepdims=True)
            i1 = jnp.min(jnp.where(logits == m1, lane_f, float(LANES)), axis=-1, keepdims=True)
            rest = jnp.where(lane_f == i1, -jnp.inf, logits)
            m2 = jnp.max(rest, axis=-1, keepdims=True)
            i2 = jnp.min(jnp.where(rest == m2, lane_f, float(LANES)), axis=-1, keepdims=True)
            e21 = jnp.exp(m2 - m1)
            p1 = 1.0 / (1.0 + e21)
            p2 = e21 * p1
            sel1 = lane_f == i1
            sel2 = lane_f == i2
            chosen = jnp.where(sel1 | sel2, 1.0, 0.0)

            r = lax.broadcasted_iota(jnp.int32, (rows, rows), 0)
            c = lax.broadcasted_iota(jnp.int32, (rows, rows), 1)
            before = _dot(jnp.where(c < r, 1.0, 0.0).astype(BF16), chosen.astype(BF16)) + carry_ref[...]
            rank1 = jnp.sum(jnp.where(sel1, before, 0.0), axis=-1, keepdims=True)
            rank2 = jnp.sum(jnp.where(sel2, before, 0.0), axis=-1, keepdims=True)
            carry_ref[...] = carry_ref[...] + jnp.sum(chosen, axis=0, keepdims=True)
            packed = jnp.where(lane == 0, i1, jnp.where(lane == 1, i2,
                               jnp.where(lane == 2, rank1, jnp.where(lane == 3, rank2, 0.0))))
            ri_ref[rs, :] = packed.astype(jnp.int32)
            rp_ref[rs, :] = jnp.where(lane == 0, p1, jnp.where(lane == 1, p2, 0.0))

    zc_ref[0:POOL_HALO, :] = zc_ref[tq:tq + POOL_HALO, :]
    if route:
        cnt_ref[...] = jnp.broadcast_to(carry_ref[...], cnt_ref.shape)


def _merge(x2, g1, w_c, w_g, b_g, oa, lse, ob, pool_w, pool_scale, wpa, wpb, wpc, wo, g2,
           router, s, tq, h2_dtype):
    t, d = x2.shape
    route = router is not None
    row = lambda i: (i, 0)
    const = lambda i: (0, 0)
    resident = lambda a: pl.BlockSpec(a.shape, lambda i: (0,) * a.ndim, pipeline_mode=pl.Buffered(1))
    args = [x2, g1, w_c, w_g, b_g, *oa, *lse, ob, pool_w, pool_scale, wpa, wpb, wpc, wo, g2]
    in_specs = [pl.BlockSpec((tq, d), row), pl.BlockSpec((1, d), const), resident(w_c), resident(w_g),
                pl.BlockSpec(b_g.shape, const)]
    in_specs += [pl.BlockSpec((tq // dil, dil * A_GROUP_WIDTH), row) for (_, dil) in DILATED_PATTERNS] * 2
    in_specs += [pl.BlockSpec((tq, B_WIDTH), row), resident(pool_w), pl.BlockSpec(pool_scale.shape, const),
                 resident(wpa), resident(wpb), resident(wpc), resident(wo), pl.BlockSpec((1, d), const)]
    if route:
        assert d == ROW_TILE * LANES, "a token row must fill exactly one (8,128) f32 tile"
        h2_spec = pl.BlockSpec((tq * ROW_TILE, LANES), row)
        h2_shape = jax.ShapeDtypeStruct((t * ROW_TILE, LANES), F32)
    else:
        h2_spec = pl.BlockSpec((tq, d), row)
        h2_shape = jax.ShapeDtypeStruct((t, d), h2_dtype)
    out_specs = [pl.BlockSpec((tq, d), row), h2_spec]
    out_shape = [jax.ShapeDtypeStruct((t, d), F32), h2_shape]
    scratch = [pltpu.VMEM((POOL_HALO + tq, C_WIDTH), F32), pltpu.VMEM((2 * A_GROUPS * (A_GROUP_WIDTH // LANES), tq, LANES), F32)]
    if route:
        args += list(router)
        in_specs += [resident(router[0]), resident(router[1])]
        out_specs += [pl.BlockSpec((tq, LANES), row), pl.BlockSpec((tq, LANES), row),
                      pl.BlockSpec((8, LANES), const)]
        out_shape += [jax.ShapeDtypeStruct((t, LANES), jnp.int32), jax.ShapeDtypeStruct((t, LANES), F32),
                      jax.ShapeDtypeStruct((8, LANES), F32)]
        scratch += [pltpu.VMEM((1, LANES), F32)]
    return pl.pallas_call(
        functools.partial(_merge_kernel, tq=tq, rows=min(256, tq), tiles_per_seq=s // tq, route=route),
        grid=(t // tq,),
        in_specs=in_specs,
        out_specs=out_specs,
        out_shape=out_shape,
        scratch_shapes=scratch,
        compiler_params=_params(("arbitrary",)),
        name="merge_route" if route else "merge",
    )(*args)


def _swiglu_kernel(te_ref, tv_ref, ts_ref, *refs, residual):
    if residual:
        x_ref, res_ref, wg_ref, wu_ref, wd_ref, o_ref, xb_ref, acc_ref = refs
    else:
        x_ref, wg_ref, wu_ref, wd_ref, o_ref, xb_ref, acc_ref = refs
    j = pl.program_id(0)
    c = pl.program_id(1)
    tm, d = xb_ref.shape
    nchunk = d // LANES

    @pl.when((tv_ref[j] == 0) & (c == 0))
    def _():
        o_ref[...] = jnp.zeros_like(o_ref)

    @pl.when(tv_ref[j] > 0)
    def _():
        @pl.when(c == 0)
        def _():
            if residual:
                xb_ref[...] = x_ref[...].astype(BF16)
            else:
                for cb in range(nchunk):
                    xb_ref[:, cb * LANES:(cb + 1) * LANES] = (
                        x_ref[pl.ds(cb, tm, stride=ROW_TILE), :].astype(BF16))
            acc_ref[...] = jnp.zeros_like(acc_ref)

        xb = xb_ref[...]
        gate = _dot(xb, wg_ref[...])
        up = _dot(xb, wu_ref[...])
        mid = (gate * jax.nn.sigmoid(gate) * up).astype(BF16)
        acc_ref[...] += _dot(mid, wd_ref[...])

        @pl.when(c == pl.num_programs(1) - 1)
        def _():
            if residual:
                o_ref[...] = res_ref[...] + acc_ref[...]
            else:
                for cb in range(nchunk):
                    o_ref[pl.ds(cb, tm, stride=ROW_TILE), :] = acc_ref[:, cb * LANES:(cb + 1) * LANES]


def _grouped_swiglu(xs, res, wg, wu, wd, tile_expert, tile_valid, tile_src, tm, cf):
    d = wg.shape[1]
    ff = wg.shape[-1]
    residual = res is not None
    xmap = lambda j, c, te, tv, ts: (ts[j], 0)
    if residual:
        n = xs.shape[0]
        row_block = (tm, d)
        in_specs = [pl.BlockSpec(row_block, xmap), pl.BlockSpec(row_block, xmap)]
        args = [xs, res]
    else:
        n = xs.shape[0] // ROW_TILE
        row_block = (tm * ROW_TILE, LANES)
        in_specs = [pl.BlockSpec(row_block, xmap)]
        args = [xs]
    in_specs += [
        pl.BlockSpec((None, d, cf), lambda j, c, te, tv, ts: (te[j], 0, c * tv[j])),
        pl.BlockSpec((None, d, cf), lambda j, c, te, tv, ts: (te[j], 0, c * tv[j])),
        pl.BlockSpec((None, cf, d), lambda j, c, te, tv, ts: (te[j], c * tv[j], 0)),
    ]
    args += [wg, wu, wd]
    return pl.pallas_call(
        functools.partial(_swiglu_kernel, residual=residual),
        grid_spec=pltpu.PrefetchScalarGridSpec(
            num_scalar_prefetch=3,
            grid=(n // tm, ff // cf),
            in_specs=in_specs,
            out_specs=pl.BlockSpec(row_block, lambda j, c, te, tv, ts: (j, 0)),
            scratch_shapes=[pltpu.VMEM((tm, d), BF16), pltpu.VMEM((tm, d), F32)],
        ),
        out_shape=jax.ShapeDtypeStruct(xs.shape, F32),
        compiler_params=_params(("arbitrary", "arbitrary")),
        name="grouped_swiglu_res" if residual else "grouped_swiglu",
    )(tile_expert, tile_valid, tile_src, *args)


def _dispatch_kernel(zt_ref, slot_ref, h_ref, xs_ref, zero_ref, sem, zsem, *, tq, tm):
    @pl.when(pl.program_id(0) == 0)
    def _():
        zero_ref[...] = jnp.zeros_like(zero_ref)
        for z in range(zt_ref.shape[0]):
            start = pl.multiple_of(zt_ref[z] * (tm * ROW_TILE), tm * ROW_TILE)
            fill = pltpu.make_async_copy(zero_ref, xs_ref.at[pl.ds(start, tm * ROW_TILE), :], zsem.at[0])
            fill.start()
            fill.wait()

    def row_copy(r, k):
        src = pl.multiple_of(r * ROW_TILE, ROW_TILE)
        dst = pl.multiple_of(slot_ref[0, k, r] * ROW_TILE, ROW_TILE)
        return pltpu.make_async_copy(h_ref.at[pl.ds(src, ROW_TILE), :],
                                     xs_ref.at[pl.ds(dst, ROW_TILE), :], sem.at[k])

    def issue(r, carry):
        row_copy(r, 0).start()
        row_copy(r, 1).start()
        return carry

    lax.fori_loop(0, tq, issue, 0, unroll=ISSUE_UNROLL)
    for k in range(TOP_K):
        pltpu.make_async_copy(h_ref, xs_ref.at[pl.ds(0, tq * ROW_TILE), :], sem.at[k]).wait()


def _moe_dispatch(h2, slots, zero_tiles, nslot, tq, tm):
    d = LANES
    return pl.pallas_call(
        functools.partial(_dispatch_kernel, tq=tq, tm=tm),
        grid_spec=pltpu.PrefetchScalarGridSpec(
            num_scalar_prefetch=1,
            grid=(h2.shape[0] // (tq * ROW_TILE),),
            in_specs=[
                pl.BlockSpec((1, TOP_K, tq), lambda i, zt: (i, 0, 0), memory_space=pltpu.SMEM),
                pl.BlockSpec((tq * ROW_TILE, d), lambda i, zt: (i, 0)),
            ],
            out_specs=pl.BlockSpec(memory_space=pl.ANY),
            scratch_shapes=[pltpu.VMEM((tm * ROW_TILE, d), h2.dtype), pltpu.SemaphoreType.DMA((TOP_K,)),
                            pltpu.SemaphoreType.DMA((1,))],
        ),
        out_shape=jax.ShapeDtypeStruct((nslot * ROW_TILE, d), h2.dtype),
        compiler_params=_params(("arbitrary",)),
        name="moe_dispatch",
    )(zero_tiles, slots, h2)


def _combine_kernel(slot_ref, x_ref, rp_ref, g_ref, ys_ref, o_ref, y_ref, sem, *, tq, normalize):
    def row_copy(r, k):
        src = pl.multiple_of(slot_ref[0, k, r] * ROW_TILE, ROW_TILE)
        dst = pl.multiple_of(r * ROW_TILE, ROW_TILE)
        return pltpu.make_async_copy(ys_ref.at[pl.ds(src, ROW_TILE), :],
                                     y_ref.at[k, pl.ds(dst, ROW_TILE), :], sem.at[k])

    def issue(r, carry):
        row_copy(r, 0).start()
        row_copy(r, 1).start()
        return carry

    lax.fori_loop(0, tq, issue, 0, unroll=ISSUE_UNROLL)
    for k in range(TOP_K):
        pltpu.make_async_copy(ys_ref.at[pl.ds(0, tq * ROW_TILE), :], y_ref.at[k], sem.at[k]).wait()
    rp = rp_ref[...]
    y = [jnp.concatenate([y_ref[k, pl.ds(cb, tq, stride=ROW_TILE), :] for cb in range(x_ref.shape[1] // LANES)],
                         axis=1) for k in range(TOP_K)]
    xn = x_ref[...] + rp[:, 0:1] * y[0] + rp[:, 1:2] * y[1]
    o_ref[...] = _rms(xn, g_ref[...]) if normalize else xn


def _moe_combine(x2, route_p, slots, ys, g, tq, normalize):
    t, d = x2.shape
    return pl.pallas_call(
        functools.partial(_combine_kernel, tq=tq, normalize=normalize),
        grid=(t // tq,),
        in_specs=[
            pl.BlockSpec((1, TOP_K, tq), lambda i: (i, 0, 0), memory_space=pltpu.SMEM),
            pl.BlockSpec((tq, d), lambda i: (i, 0)),
            pl.BlockSpec((tq, LANES), lambda i: (i, 0)),
            pl.BlockSpec((1, d), lambda i: (0, 0)),
            pl.BlockSpec(memory_space=pl.ANY),
        ],
        out_specs=pl.BlockSpec((tq, d), lambda i: (i, 0)),
        out_shape=jax.ShapeDtypeStruct((t, d), F32),
        scratch_shapes=[pltpu.VMEM((TOP_K, tq * ROW_TILE, LANES), F32), pltpu.SemaphoreType.DMA((TOP_K,))],
        compiler_params=_params(("arbitrary",)),
        name="moe_combine",
    )(slots, x2, route_p, g, ys)


def _final_norm_kernel(x_ref, g_ref, o_ref):
    o_ref[...] = _rms(x_ref[...], g_ref[...])


def _final_norm(x2, g, tq):
    t, d = x2.shape
    return pl.pallas_call(
        _final_norm_kernel,
        grid=(t // tq,),
        in_specs=[pl.BlockSpec((tq, d), lambda i: (i, 0)), pl.BlockSpec((1, d), lambda i: (0, 0))],
        out_specs=pl.BlockSpec((tq, d), lambda i: (i, 0)),
        out_shape=jax.ShapeDtypeStruct((t, d), F32),
        compiler_params=_params(("arbitrary",)),
        name="final_norm",
    )(x2, g)


def _rope_lane_tables(positions):
    inv_freq = ROPE_THETA ** (-jnp.arange(0, ROT_DIM, 2, dtype=F32) / ROT_DIM)
    ang = positions.astype(F32).reshape(-1, 1) * inv_freq
    cos, sin = jnp.cos(ang), jnp.sin(ang)
    t = ang.shape[0]
    rest = HEAD_DIM - ROT_DIM
    z8 = jnp.zeros((t, ROT_HALF), F32)
    cos_h = jnp.concatenate([cos, cos, jnp.ones((t, rest), F32)], axis=-1)
    lo_h = jnp.concatenate([-sin, z8, jnp.zeros((t, rest), F32)], axis=-1)
    hi_h = jnp.concatenate([z8, sin, jnp.zeros((t, rest), F32)], axis=-1)
    rep = LANES // HEAD_DIM
    return jnp.tile(cos_h, (1, rep)), jnp.tile(lo_h, (1, rep)), jnp.tile(hi_h, (1, rep))


def _moe_plan(route_i, counts, t, tm):
    cnt = counts[0, :N_EXPERTS].astype(jnp.int32)
    padded = ((cnt + tm - 1) // tm) * tm
    ends = jnp.cumsum(padded)
    offs = ends - padded
    slot = offs[route_i[:, 0:TOP_K]] + route_i[:, TOP_K:2 * TOP_K]
    ntile = (TOP_K * t) // tm + N_EXPERTS
    starts = jnp.arange(ntile, dtype=jnp.int32) * tm
    valid = (starts < ends[-1]).astype(jnp.int32)
    last = jnp.maximum(ends[-1] // tm - 1, 0)
    src = jnp.minimum(jnp.arange(ntile, dtype=jnp.int32), last)
    expert = jnp.sum(((src * tm)[:, None] >= ends[None, :]).astype(jnp.int32), axis=1)
    expert = jnp.minimum(expert, N_EXPERTS - 1)
    last_tile = jnp.where(padded > 0, ends // tm - 1, ntile - 1).astype(jnp.int32)
    tail = jnp.arange((TOP_K * t) // tm, ntile, dtype=jnp.int32)
    return slot, expert, valid, src, jnp.concatenate([last_tile, tail]), ntile * tm


def kernel(x, positions, norm1_g, w_in, b_gate, diff_lambda, diff_subln_g, pool_w, pool_scale,
           w_proj_a, w_proj_b, w_proj_c, w_out, norm2_g, ffn_w_gate, ffn_w_up, ffn_w_down,
           moe_router, moe_w_gate, moe_w_up, moe_w_down, final_norm_g):
    b, s, d = x.shape
    t = b * s
    depth = w_in.shape[0]
    tq = min(512, s)
    tm = min(512, s)
    ff = ffn_w_gate.shape[-1]
    cf = ff // 2 if ff % 512 == 0 and ff >= 1024 else ff
    x2 = x.reshape(t, d)
    cos_t, sinlo_t, sinhi_t = _rope_lane_tables(positions)
    out = None
    for l in range(depth):
        w_l = w_in[l]
        qkv = _qkv_proj(x2, norm1_g[l].reshape(1, d), w_l[:, :OFF_C].astype(BF16),
                        cos_t, sinlo_t, sinhi_t, tq)
        qa, ka, va = qkv[0:3], qkv[3:6], qkv[6:9]
        qb, kb, vb = qkv[9:12]
        oa, lse = [], []
        for g, (window, dil) in enumerate(DILATED_PATTERNS):
            assert window // dil == BLOCK
            o_g, lse_g = _dilated_attn(qa[g], ka[g], va[g], b, s, dil, 512)
            oa.append(o_g)
            lse.append(lse_g)
        lambda_init = 0.8 - 0.6 * math.exp(-0.3 * l)
        ob = _diff_attn(qb, kb, vb, diff_lambda[l], diff_subln_g[l].reshape(1, B_HEAD_WIDTH),
                        b, s, min(1024, s), lambda_init)

        dense = l % 2 == 0
        router = None
        if not dense:
            wr = jnp.zeros((d, LANES), F32).at[:, :N_EXPERTS].set(moe_router[l // 2])
            wr_hi = wr.astype(BF16)
            router = (wr_hi, (wr - wr_hi.astype(F32)).astype(BF16))
        merged = _merge(
            x2, norm1_g[l].reshape(1, d), w_l[:, OFF_C:OFF_G].astype(BF16), w_l[:, OFF_G:].astype(BF16),
            b_gate[l].reshape(1, N_BRANCH * d), oa, lse, ob, pool_w[l].astype(BF16),
            pool_scale[l].reshape(1, C_WIDTH), w_proj_a[l].astype(BF16), w_proj_b[l].astype(BF16),
            w_proj_c[l].astype(BF16), w_out[l].astype(BF16), norm2_g[l].reshape(1, d),
            router, s, tq, BF16 if dense else F32)
        if dense:
            xn, h2 = merged
            i = l // 2
            ntile = t // tm
            ident = jnp.arange(ntile, dtype=jnp.int32)
            x2 = _grouped_swiglu(h2, xn, ffn_w_gate[i:i + 1].astype(BF16), ffn_w_up[i:i + 1].astype(BF16),
                                 ffn_w_down[i:i + 1].astype(BF16), jnp.zeros((ntile,), jnp.int32),
                                 jnp.ones((ntile,), jnp.int32), ident, tm, cf)
            out = None
        else:
            xn, h2, route_i, route_p, counts = merged
            i = l // 2
            slot, expert, valid, src, zero_tiles, nslot = _moe_plan(route_i, counts, t, tm)
            slots = slot.reshape(t // tq, tq, TOP_K).transpose(0, 2, 1)
            xs = _moe_dispatch(h2, slots, zero_tiles, nslot, tq, tm)
            ys = _grouped_swiglu(xs, None, moe_w_gate[i].astype(BF16), moe_w_up[i].astype(BF16),
                                 moe_w_down[i].astype(BF16), expert, valid, src, tm, cf)
            last = l == depth - 1
            res = _moe_combine(xn, route_p, slots, ys, final_norm_g.reshape(1, d), tq, last)
            if last:
                out = res
            else:
                x2 = res
    if out is None:
        out = _final_norm(x2, final_norm_g.reshape(1, d), tq)
    return out.reshape(b, s, d)
```

```python
import functools
import math

import jax
import jax.numpy as jnp
from jax import lax
from jax.experimental import pallas as pl
from jax.experimental.pallas import tpu as pltpu

F32 = jnp.float32
BF16 = jnp.bfloat16

HEAD_DIM = 64
ROPE_THETA = 500000.0
ROT_DIM = HEAD_DIM // 4
ROT_HALF = ROT_DIM // 2
BLOCK = 128
EPS = 1e-6
NEG = -1e30

DILATED_PATTERNS = ((128, 1), (512, 4), (2048, 16))
A_GROUPS = len(DILATED_PATTERNS)
A_HEADS_PER_GROUP = 4
A_GROUP_WIDTH = A_HEADS_PER_GROUP * HEAD_DIM
A_WIDTH = A_GROUPS * A_GROUP_WIDTH

B_HEADS = 4
B_HEAD_WIDTH = 2 * HEAD_DIM
B_WIDTH = B_HEADS * B_HEAD_WIDTH
B_VEXT = 2 * B_HEAD_WIDTH

POOL_WINDOWS = (2, 4, 8, 16)
C_GROUP_DIM = 128
C_WIDTH = len(POOL_WINDOWS) * C_GROUP_DIM
POOL_HALO = 16

N_BRANCH = 3
N_EXPERTS = 8
TOP_K = 2

OFF_QA = 0
OFF_KA = OFF_QA + A_WIDTH
OFF_VA = OFF_KA + A_WIDTH
OFF_QB = OFF_VA + A_WIDTH
OFF_KB = OFF_QB + B_WIDTH
OFF_VB = OFF_KB + B_WIDTH
OFF_C = OFF_VB + B_WIDTH
OFF_G = OFF_C + C_WIDTH

LANES = 128
MXU_WIDTH = 256
ROW_TILE = 8
ISSUE_UNROLL = 16
V7X_VMEM_BYTES = 64 * 1024 * 1024
VMEM_LIMIT = V7X_VMEM_BYTES - 8 * 1024 * 1024

QK_SCALE = HEAD_DIM ** -0.5
LOG2_E = math.log2(math.e)


def _params(semantics):
    return pltpu.CompilerParams(dimension_semantics=semantics, vmem_limit_bytes=VMEM_LIMIT)


def _rms(x, g):
    return x * lax.rsqrt(jnp.mean(x * x, axis=-1, keepdims=True) + EPS) * g


def _dot(a, b):
    return jnp.dot(a, b, preferred_element_type=F32)


def _dot_nt(a, b):
    return lax.dot_general(a, b, (((1,), (1,)), ((), ())), preferred_element_type=F32)


def _qkv_kernel(x_ref, g_ref, w_ref, cos_ref, sinlo_ref, sinhi_ref,
                qa0, qa1, qa2, ka0, ka1, ka2, va0, va1, va2, qb, kb, vb, zs_ref):
    tq = x_ref.shape[0]
    hb = _rms(x_ref[...], g_ref[...]).astype(BF16)
    cos = cos_ref[...]
    sinlo = sinlo_ref[...]
    sinhi = sinhi_ref[...]

    def rope(z):
        return (z * cos + pltpu.roll(z, LANES - ROT_HALF, 1) * sinlo
                + pltpu.roll(z, ROT_HALF, 1) * sinhi)

    def project(off, width, out_ref, rotary, scale, dil=1):
        for c in range(0, width, MXU_WIDTH):
            zz = _dot(hb, w_ref[:, off + c:off + c + MXU_WIDTH])
            for p in range(0, MXU_WIDTH, LANES):
                z = zz[:, p:p + LANES]
                if rotary:
                    z = rope(z)
                if scale != 1.0:
                    z = z * scale
                if dil == 1:
                    out_ref[:, c + p:c + p + LANES] = z.astype(out_ref.dtype)
                else:
                    zs_ref[(c + p) // LANES] = z
        if dil > 1:
            for r in range(dil):
                for sl in range(width // LANES):
                    out_ref[:, r * width + sl * LANES:r * width + (sl + 1) * LANES] = (
                        zs_ref[sl, pl.ds(r, tq // dil, stride=dil), :].astype(out_ref.dtype))

    for g, (q_ref, k_ref, v_ref) in enumerate(((qa0, ka0, va0), (qa1, ka1, va1), (qa2, ka2, va2))):
        dil = DILATED_PATTERNS[g][1]
        project(OFF_QA + g * A_GROUP_WIDTH, A_GROUP_WIDTH, q_ref, True, QK_SCALE * LOG2_E, dil)
        project(OFF_KA + g * A_GROUP_WIDTH, A_GROUP_WIDTH, k_ref, True, 1.0, dil)
        project(OFF_VA + g * A_GROUP_WIDTH, A_GROUP_WIDTH, v_ref, False, 1.0, dil)
    project(OFF_QB, B_WIDTH, qb, True, QK_SCALE * LOG2_E)
    project(OFF_KB, B_WIDTH, kb, True, 1.0)
    lane = lax.broadcasted_iota(jnp.int32, (x_ref.shape[0], B_HEAD_WIDTH), 1)
    ones_col = jnp.where(lane == 0, 1.0, 0.0).astype(vb.dtype)
    heads_per_dot = MXU_WIDTH // B_HEAD_WIDTH
    for h0 in range(0, B_HEADS, heads_per_dot):
        zz = _dot(hb, w_ref[:, OFF_VB + h0 * B_HEAD_WIDTH:OFF_VB + (h0 + heads_per_dot) * B_HEAD_WIDTH])
        for hh in range(heads_per_dot):
            h = h0 + hh
            z = zz[:, hh * B_HEAD_WIDTH:(hh + 1) * B_HEAD_WIDTH]
            vb[:, h * B_VEXT:h * B_VEXT + B_HEAD_WIDTH] = z.astype(vb.dtype)
            vb[:, h * B_VEXT + B_HEAD_WIDTH:(h + 1) * B_VEXT] = ones_col


def _qkv_proj(x2, g, w_qkv, cos_t, sinlo_t, sinhi_t, tq):
    t, d = x2.shape
    row = lambda i: (i, 0)
    const = lambda i: (0, 0)
    dils = [dil for _ in range(3) for (_, dil) in DILATED_PATTERNS]
    shapes = [(t // dil, dil * A_GROUP_WIDTH, tq // dil) for dil in dils]
    shapes += [(t, B_WIDTH, tq), (t, B_WIDTH, tq), (t, B_HEADS * B_VEXT, tq)]
    return pl.pallas_call(
        _qkv_kernel,
        grid=(t // tq,),
        in_specs=[
            pl.BlockSpec((tq, d), row),
            pl.BlockSpec((1, d), const),
            pl.BlockSpec(w_qkv.shape, const, pipeline_mode=pl.Buffered(1)),
            pl.BlockSpec((tq, LANES), row),
            pl.BlockSpec((tq, LANES), row),
            pl.BlockSpec((tq, LANES), row),
        ],
        out_specs=[pl.BlockSpec((rows, w), row) for (_, w, rows) in shapes],
        out_shape=[jax.ShapeDtypeStruct((n, w), BF16) for (n, w, _) in shapes],
        scratch_shapes=[pltpu.VMEM((A_GROUP_WIDTH // LANES, tq, LANES), F32)],
        compiler_params=_params(("arbitrary",)),
        name="qkv_proj",
    )(x2, g, w_qkv, cos_t, sinlo_t, sinhi_t)


def _dilated_kernel(q_ref, kp_ref, kc_ref, vp_ref, vc_ref, o_ref, lse_ref, *, nsub):
    n = pl.program_id(2)
    a = lax.broadcasted_iota(jnp.int32, (BLOCK, 2 * BLOCK), 0)
    j = lax.broadcasted_iota(jnp.int32, (BLOCK, 2 * BLOCK), 1)
    band = (j >= a) & (j <= a + BLOCK)
    band_first = band & ((j >= BLOCK) | (n > 0))
    low_k = lax.broadcasted_iota(jnp.int32, (2 * BLOCK, LANES), 1) < HEAD_DIM
    low_o = lax.broadcasted_iota(jnp.int32, (BLOCK, LANES), 1) < HEAD_DIM
    ones_slab = jnp.ones((2 * BLOCK, LANES), BF16)
    for sb in range(nsub):
        rows = slice(sb * BLOCK, (sb + 1) * BLOCK)
        if sb == 0:
            kcat = jnp.concatenate([kp_ref[...], kc_ref[rows, :]], axis=0)
            vcat = jnp.concatenate([vp_ref[...], vc_ref[rows, :]], axis=0)
            mask = band_first
        else:
            kcat = kc_ref[(sb - 1) * BLOCK:(sb + 1) * BLOCK, :]
            vcat = vc_ref[(sb - 1) * BLOCK:(sb + 1) * BLOCK, :]
            mask = band
        q = q_ref[rows, :]
        for hp in range(A_HEADS_PER_GROUP // 2):
            pair = slice(hp * LANES, (hp + 1) * LANES)
            q_pair = q[:, pair]
            k_pair = kcat[:, pair]
            v_ext = jnp.concatenate([vcat[:, pair], ones_slab], axis=1)
            o_half, lse_half = [], []
            for half in range(2):
                k_h = jnp.where(low_k if half == 0 else ~low_k, k_pair, jnp.zeros_like(k_pair))
                s = jnp.where(mask, _dot_nt(q_pair, k_h), NEG)
                m = jnp.max(jnp.maximum(s[:, :LANES], s[:, LANES:]), axis=-1, keepdims=True)
                p = jnp.concatenate([jnp.exp2(s[:, :LANES] - m), jnp.exp2(s[:, LANES:] - m)], axis=1)
                pv = _dot(p.astype(BF16), v_ext)
                l = pv[:, LANES:]
                o_half.append(pv[:, :LANES] / l)
                lse_half.append(m + jnp.log2(l))
            o_ref[rows, pair] = jnp.where(low_o, o_half[0], o_half[1]).astype(o_ref.dtype)
            lse_ref[rows, pair] = jnp.where(low_o, lse_half[0], lse_half[1])


def _dilated_attn(q, k, v, b, s, dil, qrows):
    w = A_GROUP_WIDTH
    l = s // dil
    qrows = min(qrows, l)
    nsub = qrows // BLOCK
    view = lambda t: t.reshape(b, l, dil * w)
    cur = lambda bi, r, n: (bi, n, r)
    prev = lambda bi, r, n: (bi, jnp.maximum(n * nsub - 1, 0), r)
    o, lse = pl.pallas_call(
        functools.partial(_dilated_kernel, nsub=nsub),
        grid=(b, dil, l // qrows),
        in_specs=[
            pl.BlockSpec((None, qrows, w), cur),
            pl.BlockSpec((None, BLOCK, w), prev),
            pl.BlockSpec((None, qrows, w), cur),
            pl.BlockSpec((None, BLOCK, w), prev),
            pl.BlockSpec((None, qrows, w), cur),
        ],
        out_specs=[pl.BlockSpec((None, qrows, w), cur), pl.BlockSpec((None, qrows, w), cur)],
        out_shape=[jax.ShapeDtypeStruct((b, l, dil * w), BF16),
                   jax.ShapeDtypeStruct((b, l, dil * w), F32)],
        compiler_params=_params(("arbitrary", "arbitrary", "arbitrary")),
        name=f"dilated_attn_d{dil}",
    )(view(q), view(k), view(k), view(v), view(v))
    return o.reshape(b * l, dil * w), lse.reshape(b * l, dil * w)


def _diff_kernel(lam_ref, g_ref, q_ref, k_ref, v_ref, o_ref, acc_ref, m_ref, sa_ref, sb_ref,
                 *, tq, rows, lambda_init):
    i = pl.program_id(2)
    tk = tq // 2
    q = q_ref[...]
    acc_ref[...] = jnp.zeros_like(acc_ref)
    m_ref[...] = jnp.full_like(m_ref, NEG)
    streams = [(mp, rb) for rb in range(tq // rows) for mp in range(2)]

    def visible(rb, diag):
        if diag is None:
            return tk
        return max(0, min(tk, (rb + 1) * rows - diag * tk))

    def scores(jc, dst_ref, diag=None):
        k = k_ref[pl.ds(pl.multiple_of(jc * tk, tk), tk), :]
        for mp, rb in streams:
            if visible(rb, diag) == 0:
                continue
            cols = slice(mp * HEAD_DIM, (mp + 1) * HEAD_DIM)
            rs = slice(rb * rows, (rb + 1) * rows)
            dst_ref[mp, rs, :] = _dot_nt(q[rs, cols], k[:, cols])

    def consume(jc, src_ref, diag=None):
        start = pl.multiple_of(jc * tk, tk)
        for mp, rb in streams:
            vis = visible(rb, diag)
            if vis == 0:
                continue
            rs = slice(rb * rows, (rb + 1) * rows)
            s = src_ref[mp, rs, :vis]
            if diag is not None and diag * tk + vis > rb * rows + 1:
                r = lax.broadcasted_iota(jnp.int32, (rows, vis), 0) + rb * rows
                c = lax.broadcasted_iota(jnp.int32, (rows, vis), 1) + diag * tk
                s = jnp.where(c <= r, s, NEG)
            slabs = [s[:, n * LANES:(n + 1) * LANES] for n in range(vis // LANES)]
            smax = slabs[0]
            for sl in slabs[1:]:
                smax = jnp.maximum(smax, sl)
            m_old = m_ref[mp, rs]
            m_new = jnp.maximum(m_old, jnp.max(smax, axis=-1, keepdims=True))
            alpha = jnp.exp2(m_old - m_new)
            p = jnp.concatenate([jnp.exp2(sl - m_new) for sl in slabs], axis=1).astype(BF16)
            pv = _dot(p, v_ref[pl.ds(start, vis), :])
            acc = acc_ref[mp, rs]
            acc_ref[mp, rs] = jnp.concatenate(
                [acc[:, n * LANES:(n + 1) * LANES] * alpha for n in range(B_VEXT // LANES)], axis=1) + pv
            m_ref[mp, rs] = m_new

    scores(0, sa_ref)

    def pair(t, carry):
        jc = 2 * t
        scores(jc + 1, sb_ref)
        consume(jc, sa_ref)
        scores(jc + 2, sa_ref)
        consume(jc + 1, sb_ref)
        return carry

    lax.fori_loop(0, i, pair, 0)
    scores(2 * i + 1, sb_ref, diag=1)
    consume(2 * i, sa_ref, diag=0)
    consume(2 * i + 1, sb_ref, diag=1)

    lp = lam_ref[...]
    lam = (jnp.exp(jnp.sum(lp[0:1] * lp[1:2], axis=-1, keepdims=True))
           - jnp.exp(jnp.sum(lp[2:3] * lp[3:4], axis=-1, keepdims=True)) + lambda_init)
    acc1 = acc_ref[0]
    acc2 = acc_ref[1]
    o1 = acc1[:, :B_HEAD_WIDTH] / acc1[:, B_HEAD_WIDTH:B_HEAD_WIDTH + 1]
    o2 = acc2[:, :B_HEAD_WIDTH] / acc2[:, B_HEAD_WIDTH:B_HEAD_WIDTH + 1]
    o = o1 - lam * o2
    o_ref[...] = (_rms(o, g_ref[...]) * (1.0 - lambda_init)).astype(o_ref.dtype)


def _diff_attn(qb, kb, vb, lam_p, subln_g, b, s, tq, lambda_init):
    hw = B_HEAD_WIDTH
    q3 = qb.reshape(b, s, B_WIDTH)
    k3 = kb.reshape(b, s, B_WIDTH)
    v3 = vb.reshape(b, s, B_HEADS * B_VEXT)
    out = pl.pallas_call(
        functools.partial(_diff_kernel, tq=tq, rows=min(256, tq // 2), lambda_init=lambda_init),
        grid=(b, B_HEADS, s // tq),
        in_specs=[
            pl.BlockSpec(lam_p.shape, lambda bi, h, i: (0, 0)),
            pl.BlockSpec((1, hw), lambda bi, h, i: (0, 0)),
            pl.BlockSpec((None, tq, hw), lambda bi, h, i: (bi, i, h)),
            pl.BlockSpec((None, s, hw), lambda bi, h, i: (bi, 0, h)),
            pl.BlockSpec((None, s, B_VEXT), lambda bi, h, i: (bi, 0, h)),
        ],
        out_specs=pl.BlockSpec((None, tq, hw), lambda bi, h, i: (bi, i, h)),
        out_shape=jax.ShapeDtypeStruct((b, s, B_WIDTH), BF16),
        scratch_shapes=[pltpu.VMEM((2, tq, B_VEXT), F32), pltpu.VMEM((2, tq, LANES), F32),
                        pltpu.VMEM((2, tq, tq // 2), F32), pltpu.VMEM((2, tq, tq // 2), F32)],
        compiler_params=_params(("arbitrary", "arbitrary", "arbitrary")),
        name="diff_attn",
    )(lam_p, subln_g, q3, k3, v3)
    return out.reshape(b * s, B_WIDTH)


def _merge_kernel(*refs, tq, rows, tiles_per_seq, route):
    (x_ref, g1_ref, wc_ref, wg_ref, bg_ref, oa0, oa1, oa2, ls0, ls1, ls2, ob_ref,
     pw_ref, ps_ref, wpa_ref, wpb_ref, wpc_ref, wo_ref, g2_ref) = refs[:19]
    if route:
        wrh_ref, wrl_ref = refs[19:21]
        xo_ref, h2_ref, ri_ref, rp_ref, cnt_ref, zc_ref, tm_ref, carry_ref = refs[21:]
    else:
        xo_ref, h2_ref, zc_ref, tm_ref = refs[19:]
    step = pl.program_id(0)
    seq_tile = step % tiles_per_seq
    d = x_ref.shape[-1]
    dils = [dil for (_, dil) in DILATED_PATTERNS]
    slabs = A_GROUP_WIDTH // LANES

    @pl.when(seq_tile == 0)
    def _():
        zc_ref[0:POOL_HALO, :] = jnp.zeros((POOL_HALO, C_WIDTH), F32)

    if route:
        @pl.when(step == 0)
        def _():
            carry_ref[...] = jnp.zeros_like(carry_ref)

    for part in range(tq // rows):
        r0 = part * rows
        rs = slice(r0, r0 + rows)
        x = x_ref[rs, :]
        hb = _rms(x, g1_ref[...]).astype(BF16)

        zc_ref[POOL_HALO + r0:POOL_HALO + r0 + rows, :] = _dot(hb, wc_ref[...])
        pos = seq_tile * tq + r0 + lax.broadcasted_iota(jnp.int32, (rows, C_GROUP_DIM), 0)
        pooled = []
        for gi, win in enumerate(POOL_WINDOWS):
            cols = slice(gi * C_GROUP_DIM, (gi + 1) * C_GROUP_DIM)
            tok = zc_ref[POOL_HALO + r0:POOL_HALO + r0 + rows, cols]
            tot = tok
            for back in range(1, win):
                tot = tot + zc_ref[POOL_HALO + r0 - back:POOL_HALO + r0 - back + rows, cols]
            cnt = jnp.minimum(pos + 1, win).astype(F32)
            dmean = tot / cnt - tok
            pooled.append(_dot(dmean.astype(BF16), pw_ref[gi]))
        out_c = (jnp.concatenate(pooled, axis=-1) * ps_ref[...]).astype(BF16)

        def token_major(src_ref, slot, dil):
            if dil == 1:
                return src_ref[rs, :].astype(F32)
            src_rows = slice(r0 // dil, (r0 + rows) // dil)
            for r in range(dil):
                for sl in range(slabs):
                    col = r * A_GROUP_WIDTH + sl * LANES
                    tm_ref[slot * slabs + sl, pl.ds(r0 + r, rows // dil, stride=dil), :] = (
                        src_ref[src_rows, col:col + LANES].astype(F32))
            return jnp.concatenate([tm_ref[slot * slabs + sl, rs, :] for sl in range(slabs)], axis=1)

        l0, l1, l2 = (token_major(ref, gi, dil) for gi, (ref, dil) in enumerate(zip((ls0, ls1, ls2), dils)))
        o0, o1, o2 = (token_major(ref, A_GROUPS + gi, dil)
                      for gi, (ref, dil) in enumerate(zip((oa0, oa1, oa2), dils)))
        lm = jnp.maximum(jnp.maximum(l0, l1), l2)
        e0, e1, e2 = jnp.exp2(l0 - lm), jnp.exp2(l1 - lm), jnp.exp2(l2 - lm)
        out_a = ((e0 * o0 + e1 * o1 + e2 * o2) / (e0 + e1 + e2)).astype(BF16)

        branches = (_dot(out_a, wpa_ref[...]), _dot(ob_ref[rs, :], wpb_ref[...]), _dot(out_c, wpc_ref[...]))
        mixed = None
        for bi, proj in enumerate(branches):
            zg = _dot(hb, wg_ref[:, bi * d:(bi + 1) * d]) + bg_ref[:, bi * d:(bi + 1) * d]
            term = jax.nn.sigmoid(zg) * proj
            mixed = term if mixed is None else mixed + term
        xn = x + _dot(mixed.astype(BF16), wo_ref[...])
        xo_ref[rs, :] = xn
        h2 = _rms(xn, g2_ref[...])
        if route:
            for cb in range(d // LANES):
                h2_ref[pl.ds(r0 * ROW_TILE + cb, rows, stride=ROW_TILE), :] = h2[:, cb * LANES:(cb + 1) * LANES]
        else:
            h2_ref[rs, :] = h2.astype(h2_ref.dtype)

        if route:
            h_hi = h2.astype(BF16)
            h_lo = (h2 - h_hi.astype(F32)).astype(BF16)
            logits = _dot(h_hi, wrh_ref[...]) + (_dot(h_lo, wrh_ref[...]) + _dot(h_hi, wrl_ref[...]))
            lane = lax.broadcasted_iota(jnp.int32, (rows, LANES), 1)
            lane_f = lane.astype(F32)
            logits = jnp.where(lane < N_EXPERTS, logits, -jnp.inf)
            m1 = jnp.max(logits, axis=-1, keepdims=True)
            i1 = jnp.min(jnp.where(logits == m1, lane_f, float(LANES)), axis=-1, keepdims=True)
            rest = jnp.where(lane_f == i1, -jnp.inf, logits)
            m2 = jnp.max(rest, axis=-1, keepdims=True)
            i2 = jnp.min(jnp.where(rest == m2, lane_f, float(LANES)), axis=-1, keepdims=True)
            e21 = jnp.exp(m2 - m1)
            p1 = 1.0 / (1.0 + e21)
            p2 = e21 * p1
            sel1 = lane_f == i1
            sel2 = lane_f == i2
            chosen = jnp.where(sel1 | sel2, 1.0, 0.0)

            r = lax.broadcasted_iota(jnp.int32, (rows, rows), 0)
            c = lax.broadcasted_iota(jnp.int32, (rows, rows), 1)
            before = _dot(jnp.where(c < r, 1.0, 0.0).astype(BF16), chosen.astype(BF16)) + carry_ref[...]
            rank1 = jnp.sum(jnp.where(sel1, before, 0.0), axis=-1, keepdims=True)
            rank2 = jnp.sum(jnp.where(sel2, before, 0.0), axis=-1, keepdims=True)
            carry_ref[...] = carry_ref[...] + jnp.sum(chosen, axis=0, keepdims=True)
            packed = jnp.where(lane == 0, i1, jnp.where(lane == 1, i2,
                               jnp.where(lane == 2, rank1, jnp.where(lane == 3, rank2, 0.0))))
            ri_ref[rs, :] = packed.astype(jnp.int32)
            rp_ref[rs, :] = jnp.where(lane == 0, p1, jnp.where(lane == 1, p2, 0.0))

    zc_ref[0:POOL_HALO, :] = zc_ref[tq:tq + POOL_HALO, :]
    if route:
        cnt_ref[...] = jnp.broadcast_to(carry_ref[...], cnt_ref.shape)


def _merge(x2, g1, w_c, w_g, b_g, oa, lse, ob, pool_w, pool_scale, wpa, wpb, wpc, wo, g2,
           router, s, tq, h2_dtype):
    t, d = x2.shape
    route = router is not None
    row = lambda i: (i, 0)
    const = lambda i: (0, 0)
    resident = lambda a: pl.BlockSpec(a.shape, lambda i: (0,) * a.ndim, pipeline_mode=pl.Buffered(1))
    args = [x2, g1, w_c, w_g, b_g, *oa, *lse, ob, pool_w, pool_scale, wpa, wpb, wpc, wo, g2]
    in_specs = [pl.BlockSpec((tq, d), row), pl.BlockSpec((1, d), const), resident(w_c), resident(w_g),
                pl.BlockSpec(b_g.shape, const)]
    in_specs += [pl.BlockSpec((tq // dil, dil * A_GROUP_WIDTH), row) for (_, dil) in DILATED_PATTERNS] * 2
    in_specs += [pl.BlockSpec((tq, B_WIDTH), row), resident(pool_w), pl.BlockSpec(pool_scale.shape, const),
                 resident(wpa), resident(wpb), resident(wpc), resident(wo), pl.BlockSpec((1, d), const)]
    if route:
        assert d == ROW_TILE * LANES, "a token row must fill exactly one (8,128) f32 tile"
        h2_spec = pl.BlockSpec((tq * ROW_TILE, LANES), row)
        h2_shape = jax.ShapeDtypeStruct((t * ROW_TILE, LANES), F32)
    else:
        h2_spec = pl.BlockSpec((tq, d), row)
        h2_shape = jax.ShapeDtypeStruct((t, d), h2_dtype)
    out_specs = [pl.BlockSpec((tq, d), row), h2_spec]
    out_shape = [jax.ShapeDtypeStruct((t, d), F32), h2_shape]
    scratch = [pltpu.VMEM((POOL_HALO + tq, C_WIDTH), F32), pltpu.VMEM((2 * A_GROUPS * (A_GROUP_WIDTH // LANES), tq, LANES), F32)]
    if route:
        args += list(router)
        in_specs += [resident(router[0]), resident(router[1])]
        out_specs += [pl.BlockSpec((tq, LANES), row), pl.BlockSpec((tq, LANES), row),
                      pl.BlockSpec((8, LANES), const)]
        out_shape += [jax.ShapeDtypeStruct((t, LANES), jnp.int32), jax.ShapeDtypeStruct((t, LANES), F32),
                      jax.ShapeDtypeStruct((8, LANES), F32)]
        scratch += [pltpu.VMEM((1, LANES), F32)]
    return pl.pallas_call(
        functools.partial(_merge_kernel, tq=tq, rows=min(256, tq), tiles_per_seq=s // tq, route=route),
        grid=(t // tq,),
        in_specs=in_specs,
        out_specs=out_specs,
        out_shape=out_shape,
        scratch_shapes=scratch,
        compiler_params=_params(("arbitrary",)),
        name="merge_route" if route else "merge",
    )(*args)


def _swiglu_kernel(te_ref, tv_ref, ts_ref, *refs, residual, nsteps):
    if residual:
        x_ref, res_ref, wg_ref, wu_ref, wd_ref, o_ref, acc_ref = refs
    else:
        x_ref, wg_ref, wu_ref, wd_ref, o_ref, acc_ref = refs
    j = pl.program_id(0)
    c = pl.program_id(1)
    tm, d = acc_ref.shape
    nchunk = d // LANES

    @pl.when((tv_ref[j] == 0) & (c == 0))
    def _():
        o_ref[...] = jnp.zeros_like(o_ref)

    def step(first, last):
        if residual:
            xb = x_ref[...].astype(BF16)
        else:
            xb = jnp.concatenate([x_ref[pl.ds(cb, tm, stride=ROW_TILE), :] for cb in range(nchunk)],
                                 axis=1).astype(BF16)
        gate = _dot(xb, wg_ref[...])
        up = _dot(xb, wu_ref[...])
        mid = (gate * jax.nn.sigmoid(gate) * up).astype(BF16)
        part = _dot(mid, wd_ref[...])
        if not first:
            part = acc_ref[...] + part
        if not last:
            acc_ref[...] = part
        elif residual:
            o_ref[...] = res_ref[...] + part
        else:
            for cb in range(nchunk):
                o_ref[pl.ds(cb, tm, stride=ROW_TILE), :] = part[:, cb * LANES:(cb + 1) * LANES]

    valid = tv_ref[j] > 0
    if nsteps == 1:
        pl.when(valid)(lambda: step(True, True))
    else:
        pl.when(valid & (c == 0))(lambda: step(True, False))
        pl.when(valid & (c == nsteps - 1))(lambda: step(False, True))
        if nsteps > 2:
            pl.when(valid & (c > 0) & (c < nsteps - 1))(lambda: step(False, False))


def _grouped_swiglu(xs, res, wg, wu, wd, tile_expert, tile_valid, tile_src, tm, cf):
    d = wg.shape[1]
    ff = wg.shape[-1]
    residual = res is not None
    xmap = lambda j, c, te, tv, ts: (ts[j], 0)
    if residual:
        n = xs.shape[0]
        row_block = (tm, d)
        in_specs = [pl.BlockSpec(row_block, xmap), pl.BlockSpec(row_block, xmap)]
        args = [xs, res]
    else:
        n = xs.shape[0] // ROW_TILE
        row_block = (tm * ROW_TILE, LANES)
        in_specs = [pl.BlockSpec(row_block, xmap)]
        args = [xs]
    in_specs += [
        pl.BlockSpec((None, d, cf), lambda j, c, te, tv, ts: (te[j], 0, c * tv[j])),
        pl.BlockSpec((None, d, cf), lambda j, c, te, tv, ts: (te[j], 0, c * tv[j])),
        pl.BlockSpec((None, cf, d), lambda j, c, te, tv, ts: (te[j], c * tv[j], 0)),
    ]
    args += [wg, wu, wd]
    return pl.pallas_call(
        functools.partial(_swiglu_kernel, residual=residual, nsteps=ff // cf),
        grid_spec=pltpu.PrefetchScalarGridSpec(
            num_scalar_prefetch=3,
            grid=(n // tm, ff // cf),
            in_specs=in_specs,
            out_specs=pl.BlockSpec(row_block, lambda j, c, te, tv, ts: (j, 0)),
            scratch_shapes=[pltpu.VMEM((tm, d), F32)],
        ),
        out_shape=jax.ShapeDtypeStruct(xs.shape, F32),
        compiler_params=_params(("arbitrary", "arbitrary")),
        name="grouped_swiglu_res" if residual else "grouped_swiglu",
    )(tile_expert, tile_valid, tile_src, *args)


def _dispatch_kernel(zt_ref, slot_ref, h_ref, xs_ref, zero_ref, sem, zsem, *, tq, tm):
    @pl.when(pl.program_id(0) == 0)
    def _():
        zero_ref[...] = jnp.zeros_like(zero_ref)
        for z in range(zt_ref.shape[0]):
            start = pl.multiple_of(zt_ref[z] * (tm * ROW_TILE), tm * ROW_TILE)
            fill = pltpu.make_async_copy(zero_ref, xs_ref.at[pl.ds(start, tm * ROW_TILE), :], zsem.at[0])
            fill.start()
            fill.wait()

    def row_copy(r, k):
        src = pl.multiple_of(r * ROW_TILE, ROW_TILE)
        dst = pl.multiple_of(slot_ref[0, k, r] * ROW_TILE, ROW_TILE)
        return pltpu.make_async_copy(h_ref.at[pl.ds(src, ROW_TILE), :],
                                     xs_ref.at[pl.ds(dst, ROW_TILE), :], sem.at[k])

    def issue(r, carry):
        row_copy(r, 0).start()
        row_copy(r, 1).start()
        return carry

    lax.fori_loop(0, tq, issue, 0, unroll=ISSUE_UNROLL)
    for k in range(TOP_K):
        pltpu.make_async_copy(h_ref, xs_ref.at[pl.ds(0, tq * ROW_TILE), :], sem.at[k]).wait()


def _moe_dispatch(h2, slots, zero_tiles, nslot, tq, tm):
    d = LANES
    return pl.pallas_call(
        functools.partial(_dispatch_kernel, tq=tq, tm=tm),
        grid_spec=pltpu.PrefetchScalarGridSpec(
            num_scalar_prefetch=1,
            grid=(h2.shape[0] // (tq * ROW_TILE),),
            in_specs=[
                pl.BlockSpec((1, TOP_K, tq), lambda i, zt: (i, 0, 0), memory_space=pltpu.SMEM),
                pl.BlockSpec((tq * ROW_TILE, d), lambda i, zt: (i, 0)),
            ],
            out_specs=pl.BlockSpec(memory_space=pl.ANY),
            scratch_shapes=[pltpu.VMEM((tm * ROW_TILE, d), h2.dtype), pltpu.SemaphoreType.DMA((TOP_K,)),
                            pltpu.SemaphoreType.DMA((1,))],
        ),
        out_shape=jax.ShapeDtypeStruct((nslot * ROW_TILE, d), h2.dtype),
        compiler_params=_params(("arbitrary",)),
        name="moe_dispatch",
    )(zero_tiles, slots, h2)


def _combine_kernel(slot_ref, x_ref, rp_ref, g_ref, ys_ref, o_ref, y_ref, sem, *, tq, normalize):
    def row_copy(r, k):
        src = pl.multiple_of(slot_ref[0, k, r] * ROW_TILE, ROW_TILE)
        dst = pl.multiple_of(r * ROW_TILE, ROW_TILE)
        return pltpu.make_async_copy(ys_ref.at[pl.ds(src, ROW_TILE), :],
                                     y_ref.at[k, pl.ds(dst, ROW_TILE), :], sem.at[k])

    def issue(r, carry):
        row_copy(r, 0).start()
        row_copy(r, 1).start()
        return carry

    lax.fori_loop(0, tq, issue, 0, unroll=ISSUE_UNROLL)
    for k in range(TOP_K):
        pltpu.make_async_copy(ys_ref.at[pl.ds(0, tq * ROW_TILE), :], y_ref.at[k], sem.at[k]).wait()
    rp = rp_ref[...]
    y = [jnp.concatenate([y_ref[k, pl.ds(cb, tq, stride=ROW_TILE), :] for cb in range(x_ref.shape[1] // LANES)],
                         axis=1) for k in range(TOP_K)]
    xn = x_ref[...] + rp[:, 0:1] * y[0] + rp[:, 1:2] * y[1]
    o_ref[...] = _rms(xn, g_ref[...]) if normalize else xn


def _moe_combine(x2, route_p, slots, ys, g, tq, normalize):
    t, d = x2.shape
    return pl.pallas_call(
        functools.partial(_combine_kernel, tq=tq, normalize=normalize),
        grid=(t // tq,),
        in_specs=[
            pl.BlockSpec((1, TOP_K, tq), lambda i: (i, 0, 0), memory_space=pltpu.SMEM),
            pl.BlockSpec((tq, d), lambda i: (i, 0)),
            pl.BlockSpec((tq, LANES), lambda i: (i, 0)),
            pl.BlockSpec((1, d), lambda i: (0, 0)),
            pl.BlockSpec(memory_space=pl.ANY),
        ],
        out_specs=pl.BlockSpec((tq, d), lambda i: (i, 0)),
        out_shape=jax.ShapeDtypeStruct((t, d), F32),
        scratch_shapes=[pltpu.VMEM((TOP_K, tq * ROW_TILE, LANES), F32), pltpu.SemaphoreType.DMA((TOP_K,))],
        compiler_params=_params(("arbitrary",)),
        name="moe_combine",
    )(slots, x2, route_p, g, ys)


def _final_norm_kernel(x_ref, g_ref, o_ref):
    o_ref[...] = _rms(x_ref[...], g_ref[...])


def _final_norm(x2, g, tq):
    t, d = x2.shape
    return pl.pallas_call(
        _final_norm_kernel,
        grid=(t // tq,),
        in_specs=[pl.BlockSpec((tq, d), lambda i: (i, 0)), pl.BlockSpec((1, d), lambda i: (0, 0))],
        out_specs=pl.BlockSpec((tq, d), lambda i: (i, 0)),
        out_shape=jax.ShapeDtypeStruct((t, d), F32),
        compiler_params=_params(("arbitrary",)),
        name="final_norm",
    )(x2, g)


def _rope_lane_tables(positions):
    inv_freq = ROPE_THETA ** (-jnp.arange(0, ROT_DIM, 2, dtype=F32) / ROT_DIM)
    ang = positions.astype(F32).reshape(-1, 1) * inv_freq
    cos, sin = jnp.cos(ang), jnp.sin(ang)
    t = ang.shape[0]
    rest = HEAD_DIM - ROT_DIM
    z8 = jnp.zeros((t, ROT_HALF), F32)
    cos_h = jnp.concatenate([cos, cos, jnp.ones((t, rest), F32)], axis=-1)
    lo_h = jnp.concatenate([-sin, z8, jnp.zeros((t, rest), F32)], axis=-1)
    hi_h = jnp.concatenate([z8, sin, jnp.zeros((t, rest), F32)], axis=-1)
    rep = LANES // HEAD_DIM
    return jnp.tile(cos_h, (1, rep)), jnp.tile(lo_h, (1, rep)), jnp.tile(hi_h, (1, rep))


def _moe_plan(route_i, counts, t, tm):
    cnt = counts[0, :N_EXPERTS].astype(jnp.int32)
    padded = ((cnt + tm - 1) // tm) * tm
    ends = jnp.cumsum(padded)
    offs = ends - padded
    slot = offs[route_i[:, 0:TOP_K]] + route_i[:, TOP_K:2 * TOP_K]
    ntile = (TOP_K * t) // tm + N_EXPERTS
    starts = jnp.arange(ntile, dtype=jnp.int32) * tm
    valid = (starts < ends[-1]).astype(jnp.int32)
    last = jnp.maximum(ends[-1] // tm - 1, 0)
    src = jnp.minimum(jnp.arange(ntile, dtype=jnp.int32), last)
    expert = jnp.sum(((src * tm)[:, None] >= ends[None, :]).astype(jnp.int32), axis=1)
    expert = jnp.minimum(expert, N_EXPERTS - 1)
    last_tile = jnp.where(padded > 0, ends // tm - 1, ntile - 1).astype(jnp.int32)
    tail = jnp.arange((TOP_K * t) // tm, ntile, dtype=jnp.int32)
    return slot, expert, valid, src, jnp.concatenate([last_tile, tail]), ntile * tm


def kernel(x, positions, norm1_g, w_in, b_gate, diff_lambda, diff_subln_g, pool_w, pool_scale,
           w_proj_a, w_proj_b, w_proj_c, w_out, norm2_g, ffn_w_gate, ffn_w_up, ffn_w_down,
           moe_router, moe_w_gate, moe_w_up, moe_w_down, final_norm_g):
    b, s, d = x.shape
    t = b * s
    depth = w_in.shape[0]
    tq = min(512, s)
    tm = min(512, s)
    ff = ffn_w_gate.shape[-1]
    cf = ff // 2 if ff % 512 == 0 and ff >= 1024 else ff
    x2 = x.reshape(t, d)
    cos_t, sinlo_t, sinhi_t = _rope_lane_tables(positions)
    out = None
    for l in range(depth):
        w_l = w_in[l]
        qkv = _qkv_proj(x2, norm1_g[l].reshape(1, d), w_l[:, :OFF_C].astype(BF16),
                        cos_t, sinlo_t, sinhi_t, tq)
        qa, ka, va = qkv[0:3], qkv[3:6], qkv[6:9]
        qb, kb, vb = qkv[9:12]
        oa, lse = [], []
        for g, (window, dil) in enumerate(DILATED_PATTERNS):
            assert window // dil == BLOCK
            o_g, lse_g = _dilated_attn(qa[g], ka[g], va[g], b, s, dil, 512)
            oa.append(o_g)
            lse.append(lse_g)
        lambda_init = 0.8 - 0.6 * math.exp(-0.3 * l)
        ob = _diff_attn(qb, kb, vb, diff_lambda[l], diff_subln_g[l].reshape(1, B_HEAD_WIDTH),
                        b, s, min(1024, s), lambda_init)

        dense = l % 2 == 0
        router = None
        if not dense:
            wr = jnp.zeros((d, LANES), F32).at[:, :N_EXPERTS].set(moe_router[l // 2])
            wr_hi = wr.astype(BF16)
            router = (wr_hi, (wr - wr_hi.astype(F32)).astype(BF16))
        merged = _merge(
            x2, norm1_g[l].reshape(1, d), w_l[:, OFF_C:OFF_G].astype(BF16), w_l[:, OFF_G:].astype(BF16),
            b_gate[l].reshape(1, N_BRANCH * d), oa, lse, ob, pool_w[l].astype(BF16),
            pool_scale[l].reshape(1, C_WIDTH), w_proj_a[l].astype(BF16), w_proj_b[l].astype(BF16),
            w_proj_c[l].astype(BF16), w_out[l].astype(BF16), norm2_g[l].reshape(1, d),
            router, s, tq, BF16 if dense else F32)
        if dense:
            xn, h2 = merged
            i = l // 2
            ntile = t // tm
            ident = jnp.arange(ntile, dtype=jnp.int32)
            x2 = _grouped_swiglu(h2, xn, ffn_w_gate[i:i + 1].astype(BF16), ffn_w_up[i:i + 1].astype(BF16),
                                 ffn_w_down[i:i + 1].astype(BF16), jnp.zeros((ntile,), jnp.int32),
                                 jnp.ones((ntile,), jnp.int32), ident, tm, cf)
            out = None
        else:
            xn, h2, route_i, route_p, counts = merged
            i = l // 2
            slot, expert, valid, src, zero_tiles, nslot = _moe_plan(route_i, counts, t, tm)
            tq_d, tq_c = min(2048, t), min(1024, t)
            tiled = lambda n: slot.reshape(t // n, n, TOP_K).transpose(0, 2, 1)
            xs = _moe_dispatch(h2, tiled(tq_d), zero_tiles, nslot, tq_d, tm)
            ys = _grouped_swiglu(xs, None, moe_w_gate[i].astype(BF16), moe_w_up[i].astype(BF16),
                                 moe_w_down[i].astype(BF16), expert, valid, src, tm, cf)
            last = l == depth - 1
            res = _moe_combine(xn, route_p, tiled(tq_c), ys, final_norm_g.reshape(1, d), tq_c, last)
            if last:
                out = res
            else:
                x2 = res
    if out is None:
        out = _final_norm(x2, final_norm_g.reshape(1, d), tq)
    return out.reshape(b, s, d)
```

```python
import functools
import math

import jax
import jax.numpy as jnp
from jax import lax
from jax.experimental import pallas as pl
from jax.experimental.pallas import tpu as pltpu

F32 = jnp.float32
BF16 = jnp.bfloat16

HEAD_DIM = 64
ROPE_THETA = 500000.0
ROT_DIM = HEAD_DIM // 4
ROT_HALF = ROT_DIM // 2
BLOCK = 128
EPS = 1e-6
NEG = -1e30

DILATED_PATTERNS = ((128, 1), (512, 4), (2048, 16))
A_GROUPS = len(DILATED_PATTERNS)
A_HEADS_PER_GROUP = 4
A_GROUP_WIDTH = A_HEADS_PER_GROUP * HEAD_DIM
A_WIDTH = A_GROUPS * A_GROUP_WIDTH

B_HEADS = 4
B_HEAD_WIDTH = 2 * HEAD_DIM
B_WIDTH = B_HEADS * B_HEAD_WIDTH
B_VEXT = 2 * B_HEAD_WIDTH

POOL_WINDOWS = (2, 4, 8, 16)
C_GROUP_DIM = 128
C_WIDTH = len(POOL_WINDOWS) * C_GROUP_DIM
POOL_HALO = 16

N_BRANCH = 3
N_EXPERTS = 8
TOP_K = 2

OFF_QA = 0
OFF_KA = OFF_QA + A_WIDTH
OFF_VA = OFF_KA + A_WIDTH
OFF_QB = OFF_VA + A_WIDTH
OFF_KB = OFF_QB + B_WIDTH
OFF_VB = OFF_KB + B_WIDTH
OFF_C = OFF_VB + B_WIDTH
OFF_G = OFF_C + C_WIDTH

LANES = 128
MXU_WIDTH = 256
ROW_TILE = 8
ISSUE_UNROLL = 16
V7X_VMEM_BYTES = 64 * 1024 * 1024
VMEM_LIMIT = V7X_VMEM_BYTES - 8 * 1024 * 1024

QK_SCALE = HEAD_DIM ** -0.5
LOG2_E = math.log2(math.e)


def _params(semantics):
    return pltpu.CompilerParams(dimension_semantics=semantics, vmem_limit_bytes=VMEM_LIMIT)


def _rms(x, g):
    return x * lax.rsqrt(jnp.mean(x * x, axis=-1, keepdims=True) + EPS) * g


def _dot(a, b):
    return jnp.dot(a, b, preferred_element_type=F32)


def _dot_nt(a, b):
    return lax.dot_general(a, b, (((1,), (1,)), ((), ())), preferred_element_type=F32)


def _qkv_kernel(x_ref, g_ref, w_ref, cos_ref, sinlo_ref, sinhi_ref,
                qa0, qa1, qa2, ka0, ka1, ka2, va0, va1, va2, qb, kb, vb, zs_ref):
    tq = x_ref.shape[0]
    hb = _rms(x_ref[...], g_ref[...]).astype(BF16)
    cos = cos_ref[...]
    sinlo = sinlo_ref[...]
    sinhi = sinhi_ref[...]

    def rope(z):
        return (z * cos + pltpu.roll(z, LANES - ROT_HALF, 1) * sinlo
                + pltpu.roll(z, ROT_HALF, 1) * sinhi)

    def project(off, width, out_ref, rotary, scale, dil=1):
        for c in range(0, width, MXU_WIDTH):
            zz = _dot(hb, w_ref[:, off + c:off + c + MXU_WIDTH])
            for p in range(0, MXU_WIDTH, LANES):
                z = zz[:, p:p + LANES]
                if rotary:
                    z = rope(z)
                if scale != 1.0:
                    z = z * scale
                if dil == 1:
                    out_ref[:, c + p:c + p + LANES] = z.astype(out_ref.dtype)
                else:
                    zs_ref[(c + p) // LANES] = z
        if dil > 1:
            for r in range(dil):
                for sl in range(width // LANES):
                    out_ref[:, r * width + sl * LANES:r * width + (sl + 1) * LANES] = (
                        zs_ref[sl, pl.ds(r, tq // dil, stride=dil), :].astype(out_ref.dtype))

    for g, (q_ref, k_ref, v_ref) in enumerate(((qa0, ka0, va0), (qa1, ka1, va1), (qa2, ka2, va2))):
        dil = DILATED_PATTERNS[g][1]
        project(OFF_QA + g * A_GROUP_WIDTH, A_GROUP_WIDTH, q_ref, True, QK_SCALE * LOG2_E, dil)
        project(OFF_KA + g * A_GROUP_WIDTH, A_GROUP_WIDTH, k_ref, True, 1.0, dil)
        project(OFF_VA + g * A_GROUP_WIDTH, A_GROUP_WIDTH, v_ref, False, 1.0, dil)
    project(OFF_QB, B_WIDTH, qb, True, QK_SCALE * LOG2_E)
    project(OFF_KB, B_WIDTH, kb, True, 1.0)
    lane = lax.broadcasted_iota(jnp.int32, (x_ref.shape[0], B_HEAD_WIDTH), 1)
    ones_col = jnp.where(lane == 0, 1.0, 0.0).astype(vb.dtype)
    heads_per_dot = MXU_WIDTH // B_HEAD_WIDTH
    for h0 in range(0, B_HEADS, heads_per_dot):
        zz = _dot(hb, w_ref[:, OFF_VB + h0 * B_HEAD_WIDTH:OFF_VB + (h0 + heads_per_dot) * B_HEAD_WIDTH])
        for hh in range(heads_per_dot):
            h = h0 + hh
            z = zz[:, hh * B_HEAD_WIDTH:(hh + 1) * B_HEAD_WIDTH]
            vb[:, h * B_VEXT:h * B_VEXT + B_HEAD_WIDTH] = z.astype(vb.dtype)
            vb[:, h * B_VEXT + B_HEAD_WIDTH:(h + 1) * B_VEXT] = ones_col


def _qkv_proj(x2, g, w_qkv, cos_t, sinlo_t, sinhi_t, tq):
    t, d = x2.shape
    row = lambda i: (i, 0)
    const = lambda i: (0, 0)
    dils = [dil for _ in range(3) for (_, dil) in DILATED_PATTERNS]
    shapes = [(t // dil, dil * A_GROUP_WIDTH, tq // dil) for dil in dils]
    shapes += [(t, B_WIDTH, tq), (t, B_WIDTH, tq), (t, B_HEADS * B_VEXT, tq)]
    return pl.pallas_call(
        _qkv_kernel,
        grid=(t // tq,),
        in_specs=[
            pl.BlockSpec((tq, d), row),
            pl.BlockSpec((1, d), const),
            pl.BlockSpec(w_qkv.shape, const, pipeline_mode=pl.Buffered(1)),
            pl.BlockSpec((tq, LANES), row),
            pl.BlockSpec((tq, LANES), row),
            pl.BlockSpec((tq, LANES), row),
        ],
        out_specs=[pl.BlockSpec((rows, w), row) for (_, w, rows) in shapes],
        out_shape=[jax.ShapeDtypeStruct((n, w), BF16) for (n, w, _) in shapes],
        scratch_shapes=[pltpu.VMEM((A_GROUP_WIDTH // LANES, tq, LANES), F32)],
        compiler_params=_params(("arbitrary",)),
        name="qkv_proj",
    )(x2, g, w_qkv, cos_t, sinlo_t, sinhi_t)


def _dilated_kernel(q_ref, kp_ref, kc_ref, vp_ref, vc_ref, o_ref, lse_ref, *, nsub):
    n = pl.program_id(2)
    a = lax.broadcasted_iota(jnp.int32, (BLOCK, 2 * BLOCK), 0)
    j = lax.broadcasted_iota(jnp.int32, (BLOCK, 2 * BLOCK), 1)
    band = (j >= a) & (j <= a + BLOCK)
    band_first = band & ((j >= BLOCK) | (n > 0))
    low_k = lax.broadcasted_iota(jnp.int32, (2 * BLOCK, LANES), 1) < HEAD_DIM
    low_o = lax.broadcasted_iota(jnp.int32, (BLOCK, LANES), 1) < HEAD_DIM
    ones_slab = jnp.ones((2 * BLOCK, LANES), BF16)
    for sb in range(nsub):
        rows = slice(sb * BLOCK, (sb + 1) * BLOCK)
        if sb == 0:
            kcat = jnp.concatenate([kp_ref[...], kc_ref[rows, :]], axis=0)
            vcat = jnp.concatenate([vp_ref[...], vc_ref[rows, :]], axis=0)
            mask = band_first
        else:
            kcat = kc_ref[(sb - 1) * BLOCK:(sb + 1) * BLOCK, :]
            vcat = vc_ref[(sb - 1) * BLOCK:(sb + 1) * BLOCK, :]
            mask = band
        q = q_ref[rows, :]
        for hp in range(A_HEADS_PER_GROUP // 2):
            pair = slice(hp * LANES, (hp + 1) * LANES)
            q_pair = q[:, pair]
            k_pair = kcat[:, pair]
            v_ext = jnp.concatenate([vcat[:, pair], ones_slab], axis=1)
            o_half, lse_half = [], []
            for half in range(2):
                k_h = jnp.where(low_k if half == 0 else ~low_k, k_pair, jnp.zeros_like(k_pair))
                s = jnp.where(mask, _dot_nt(q_pair, k_h), NEG)
                m = jnp.max(jnp.maximum(s[:, :LANES], s[:, LANES:]), axis=-1, keepdims=True)
                p = jnp.concatenate([jnp.exp2(s[:, :LANES] - m), jnp.exp2(s[:, LANES:] - m)], axis=1)
                pv = _dot(p.astype(BF16), v_ext)
                l = pv[:, LANES:]
                o_half.append(pv[:, :LANES] / l)
                lse_half.append(m + jnp.log2(l))
            o_ref[rows, pair] = jnp.where(low_o, o_half[0], o_half[1]).astype(o_ref.dtype)
            lse_ref[rows, pair] = jnp.where(low_o, lse_half[0], lse_half[1])


def _dilated_attn(q, k, v, b, s, dil, qrows):
    w = A_GROUP_WIDTH
    l = s // dil
    qrows = min(qrows, l)
    nsub = qrows // BLOCK
    view = lambda t: t.reshape(b, l, dil * w)
    cur = lambda bi, r, n: (bi, n, r)
    prev = lambda bi, r, n: (bi, jnp.maximum(n * nsub - 1, 0), r)
    o, lse = pl.pallas_call(
        functools.partial(_dilated_kernel, nsub=nsub),
        grid=(b, dil, l // qrows),
        in_specs=[
            pl.BlockSpec((None, qrows, w), cur),
            pl.BlockSpec((None, BLOCK, w), prev),
            pl.BlockSpec((None, qrows, w), cur),
            pl.BlockSpec((None, BLOCK, w), prev),
            pl.BlockSpec((None, qrows, w), cur),
        ],
        out_specs=[pl.BlockSpec((None, qrows, w), cur), pl.BlockSpec((None, qrows, w), cur)],
        out_shape=[jax.ShapeDtypeStruct((b, l, dil * w), BF16),
                   jax.ShapeDtypeStruct((b, l, dil * w), F32)],
        compiler_params=_params(("arbitrary", "arbitrary", "arbitrary")),
        name=f"dilated_attn_d{dil}",
    )(view(q), view(k), view(k), view(v), view(v))
    return o.reshape(b * l, dil * w), lse.reshape(b * l, dil * w)


def _diff_kernel(lam_ref, g_ref, q_ref, k_ref, v_ref, o_ref, *scratch, tq, tk, rows, lambda_init):
    i = pl.program_id(2)
    nd = tq // tk
    q = q_ref[...]
    streams = [(mp, rb) for rb in range(tq // rows) for mp in range(2)]
    acc_all, m_all = scratch[0], scratch[1]
    acc_all[...] = jnp.zeros_like(acc_all)
    m_all[...] = jnp.full_like(m_all, NEG)
    acc_refs = {(mp, rb): acc_all.at[mp, rb] for mp, rb in streams}
    m_refs = {(mp, rb): m_all.at[mp, rb] for mp, rb in streams}
    sa_ref = {(mp, rb): scratch[2].at[mp, rb] for mp, rb in streams}
    sb_ref = {(mp, rb): scratch[3].at[mp, rb] for mp, rb in streams}

    def visible(rb, diag):
        if diag is None:
            return tk
        return max(0, min(tk, (rb + 1) * rows - diag * tk))

    def scores(jc, dst, diag=None):
        k = k_ref[pl.ds(pl.multiple_of(jc * tk, tk), tk), :]
        for mp, rb in streams:
            if visible(rb, diag) == 0:
                continue
            cols = slice(mp * HEAD_DIM, (mp + 1) * HEAD_DIM)
            rs = slice(rb * rows, (rb + 1) * rows)
            dst[mp, rb][...] = _dot_nt(q[rs, cols], k[:, cols])

    def consume(jc, src, diag=None):
        start = pl.multiple_of(jc * tk, tk)
        for mp, rb in streams:
            vis = visible(rb, diag)
            if vis == 0:
                continue
            acc_ref, m_ref = acc_refs[mp, rb], m_refs[mp, rb]
            s = src[mp, rb][:, :vis]
            if diag is not None and diag * tk + vis > rb * rows + 1:
                r = lax.broadcasted_iota(jnp.int32, (rows, vis), 0) + rb * rows
                c = lax.broadcasted_iota(jnp.int32, (rows, vis), 1) + diag * tk
                s = jnp.where(c <= r, s, NEG)
            slabs = [s[:, n * LANES:(n + 1) * LANES] for n in range(vis // LANES)]
            smax = slabs[0]
            for sl in slabs[1:]:
                smax = jnp.maximum(smax, sl)
            m_old = m_ref[...]
            m_new = jnp.maximum(m_old, jnp.max(smax, axis=-1, keepdims=True))
            alpha = jnp.exp2(m_old - m_new)
            p = jnp.concatenate([jnp.exp2(sl - m_new) for sl in slabs], axis=1).astype(BF16)
            pv = _dot(p, v_ref[pl.ds(start, vis), :])
            acc = acc_ref[...]
            acc_ref[...] = jnp.concatenate(
                [acc[:, n * LANES:(n + 1) * LANES] * alpha for n in range(B_VEXT // LANES)], axis=1) + pv
            m_ref[...] = m_new

    scores(0, sa_ref)

    def pair(t, carry):
        jc = 2 * t
        scores(jc + 1, sb_ref)
        consume(jc, sa_ref)
        scores(jc + 2, sa_ref)
        consume(jc + 1, sb_ref)
        return carry

    lax.fori_loop(0, i * (nd // 2), pair, 0)
    first = nd * i
    bufs = (sa_ref, sb_ref)
    for dg in range(nd):
        if dg + 1 < nd:
            scores(first + dg + 1, bufs[(dg + 1) % 2], diag=dg + 1)
        consume(first + dg, bufs[dg % 2], diag=dg)

    lp = lam_ref[...]
    lam = (jnp.exp(jnp.sum(lp[0:1] * lp[1:2], axis=-1, keepdims=True))
           - jnp.exp(jnp.sum(lp[2:3] * lp[3:4], axis=-1, keepdims=True)) + lambda_init)
    for rb in range(tq // rows):
        acc1 = acc_refs[0, rb][...]
        acc2 = acc_refs[1, rb][...]
        o1 = acc1[:, :B_HEAD_WIDTH] / acc1[:, B_HEAD_WIDTH:B_HEAD_WIDTH + 1]
        o2 = acc2[:, :B_HEAD_WIDTH] / acc2[:, B_HEAD_WIDTH:B_HEAD_WIDTH + 1]
        o = o1 - lam * o2
        o_ref[rb * rows:(rb + 1) * rows, :] = (_rms(o, g_ref[...]) * (1.0 - lambda_init)).astype(o_ref.dtype)


def _diff_attn(qb, kb, vb, lam_p, subln_g, b, s, tq, tk, rows, lambda_init):
    hw = B_HEAD_WIDTH
    q3 = qb.reshape(b, s, B_WIDTH)
    k3 = kb.reshape(b, s, B_WIDTH)
    v3 = vb.reshape(b, s, B_HEADS * B_VEXT)
    assert tq % (2 * tk) == 0 and tk % rows == 0
    scratch = ([pltpu.VMEM((2, tq // rows, rows, B_VEXT), F32), pltpu.VMEM((2, tq // rows, rows, LANES), F32)]
               + [pltpu.VMEM((2, tq // rows, rows, tk), F32)] * 2)
    out = pl.pallas_call(
        functools.partial(_diff_kernel, tq=tq, tk=tk, rows=rows, lambda_init=lambda_init),
        grid=(b, B_HEADS, s // tq),
        in_specs=[
            pl.BlockSpec(lam_p.shape, lambda bi, h, i: (0, 0)),
            pl.BlockSpec((1, hw), lambda bi, h, i: (0, 0)),
            pl.BlockSpec((None, tq, hw), lambda bi, h, i: (bi, i, h)),
            pl.BlockSpec((None, s, hw), lambda bi, h, i: (bi, 0, h)),
            pl.BlockSpec((None, s, B_VEXT), lambda bi, h, i: (bi, 0, h)),
        ],
        out_specs=pl.BlockSpec((None, tq, hw), lambda bi, h, i: (bi, i, h)),
        out_shape=jax.ShapeDtypeStruct((b, s, B_WIDTH), BF16),
        scratch_shapes=scratch,
        compiler_params=_params(("arbitrary", "arbitrary", "arbitrary")),
        name="diff_attn",
    )(lam_p, subln_g, q3, k3, v3)
    return out.reshape(b * s, B_WIDTH)


def _merge_kernel(*refs, tq, rows, tiles_per_seq, route):
    (x_ref, g1_ref, wc_ref, wg_ref, bg_ref, oa0, oa1, oa2, ls0, ls1, ls2, ob_ref,
     pw_ref, ps_ref, wpa_ref, wpb_ref, wpc_ref, wo_ref, g2_ref) = refs[:19]
    if route:
        wrh_ref, wrl_ref = refs[19:21]
        xo_ref, h2_ref, ri_ref, rp_ref, cnt_ref, zc_ref, tm_ref, carry_ref = refs[21:]
    else:
        xo_ref, h2_ref, zc_ref, tm_ref = refs[19:]
    step = pl.program_id(0)
    seq_tile = step % tiles_per_seq
    d = x_ref.shape[-1]
    dils = [dil for (_, dil) in DILATED_PATTERNS]
    slabs = A_GROUP_WIDTH // LANES

    @pl.when(seq_tile == 0)
    def _():
        zc_ref[0:POOL_HALO, :] = jnp.zeros((POOL_HALO, C_WIDTH), F32)

    if route:
        @pl.when(step == 0)
        def _():
            carry_ref[...] = jnp.zeros_like(carry_ref)

    for part in range(tq // rows):
        r0 = part * rows
        rs = slice(r0, r0 + rows)
        x = x_ref[rs, :]
        hb = _rms(x, g1_ref[...]).astype(BF16)

        zc_ref[POOL_HALO + r0:POOL_HALO + r0 + rows, :] = _dot(hb, wc_ref[...])
        pos = seq_tile * tq + r0 + lax.broadcasted_iota(jnp.int32, (rows, C_GROUP_DIM), 0)
        pooled = []
        for gi, win in enumerate(POOL_WINDOWS):
            cols = slice(gi * C_GROUP_DIM, (gi + 1) * C_GROUP_DIM)
            tok = zc_ref[POOL_HALO + r0:POOL_HALO + r0 + rows, cols]
            tot = tok
            for back in range(1, win):
                tot = tot + zc_ref[POOL_HALO + r0 - back:POOL_HALO + r0 - back + rows, cols]
            cnt = jnp.minimum(pos + 1, win).astype(F32)
            dmean = tot / cnt - tok
            pooled.append(_dot(dmean.astype(BF16), pw_ref[gi]))
        out_c = (jnp.concatenate(pooled, axis=-1) * ps_ref[...]).astype(BF16)

        def token_major(src_ref, slot, dil):
            if dil == 1:
                return src_ref[rs, :].astype(F32)
            src_rows = slice(r0 // dil, (r0 + rows) // dil)
            for r in range(dil):
                for sl in range(slabs):
                    col = r * A_GROUP_WIDTH + sl * LANES
                    tm_ref[slot * slabs + sl, pl.ds(r0 + r, rows // dil, stride=dil), :] = (
                        src_ref[src_rows, col:col + LANES].astype(F32))
            return jnp.concatenate([tm_ref[slot * slabs + sl, rs, :] for sl in range(slabs)], axis=1)

        l0, l1, l2 = (token_major(ref, gi, dil) for gi, (ref, dil) in enumerate(zip((ls0, ls1, ls2), dils)))
        o0, o1, o2 = (token_major(ref, A_GROUPS + gi, dil)
                      for gi, (ref, dil) in enumerate(zip((oa0, oa1, oa2), dils)))
        lm = jnp.maximum(jnp.maximum(l0, l1), l2)
        e0, e1, e2 = jnp.exp2(l0 - lm), jnp.exp2(l1 - lm), jnp.exp2(l2 - lm)
        out_a = ((e0 * o0 + e1 * o1 + e2 * o2) / (e0 + e1 + e2)).astype(BF16)

        branches = (_dot(out_a, wpa_ref[...]), _dot(ob_ref[rs, :], wpb_ref[...]), _dot(out_c, wpc_ref[...]))
        mixed = None
        for bi, proj in enumerate(branches):
            zg = _dot(hb, wg_ref[:, bi * d:(bi + 1) * d]) + bg_ref[:, bi * d:(bi + 1) * d]
            term = jax.nn.sigmoid(zg) * proj
            mixed = term if mixed is None else mixed + term
        xn = x + _dot(mixed.astype(BF16), wo_ref[...])
        xo_ref[rs, :] = xn
        h2 = _rms(xn, g2_ref[...])
        if route:
            for cb in range(d // LANES):
                h2_ref[pl.ds(r0 * ROW_TILE + cb, rows, stride=ROW_TILE), :] = h2[:, cb * LANES:(cb + 1) * LANES]
        else:
            h2_ref[rs, :] = h2.astype(h2_ref.dtype)

        if route:
            h_hi = h2.astype(BF16)
            h_lo = (h2 - h_hi.astype(F32)).astype(BF16)
            logits = _dot(h_hi, wrh_ref[...]) + (_dot(h_lo, wrh_ref[...]) + _dot(h_hi, wrl_ref[...]))
            lane = lax.broadcasted_iota(jnp.int32, (rows, LANES), 1)
            lane_f = lane.astype(F32)
            logits = jnp.where(lane < N_EXPERTS, logits, -jnp.inf)
            m1 = jnp.max(logits, axis=-1, keepdims=True)
            i1 = jnp.min(jnp.where(logits == m1, lane_f, float(LANES)), axis=-1, keepdims=True)
            rest = jnp.where(lane_f == i1, -jnp.inf, logits)
            m2 = jnp.max(rest, axis=-1, keepdims=True)
            i2 = jnp.min(jnp.where(rest == m2, lane_f, float(LANES)), axis=-1, keepdims=True)
            e21 = jnp.exp(m2 - m1)
            p1 = 1.0 / (1.0 + e21)
            p2 = e21 * p1
            sel1 = lane_f == i1
            sel2 = lane_f == i2
            chosen = jnp.where(sel1 | sel2, 1.0, 0.0)

            r = lax.broadcasted_iota(jnp.int32, (rows, rows), 0)
            c = lax.broadcasted_iota(jnp.int32, (rows, rows), 1)
            before = _dot(jnp.where(c < r, 1.0, 0.0).astype(BF16), chosen.astype(BF16)) + carry_ref[...]
            rank1 = jnp.sum(jnp.where(sel1, before, 0.0), axis=-1, keepdims=True)
            rank2 = jnp.sum(jnp.where(sel2, before, 0.0), axis=-1, keepdims=True)
            carry_ref[...] = carry_ref[...] + jnp.sum(chosen, axis=0, keepdims=True)
            packed = jnp.where(lane == 0, i1, jnp.where(lane == 1, i2,
                               jnp.where(lane == 2, rank1, jnp.where(lane == 3, rank2, 0.0))))
            ri_ref[rs, :] = packed.astype(jnp.int32)
            rp_ref[rs, :] = jnp.where(lane == 0, p1, jnp.where(lane == 1, p2, 0.0))

    zc_ref[0:POOL_HALO, :] = zc_ref[tq:tq + POOL_HALO, :]
    if route:
        cnt_ref[...] = jnp.broadcast_to(carry_ref[...], cnt_ref.shape)


def _merge(x2, g1, w_c, w_g, b_g, oa, lse, ob, pool_w, pool_scale, wpa, wpb, wpc, wo, g2,
           router, s, tq, h2_dtype):
    t, d = x2.shape
    route = router is not None
    row = lambda i: (i, 0)
    const = lambda i: (0, 0)
    resident = lambda a: pl.BlockSpec(a.shape, lambda i: (0,) * a.ndim, pipeline_mode=pl.Buffered(1))
    args = [x2, g1, w_c, w_g, b_g, *oa, *lse, ob, pool_w, pool_scale, wpa, wpb, wpc, wo, g2]
    in_specs = [pl.BlockSpec((tq, d), row), pl.BlockSpec((1, d), const), resident(w_c), resident(w_g),
                pl.BlockSpec(b_g.shape, const)]
    in_specs += [pl.BlockSpec((tq // dil, dil * A_GROUP_WIDTH), row) for (_, dil) in DILATED_PATTERNS] * 2
    in_specs += [pl.BlockSpec((tq, B_WIDTH), row), resident(pool_w), pl.BlockSpec(pool_scale.shape, const),
                 resident(wpa), resident(wpb), resident(wpc), resident(wo), pl.BlockSpec((1, d), const)]
    if route:
        assert d == ROW_TILE * LANES, "a token row must fill exactly one (8,128) f32 tile"
        h2_spec = pl.BlockSpec((tq * ROW_TILE, LANES), row)
        h2_shape = jax.ShapeDtypeStruct((t * ROW_TILE, LANES), F32)
    else:
        h2_spec = pl.BlockSpec((tq, d), row)
        h2_shape = jax.ShapeDtypeStruct((t, d), h2_dtype)
    out_specs = [pl.BlockSpec((tq, d), row), h2_spec]
    out_shape = [jax.ShapeDtypeStruct((t, d), F32), h2_shape]
    scratch = [pltpu.VMEM((POOL_HALO + tq, C_WIDTH), F32), pltpu.VMEM((2 * A_GROUPS * (A_GROUP_WIDTH // LANES), tq, LANES), F32)]
    if route:
        args += list(router)
        in_specs += [resident(router[0]), resident(router[1])]
        out_specs += [pl.BlockSpec((tq, LANES), row), pl.BlockSpec((tq, LANES), row),
                      pl.BlockSpec((8, LANES), const)]
        out_shape += [jax.ShapeDtypeStruct((t, LANES), jnp.int32), jax.ShapeDtypeStruct((t, LANES), F32),
                      jax.ShapeDtypeStruct((8, LANES), F32)]
        scratch += [pltpu.VMEM((1, LANES), F32)]
    return pl.pallas_call(
        functools.partial(_merge_kernel, tq=tq, rows=min(256, tq), tiles_per_seq=s // tq, route=route),
        grid=(t // tq,),
        in_specs=in_specs,
        out_specs=out_specs,
        out_shape=out_shape,
        scratch_shapes=scratch,
        compiler_params=_params(("arbitrary",)),
        name="merge_route" if route else "merge",
    )(*args)


def _swiglu_kernel(te_ref, tv_ref, ts_ref, *refs, residual, nsteps):
    if residual:
        x_ref, res_ref, wg_ref, wu_ref, wd_ref, o_ref, acc_ref = refs
    else:
        x_ref, wg_ref, wu_ref, wd_ref, o_ref, acc_ref = refs
    j = pl.program_id(0)
    c = pl.program_id(1)
    tm, d = acc_ref.shape
    nchunk = d // LANES

    @pl.when((tv_ref[j] == 0) & (c == 0))
    def _():
        o_ref[...] = jnp.zeros_like(o_ref)

    def step(first, last):
        if residual:
            xb = x_ref[...].astype(BF16)
        else:
            xb = jnp.concatenate([x_ref[pl.ds(cb, tm, stride=ROW_TILE), :] for cb in range(nchunk)],
                                 axis=1).astype(BF16)
        gate = _dot(xb, wg_ref[...])
        up = _dot(xb, wu_ref[...])
        mid = (gate * jax.nn.sigmoid(gate) * up).astype(BF16)
        part = _dot(mid, wd_ref[...])
        if not first:
            part = acc_ref[...] + part
        if not last:
            acc_ref[...] = part
        elif residual:
            o_ref[...] = res_ref[...] + part
        else:
            for cb in range(nchunk):
                o_ref[pl.ds(cb, tm, stride=ROW_TILE), :] = part[:, cb * LANES:(cb + 1) * LANES]

    valid = tv_ref[j] > 0
    if nsteps == 1:
        pl.when(valid)(lambda: step(True, True))
    else:
        pl.when(valid & (c == 0))(lambda: step(True, False))
        pl.when(valid & (c == nsteps - 1))(lambda: step(False, True))
        if nsteps > 2:
            pl.when(valid & (c > 0) & (c < nsteps - 1))(lambda: step(False, False))


def _grouped_swiglu(xs, res, wg, wu, wd, tile_expert, tile_valid, tile_src, tm, cf):
    d = wg.shape[1]
    ff = wg.shape[-1]
    residual = res is not None
    xmap = lambda j, c, te, tv, ts: (ts[j], 0)
    if residual:
        n = xs.shape[0]
        row_block = (tm, d)
        in_specs = [pl.BlockSpec(row_block, xmap), pl.BlockSpec(row_block, xmap)]
        args = [xs, res]
    else:
        n = xs.shape[0] // ROW_TILE
        row_block = (tm * ROW_TILE, LANES)
        in_specs = [pl.BlockSpec(row_block, xmap)]
        args = [xs]
    in_specs += [
        pl.BlockSpec((None, d, cf), lambda j, c, te, tv, ts: (te[j], 0, c * tv[j])),
        pl.BlockSpec((None, d, cf), lambda j, c, te, tv, ts: (te[j], 0, c * tv[j])),
        pl.BlockSpec((None, cf, d), lambda j, c, te, tv, ts: (te[j], c * tv[j], 0)),
    ]
    args += [wg, wu, wd]
    return pl.pallas_call(
        functools.partial(_swiglu_kernel, residual=residual, nsteps=ff // cf),
        grid_spec=pltpu.PrefetchScalarGridSpec(
            num_scalar_prefetch=3,
            grid=(n // tm, ff // cf),
            in_specs=in_specs,
            out_specs=pl.BlockSpec(row_block, lambda j, c, te, tv, ts: (j, 0)),
            scratch_shapes=[pltpu.VMEM((tm, d), F32)],
        ),
        out_shape=jax.ShapeDtypeStruct(xs.shape, F32),
        compiler_params=_params(("arbitrary", "arbitrary")),
        name="grouped_swiglu_res" if residual else "grouped_swiglu",
    )(tile_expert, tile_valid, tile_src, *args)


def _dispatch_kernel(zt_ref, slot_ref, h_ref, xs_ref, zero_ref, sem, zsem, *, tq, tm):
    @pl.when(pl.program_id(0) == 0)
    def _():
        zero_ref[...] = jnp.zeros_like(zero_ref)
        for z in range(zt_ref.shape[0]):
            start = pl.multiple_of(zt_ref[z] * (tm * ROW_TILE), tm * ROW_TILE)
            fill = pltpu.make_async_copy(zero_ref, xs_ref.at[pl.ds(start, tm * ROW_TILE), :], zsem.at[0])
            fill.start()
            fill.wait()

    def row_copy(r, k):
        src = pl.multiple_of(r * ROW_TILE, ROW_TILE)
        dst = pl.multiple_of(slot_ref[0, k, r] * ROW_TILE, ROW_TILE)
        return pltpu.make_async_copy(h_ref.at[pl.ds(src, ROW_TILE), :],
                                     xs_ref.at[pl.ds(dst, ROW_TILE), :], sem.at[k])

    def issue(r, carry):
        row_copy(r, 0).start(priority=0)
        row_copy(r, 1).start(priority=1)
        return carry

    lax.fori_loop(0, tq, issue, 0, unroll=ISSUE_UNROLL)
    for k in range(TOP_K):
        pltpu.make_async_copy(h_ref, xs_ref.at[pl.ds(0, tq * ROW_TILE), :], sem.at[k]).wait()


def _moe_dispatch(h2, slots, zero_tiles, nslot, tq, tm):
    d = LANES
    return pl.pallas_call(
        functools.partial(_dispatch_kernel, tq=tq, tm=tm),
        grid_spec=pltpu.PrefetchScalarGridSpec(
            num_scalar_prefetch=1,
            grid=(h2.shape[0] // (tq * ROW_TILE),),
            in_specs=[
                pl.BlockSpec((1, TOP_K, tq), lambda i, zt: (i, 0, 0), memory_space=pltpu.SMEM),
                pl.BlockSpec((tq * ROW_TILE, d), lambda i, zt: (i, 0)),
            ],
            out_specs=pl.BlockSpec(memory_space=pl.ANY),
            scratch_shapes=[pltpu.VMEM((tm * ROW_TILE, d), h2.dtype), pltpu.SemaphoreType.DMA((TOP_K,)),
                            pltpu.SemaphoreType.DMA((1,))],
        ),
        out_shape=jax.ShapeDtypeStruct((nslot * ROW_TILE, d), h2.dtype),
        compiler_params=_params(("arbitrary",)),
        name="moe_dispatch",
    )(zero_tiles, slots, h2)


def _combine_kernel(slot_ref, x_ref, rp_ref, g_ref, ys_ref, o_ref, y_ref, sem, *, tq, normalize):
    def row_copy(r, k):
        src = pl.multiple_of(slot_ref[0, k, r] * ROW_TILE, ROW_TILE)
        dst = pl.multiple_of(r * ROW_TILE, ROW_TILE)
        return pltpu.make_async_copy(ys_ref.at[pl.ds(src, ROW_TILE), :],
                                     y_ref.at[k, pl.ds(dst, ROW_TILE), :], sem.at[k])

    def issue(r, carry):
        row_copy(r, 0).start(priority=0)
        row_copy(r, 1).start(priority=1)
        return carry

    lax.fori_loop(0, tq, issue, 0, unroll=ISSUE_UNROLL)
    for k in range(TOP_K):
        pltpu.make_async_copy(ys_ref.at[pl.ds(0, tq * ROW_TILE), :], y_ref.at[k], sem.at[k]).wait()
    rp = rp_ref[...]
    y = [jnp.concatenate([y_ref[k, pl.ds(cb, tq, stride=ROW_TILE), :] for cb in range(x_ref.shape[1] // LANES)],
                         axis=1) for k in range(TOP_K)]
    xn = x_ref[...] + rp[:, 0:1] * y[0] + rp[:, 1:2] * y[1]
    o_ref[...] = _rms(xn, g_ref[...]) if normalize else xn


def _moe_combine(x2, route_p, slots, ys, g, tq, normalize):
    t, d = x2.shape
    return pl.pallas_call(
        functools.partial(_combine_kernel, tq=tq, normalize=normalize),
        grid=(t // tq,),
        in_specs=[
            pl.BlockSpec((1, TOP_K, tq), lambda i: (i, 0, 0), memory_space=pltpu.SMEM),
            pl.BlockSpec((tq, d), lambda i: (i, 0)),
            pl.BlockSpec((tq, LANES), lambda i: (i, 0)),
            pl.BlockSpec((1, d), lambda i: (0, 0)),
            pl.BlockSpec(memory_space=pl.ANY),
        ],
        out_specs=pl.BlockSpec((tq, d), lambda i: (i, 0)),
        out_shape=jax.ShapeDtypeStruct((t, d), F32),
        scratch_shapes=[pltpu.VMEM((TOP_K, tq * ROW_TILE, LANES), F32), pltpu.SemaphoreType.DMA((TOP_K,))],
        compiler_params=_params(("arbitrary",)),
        name="moe_combine",
    )(slots, x2, route_p, g, ys)


def _final_norm_kernel(x_ref, g_ref, o_ref):
    o_ref[...] = _rms(x_ref[...], g_ref[...])


def _final_norm(x2, g, tq):
    t, d = x2.shape
    return pl.pallas_call(
        _final_norm_kernel,
        grid=(t // tq,),
        in_specs=[pl.BlockSpec((tq, d), lambda i: (i, 0)), pl.BlockSpec((1, d), lambda i: (0, 0))],
        out_specs=pl.BlockSpec((tq, d), lambda i: (i, 0)),
        out_shape=jax.ShapeDtypeStruct((t, d), F32),
        compiler_params=_params(("arbitrary",)),
        name="final_norm",
    )(x2, g)


def _rope_lane_tables(positions):
    inv_freq = ROPE_THETA ** (-jnp.arange(0, ROT_DIM, 2, dtype=F32) / ROT_DIM)
    ang = positions.astype(F32).reshape(-1, 1) * inv_freq
    cos, sin = jnp.cos(ang), jnp.sin(ang)
    t = ang.shape[0]
    rest = HEAD_DIM - ROT_DIM
    z8 = jnp.zeros((t, ROT_HALF), F32)
    cos_h = jnp.concatenate([cos, cos, jnp.ones((t, rest), F32)], axis=-1)
    lo_h = jnp.concatenate([-sin, z8, jnp.zeros((t, rest), F32)], axis=-1)
    hi_h = jnp.concatenate([z8, sin, jnp.zeros((t, rest), F32)], axis=-1)
    rep = LANES // HEAD_DIM
    return jnp.tile(cos_h, (1, rep)), jnp.tile(lo_h, (1, rep)), jnp.tile(hi_h, (1, rep))


def _moe_plan(route_i, counts, t, tm):
    cnt = counts[0, :N_EXPERTS].astype(jnp.int32)
    padded = ((cnt + tm - 1) // tm) * tm
    ends = jnp.cumsum(padded)
    offs = ends - padded
    slot = offs[route_i[:, 0:TOP_K]] + route_i[:, TOP_K:2 * TOP_K]
    ntile = (TOP_K * t) // tm + N_EXPERTS
    starts = jnp.arange(ntile, dtype=jnp.int32) * tm
    valid = (starts < ends[-1]).astype(jnp.int32)
    last = jnp.maximum(ends[-1] // tm - 1, 0)
    src = jnp.minimum(jnp.arange(ntile, dtype=jnp.int32), last)
    expert = jnp.sum(((src * tm)[:, None] >= ends[None, :]).astype(jnp.int32), axis=1)
    expert = jnp.minimum(expert, N_EXPERTS - 1)
    last_tile = jnp.where(padded > 0, ends // tm - 1, ntile - 1).astype(jnp.int32)
    tail = jnp.arange((TOP_K * t) // tm, ntile, dtype=jnp.int32)
    return slot, expert, valid, src, jnp.concatenate([last_tile, tail]), ntile * tm


def kernel(x, positions, norm1_g, w_in, b_gate, diff_lambda, diff_subln_g, pool_w, pool_scale,
           w_proj_a, w_proj_b, w_proj_c, w_out, norm2_g, ffn_w_gate, ffn_w_up, ffn_w_down,
           moe_router, moe_w_gate, moe_w_up, moe_w_down, final_norm_g):
    b, s, d = x.shape
    t = b * s
    depth = w_in.shape[0]
    tq = min(512, s)
    tm = min(512, s)
    ff = ffn_w_gate.shape[-1]
    cf = ff // 2 if ff % 512 == 0 and ff >= 1024 else ff
    x2 = x.reshape(t, d)
    cos_t, sinlo_t, sinhi_t = _rope_lane_tables(positions)
    out = None
    for l in range(depth):
        w_l = w_in[l]
        qkv = _qkv_proj(x2, norm1_g[l].reshape(1, d), w_l[:, :OFF_C].astype(BF16),
                        cos_t, sinlo_t, sinhi_t, tq)
        qa, ka, va = qkv[0:3], qkv[3:6], qkv[6:9]
        qb, kb, vb = qkv[9:12]
        oa, lse = [], []
        for g, (window, dil) in enumerate(DILATED_PATTERNS):
            assert window // dil == BLOCK
            o_g, lse_g = _dilated_attn(qa[g], ka[g], va[g], b, s, dil, 512)
            oa.append(o_g)
            lse.append(lse_g)
        lambda_init = 0.8 - 0.6 * math.exp(-0.3 * l)
        tq_b = min(1024, s)
        ob = _diff_attn(qb, kb, vb, diff_lambda[l], diff_subln_g[l].reshape(1, B_HEAD_WIDTH),
                        b, s, tq_b, min(512, tq_b // 2), min(256, tq_b // 2), lambda_init)

        dense = l % 2 == 0
        router = None
        if not dense:
            wr = jnp.zeros((d, LANES), F32).at[:, :N_EXPERTS].set(moe_router[l // 2])
            wr_hi = wr.astype(BF16)
            router = (wr_hi, (wr - wr_hi.astype(F32)).astype(BF16))
        merged = _merge(
            x2, norm1_g[l].reshape(1, d), w_l[:, OFF_C:OFF_G].astype(BF16), w_l[:, OFF_G:].astype(BF16),
            b_gate[l].reshape(1, N_BRANCH * d), oa, lse, ob, pool_w[l].astype(BF16),
            pool_scale[l].reshape(1, C_WIDTH), w_proj_a[l].astype(BF16), w_proj_b[l].astype(BF16),
            w_proj_c[l].astype(BF16), w_out[l].astype(BF16), norm2_g[l].reshape(1, d),
            router, s, tq, BF16 if dense else F32)
        if dense:
            xn, h2 = merged
            i = l // 2
            ntile = t // tm
            ident = jnp.arange(ntile, dtype=jnp.int32)
            x2 = _grouped_swiglu(h2, xn, ffn_w_gate[i:i + 1].astype(BF16), ffn_w_up[i:i + 1].astype(BF16),
                                 ffn_w_down[i:i + 1].astype(BF16), jnp.zeros((ntile,), jnp.int32),
                                 jnp.ones((ntile,), jnp.int32), ident, tm, cf)
            out = None
        else:
            xn, h2, route_i, route_p, counts = merged
            i = l // 2
            slot, expert, valid, src, zero_tiles, nslot = _moe_plan(route_i, counts, t, tm)
            tq_d, tq_c = min(2048, t), min(1024, t)
            tiled = lambda n: slot.reshape(t // n, n, TOP_K).transpose(0, 2, 1)
            xs = _moe_dispatch(h2, tiled(tq_d), zero_tiles, nslot, tq_d, tm)
            ys = _grouped_swiglu(xs, None, moe_w_gate[i].astype(BF16), moe_w_up[i].astype(BF16),
                                 moe_w_down[i].astype(BF16), expert, valid, src, tm, cf)
            last = l == depth - 1
            res = _moe_combine(xn, route_p, tiled(tq_c), ys, final_norm_g.reshape(1, d), tq_c, last)
            if last:
                out = res
            else:
                x2 = res
    if out is None:
        out = _final_norm(x2, final_norm_g.reshape(1, d), tq)
    return out.reshape(b, s, d)
```

```python
import functools
import math

import jax
import jax.numpy as jnp
from jax import lax
from jax.experimental import pallas as pl
from jax.experimental.pallas import tpu as pltpu

F32 = jnp.float32
BF16 = jnp.bfloat16

HEAD_DIM = 64
ROPE_THETA = 500000.0
ROT_DIM = HEAD_DIM // 4
ROT_HALF = ROT_DIM // 2
BLOCK = 128
EPS = 1e-6
NEG = -1e30

DILATED_PATTERNS = ((128, 1), (512, 4), (2048, 16))
A_GROUPS = len(DILATED_PATTERNS)
A_HEADS_PER_GROUP = 4
A_GROUP_WIDTH = A_HEADS_PER_GROUP * HEAD_DIM
A_WIDTH = A_GROUPS * A_GROUP_WIDTH

B_HEADS = 4
B_HEAD_WIDTH = 2 * HEAD_DIM
B_WIDTH = B_HEADS * B_HEAD_WIDTH
B_VEXT = 2 * B_HEAD_WIDTH

POOL_WINDOWS = (2, 4, 8, 16)
C_GROUP_DIM = 128
C_WIDTH = len(POOL_WINDOWS) * C_GROUP_DIM
POOL_HALO = 16

N_BRANCH = 3
N_EXPERTS = 8
TOP_K = 2

OFF_QA = 0
OFF_KA = OFF_QA + A_WIDTH
OFF_VA = OFF_KA + A_WIDTH
OFF_QB = OFF_VA + A_WIDTH
OFF_KB = OFF_QB + B_WIDTH
OFF_VB = OFF_KB + B_WIDTH
OFF_C = OFF_VB + B_WIDTH
OFF_G = OFF_C + C_WIDTH

LANES = 128
MXU_WIDTH = 256
ROW_TILE = 8
ISSUE_GROUP = LANES
V7X_VMEM_BYTES = 64 * 1024 * 1024
VMEM_LIMIT = V7X_VMEM_BYTES - 8 * 1024 * 1024

QK_SCALE = HEAD_DIM ** -0.5
LOG2_E = math.log2(math.e)


def _params(semantics):
    return pltpu.CompilerParams(dimension_semantics=semantics, vmem_limit_bytes=VMEM_LIMIT)


def _rms(x, g):
    return x * lax.rsqrt(jnp.mean(x * x, axis=-1, keepdims=True) + EPS) * g


def _dot(a, b):
    return jnp.dot(a, b, preferred_element_type=F32)


def _dot_nt(a, b):
    return lax.dot_general(a, b, (((1,), (1,)), ((), ())), preferred_element_type=F32)


def _qkv_kernel(x_ref, g_ref, w_ref, cos_ref, sinlo_ref, sinhi_ref,
                qa0, qa1, qa2, ka0, ka1, ka2, va0, va1, va2, qb, kb, vb, zs_ref):
    tq = x_ref.shape[0]
    hb = _rms(x_ref[...], g_ref[...]).astype(BF16)
    cos = cos_ref[...]
    sinlo = sinlo_ref[...]
    sinhi = sinhi_ref[...]

    def rope(z):
        return (z * cos + pltpu.roll(z, LANES - ROT_HALF, 1) * sinlo
                + pltpu.roll(z, ROT_HALF, 1) * sinhi)

    def project(off, width, out_ref, rotary, scale, dil=1):
        for c in range(0, width, MXU_WIDTH):
            zz = _dot(hb, w_ref[:, off + c:off + c + MXU_WIDTH])
            for p in range(0, MXU_WIDTH, LANES):
                z = zz[:, p:p + LANES]
                if rotary:
                    z = rope(z)
                if scale != 1.0:
                    z = z * scale
                if dil == 1:
                    out_ref[:, c + p:c + p + LANES] = z.astype(out_ref.dtype)
                else:
                    zs_ref[(c + p) // LANES] = z
        if dil > 1:
            for r in range(dil):
                for sl in range(width // LANES):
                    out_ref[:, r * width + sl * LANES:r * width + (sl + 1) * LANES] = (
                        zs_ref[sl, pl.ds(r, tq // dil, stride=dil), :].astype(out_ref.dtype))

    for g, (q_ref, k_ref, v_ref) in enumerate(((qa0, ka0, va0), (qa1, ka1, va1), (qa2, ka2, va2))):
        dil = DILATED_PATTERNS[g][1]
        project(OFF_QA + g * A_GROUP_WIDTH, A_GROUP_WIDTH, q_ref, True, QK_SCALE * LOG2_E, dil)
        project(OFF_KA + g * A_GROUP_WIDTH, A_GROUP_WIDTH, k_ref, True, 1.0, dil)
        project(OFF_VA + g * A_GROUP_WIDTH, A_GROUP_WIDTH, v_ref, False, 1.0, dil)
    project(OFF_QB, B_WIDTH, qb, True, QK_SCALE * LOG2_E)
    project(OFF_KB, B_WIDTH, kb, True, 1.0)
    lane = lax.broadcasted_iota(jnp.int32, (x_ref.shape[0], B_HEAD_WIDTH), 1)
    ones_col = jnp.where(lane == 0, 1.0, 0.0).astype(vb.dtype)
    heads_per_dot = MXU_WIDTH // B_HEAD_WIDTH
    for h0 in range(0, B_HEADS, heads_per_dot):
        zz = _dot(hb, w_ref[:, OFF_VB + h0 * B_HEAD_WIDTH:OFF_VB + (h0 + heads_per_dot) * B_HEAD_WIDTH])
        for hh in range(heads_per_dot):
            h = h0 + hh
            z = zz[:, hh * B_HEAD_WIDTH:(hh + 1) * B_HEAD_WIDTH]
            vb[:, h * B_VEXT:h * B_VEXT + B_HEAD_WIDTH] = z.astype(vb.dtype)
            vb[:, h * B_VEXT + B_HEAD_WIDTH:(h + 1) * B_VEXT] = ones_col


def _qkv_proj(x2, g, w_qkv, cos_t, sinlo_t, sinhi_t, tq):
    t, d = x2.shape
    row = lambda i: (i, 0)
    const = lambda i: (0, 0)
    dils = [dil for _ in range(3) for (_, dil) in DILATED_PATTERNS]
    shapes = [(t // dil, dil * A_GROUP_WIDTH, tq // dil) for dil in dils]
    shapes += [(t, B_WIDTH, tq), (t, B_WIDTH, tq), (t, B_HEADS * B_VEXT, tq)]
    return pl.pallas_call(
        _qkv_kernel,
        grid=(t // tq,),
        in_specs=[
            pl.BlockSpec((tq, d), row),
            pl.BlockSpec((1, d), const),
            pl.BlockSpec(w_qkv.shape, const, pipeline_mode=pl.Buffered(1)),
            pl.BlockSpec((tq, LANES), row),
            pl.BlockSpec((tq, LANES), row),
            pl.BlockSpec((tq, LANES), row),
        ],
        out_specs=[pl.BlockSpec((rows, w), row) for (_, w, rows) in shapes],
        out_shape=[jax.ShapeDtypeStruct((n, w), BF16) for (n, w, _) in shapes],
        scratch_shapes=[pltpu.VMEM((A_GROUP_WIDTH // LANES, tq, LANES), F32)],
        compiler_params=_params(("arbitrary",)),
        name="qkv_proj",
    )(x2, g, w_qkv, cos_t, sinlo_t, sinhi_t)


def _dilated_kernel(q_ref, kp_ref, kc_ref, vp_ref, vc_ref, o_ref, lse_ref, *, nsub):
    n = pl.program_id(2)
    a = lax.broadcasted_iota(jnp.int32, (BLOCK, 2 * BLOCK), 0)
    j = lax.broadcasted_iota(jnp.int32, (BLOCK, 2 * BLOCK), 1)
    band = (j >= a) & (j <= a + BLOCK)
    band_first = band & ((j >= BLOCK) | (n > 0))
    low_k = lax.broadcasted_iota(jnp.int32, (2 * BLOCK, LANES), 1) < HEAD_DIM
    low_o = lax.broadcasted_iota(jnp.int32, (BLOCK, LANES), 1) < HEAD_DIM
    ones_slab = jnp.ones((2 * BLOCK, LANES), BF16)
    for sb in range(nsub):
        rows = slice(sb * BLOCK, (sb + 1) * BLOCK)
        if sb == 0:
            kcat = jnp.concatenate([kp_ref[...], kc_ref[rows, :]], axis=0)
            vcat = jnp.concatenate([vp_ref[...], vc_ref[rows, :]], axis=0)
            mask = band_first
        else:
            kcat = kc_ref[(sb - 1) * BLOCK:(sb + 1) * BLOCK, :]
            vcat = vc_ref[(sb - 1) * BLOCK:(sb + 1) * BLOCK, :]
            mask = band
        q = q_ref[rows, :]
        for hp in range(A_HEADS_PER_GROUP // 2):
            pair = slice(hp * LANES, (hp + 1) * LANES)
            q_pair = q[:, pair]
            k_pair = kcat[:, pair]
            v_ext = jnp.concatenate([vcat[:, pair], ones_slab], axis=1)
            o_half, lse_half = [], []
            for half in range(2):
                k_h = jnp.where(low_k if half == 0 else ~low_k, k_pair, jnp.zeros_like(k_pair))
                s = jnp.where(mask, _dot_nt(q_pair, k_h), NEG)
                m = jnp.max(jnp.maximum(s[:, :LANES], s[:, LANES:]), axis=-1, keepdims=True)
                p = jnp.concatenate([jnp.exp2(s[:, :LANES] - m), jnp.exp2(s[:, LANES:] - m)], axis=1)
                pv = _dot(p.astype(BF16), v_ext)
                l = pv[:, LANES:]
                o_half.append(pv[:, :LANES] / l)
                lse_half.append(m + jnp.log2(l))
            o_ref[rows, pair] = jnp.where(low_o, o_half[0], o_half[1]).astype(o_ref.dtype)
            lse_ref[rows, pair] = jnp.where(low_o, lse_half[0], lse_half[1])


def _dilated_attn(q, k, v, b, s, dil, qrows):
    w = A_GROUP_WIDTH
    l = s // dil
    qrows = min(qrows, l)
    nsub = qrows // BLOCK
    view = lambda t: t.reshape(b, l, dil * w)
    cur = lambda bi, r, n: (bi, n, r)
    prev = lambda bi, r, n: (bi, jnp.maximum(n * nsub - 1, 0), r)
    o, lse = pl.pallas_call(
        functools.partial(_dilated_kernel, nsub=nsub),
        grid=(b, dil, l // qrows),
        in_specs=[
            pl.BlockSpec((None, qrows, w), cur),
            pl.BlockSpec((None, BLOCK, w), prev),
            pl.BlockSpec((None, qrows, w), cur),
            pl.BlockSpec((None, BLOCK, w), prev),
            pl.BlockSpec((None, qrows, w), cur),
        ],
        out_specs=[pl.BlockSpec((None, qrows, w), cur), pl.BlockSpec((None, qrows, w), cur)],
        out_shape=[jax.ShapeDtypeStruct((b, l, dil * w), BF16),
                   jax.ShapeDtypeStruct((b, l, dil * w), F32)],
        compiler_params=_params(("arbitrary", "arbitrary", "arbitrary")),
        name=f"dilated_attn_d{dil}",
    )(view(q), view(k), view(k), view(v), view(v))
    return o.reshape(b * l, dil * w), lse.reshape(b * l, dil * w)


def _diff_kernel(lam_ref, g_ref, q_ref, k_ref, v_ref, o_ref, *scratch, tq, tk, rows, lambda_init):
    i = pl.program_id(2)
    nd = tq // tk
    q = q_ref[...]
    streams = [(mp, rb) for rb in range(tq // rows) for mp in range(2)]
    acc_all, m_all = scratch[0], scratch[1]
    acc_all[...] = jnp.zeros_like(acc_all)
    m_all[...] = jnp.full_like(m_all, NEG)
    acc_refs = {(mp, rb): acc_all.at[mp, rb] for mp, rb in streams}
    m_refs = {(mp, rb): m_all.at[mp, rb] for mp, rb in streams}
    sa_ref = {(mp, rb): scratch[2].at[mp, rb] for mp, rb in streams}
    sb_ref = {(mp, rb): scratch[3].at[mp, rb] for mp, rb in streams}

    def visible(rb, diag):
        if diag is None:
            return tk
        return max(0, min(tk, (rb + 1) * rows - diag * tk))

    def scores(jc, dst, diag=None):
        k = k_ref[pl.ds(pl.multiple_of(jc * tk, tk), tk), :]
        for mp, rb in streams:
            if visible(rb, diag) == 0:
                continue
            cols = slice(mp * HEAD_DIM, (mp + 1) * HEAD_DIM)
            rs = slice(rb * rows, (rb + 1) * rows)
            dst[mp, rb][...] = _dot_nt(q[rs, cols], k[:, cols])

    def consume(jc, src, diag=None):
        start = pl.multiple_of(jc * tk, tk)
        for mp, rb in streams:
            vis = visible(rb, diag)
            if vis == 0:
                continue
            acc_ref, m_ref = acc_refs[mp, rb], m_refs[mp, rb]
            s = src[mp, rb][:, :vis]
            if diag is not None and diag * tk + vis > rb * rows + 1:
                r = lax.broadcasted_iota(jnp.int32, (rows, vis), 0) + rb * rows
                c = lax.broadcasted_iota(jnp.int32, (rows, vis), 1) + diag * tk
                s = jnp.where(c <= r, s, NEG)
            slabs = [s[:, n * LANES:(n + 1) * LANES] for n in range(vis // LANES)]
            smax = slabs[0]
            for sl in slabs[1:]:
                smax = jnp.maximum(smax, sl)
            m_old = m_ref[...]
            m_new = jnp.maximum(m_old, jnp.max(smax, axis=-1, keepdims=True))
            alpha = jnp.exp2(m_old - m_new)
            p = jnp.concatenate([jnp.exp2(sl - m_new) for sl in slabs], axis=1).astype(BF16)
            pv = _dot(p, v_ref[pl.ds(start, vis), :])
            acc = acc_ref[...]
            acc_ref[...] = jnp.concatenate(
                [acc[:, n * LANES:(n + 1) * LANES] * alpha for n in range(B_VEXT // LANES)], axis=1) + pv
            m_ref[...] = m_new

    scores(0, sa_ref)

    def pair(t, carry):
        jc = 2 * t
        scores(jc + 1, sb_ref)
        consume(jc, sa_ref)
        scores(jc + 2, sa_ref)
        consume(jc + 1, sb_ref)
        return carry

    lax.fori_loop(0, i * (nd // 2), pair, 0)
    first = nd * i
    bufs = (sa_ref, sb_ref)
    for dg in range(nd):
        if dg + 1 < nd:
            scores(first + dg + 1, bufs[(dg + 1) % 2], diag=dg + 1)
        consume(first + dg, bufs[dg % 2], diag=dg)

    lp = lam_ref[...]
    lam = (jnp.exp(jnp.sum(lp[0:1] * lp[1:2], axis=-1, keepdims=True))
           - jnp.exp(jnp.sum(lp[2:3] * lp[3:4], axis=-1, keepdims=True)) + lambda_init)
    for rb in range(tq // rows):
        acc1 = acc_refs[0, rb][...]
        acc2 = acc_refs[1, rb][...]
        o1 = acc1[:, :B_HEAD_WIDTH] / acc1[:, B_HEAD_WIDTH:B_HEAD_WIDTH + 1]
        o2 = acc2[:, :B_HEAD_WIDTH] / acc2[:, B_HEAD_WIDTH:B_HEAD_WIDTH + 1]
        o = o1 - lam * o2
        o_ref[rb * rows:(rb + 1) * rows, :] = (_rms(o, g_ref[...]) * (1.0 - lambda_init)).astype(o_ref.dtype)


def _diff_attn(qb, kb, vb, lam_p, subln_g, b, s, tq, tk, rows, lambda_init):
    hw = B_HEAD_WIDTH
    q3 = qb.reshape(b, s, B_WIDTH)
    k3 = kb.reshape(b, s, B_WIDTH)
    v3 = vb.reshape(b, s, B_HEADS * B_VEXT)
    assert tq % (2 * tk) == 0 and tk % rows == 0
    scratch = ([pltpu.VMEM((2, tq // rows, rows, B_VEXT), F32), pltpu.VMEM((2, tq // rows, rows, LANES), F32)]
               + [pltpu.VMEM((2, tq // rows, rows, tk), F32)] * 2)
    out = pl.pallas_call(
        functools.partial(_diff_kernel, tq=tq, tk=tk, rows=rows, lambda_init=lambda_init),
        grid=(b, B_HEADS, s // tq),
        in_specs=[
            pl.BlockSpec(lam_p.shape, lambda bi, h, i: (0, 0)),
            pl.BlockSpec((1, hw), lambda bi, h, i: (0, 0)),
            pl.BlockSpec((None, tq, hw), lambda bi, h, i: (bi, i, h)),
            pl.BlockSpec((None, s, hw), lambda bi, h, i: (bi, 0, h)),
            pl.BlockSpec((None, s, B_VEXT), lambda bi, h, i: (bi, 0, h)),
        ],
        out_specs=pl.BlockSpec((None, tq, hw), lambda bi, h, i: (bi, i, h)),
        out_shape=jax.ShapeDtypeStruct((b, s, B_WIDTH), BF16),
        scratch_shapes=scratch,
        compiler_params=_params(("arbitrary", "arbitrary", "arbitrary")),
        name="diff_attn",
    )(lam_p, subln_g, q3, k3, v3)
    return out.reshape(b * s, B_WIDTH)


def _merge_kernel(*refs, tq, rows, tiles_per_seq, route):
    (x_ref, g1_ref, wc_ref, wg_ref, bg_ref, oa0, oa1, oa2, ls0, ls1, ls2, ob_ref,
     pw_ref, ps_ref, wpa_ref, wpb_ref, wpc_ref, wo_ref, g2_ref) = refs[:19]
    if route:
        wrh_ref, wrl_ref = refs[19:21]
        xo_ref, h2_ref, ri_ref, rp_ref, cnt_ref, zc_ref, tm_ref, carry_ref = refs[21:]
    else:
        xo_ref, h2_ref, zc_ref, tm_ref = refs[19:]
    step = pl.program_id(0)
    seq_tile = step % tiles_per_seq
    d = x_ref.shape[-1]
    dils = [dil for (_, dil) in DILATED_PATTERNS]
    slabs = A_GROUP_WIDTH // LANES

    @pl.when(seq_tile == 0)
    def _():
        zc_ref[0:POOL_HALO, :] = jnp.zeros((POOL_HALO, C_WIDTH), F32)

    if route:
        @pl.when(step == 0)
        def _():
            carry_ref[...] = jnp.zeros_like(carry_ref)

    for part in range(tq // rows):
        r0 = part * rows
        rs = slice(r0, r0 + rows)
        x = x_ref[rs, :]
        hb = _rms(x, g1_ref[...]).astype(BF16)

        zc_ref[POOL_HALO + r0:POOL_HALO + r0 + rows, :] = _dot(hb, wc_ref[...])
        pos = seq_tile * tq + r0 + lax.broadcasted_iota(jnp.int32, (rows, C_GROUP_DIM), 0)
        pooled = []
        for gi, win in enumerate(POOL_WINDOWS):
            cols = slice(gi * C_GROUP_DIM, (gi + 1) * C_GROUP_DIM)
            tok = zc_ref[POOL_HALO + r0:POOL_HALO + r0 + rows, cols]
            tot = tok
            for back in range(1, win):
                tot = tot + zc_ref[POOL_HALO + r0 - back:POOL_HALO + r0 - back + rows, cols]
            cnt = jnp.minimum(pos + 1, win).astype(F32)
            dmean = tot / cnt - tok
            pooled.append(_dot(dmean.astype(BF16), pw_ref[gi]))
        out_c = (jnp.concatenate(pooled, axis=-1) * ps_ref[...]).astype(BF16)

        def token_major(src_ref, slot, dil):
            if dil == 1:
                return src_ref[rs, :].astype(F32)
            src_rows = slice(r0 // dil, (r0 + rows) // dil)
            for r in range(dil):
                for sl in range(slabs):
                    col = r * A_GROUP_WIDTH + sl * LANES
                    tm_ref[slot * slabs + sl, pl.ds(r0 + r, rows // dil, stride=dil), :] = (
                        src_ref[src_rows, col:col + LANES].astype(F32))
            return jnp.concatenate([tm_ref[slot * slabs + sl, rs, :] for sl in range(slabs)], axis=1)

        l0, l1, l2 = (token_major(ref, gi, dil) for gi, (ref, dil) in enumerate(zip((ls0, ls1, ls2), dils)))
        o0, o1, o2 = (token_major(ref, A_GROUPS + gi, dil)
                      for gi, (ref, dil) in enumerate(zip((oa0, oa1, oa2), dils)))
        lm = jnp.maximum(jnp.maximum(l0, l1), l2)
        e0, e1, e2 = jnp.exp2(l0 - lm), jnp.exp2(l1 - lm), jnp.exp2(l2 - lm)
        out_a = ((e0 * o0 + e1 * o1 + e2 * o2) / (e0 + e1 + e2)).astype(BF16)

        branches = (_dot(out_a, wpa_ref[...]), _dot(ob_ref[rs, :], wpb_ref[...]), _dot(out_c, wpc_ref[...]))
        mixed = None
        for bi, proj in enumerate(branches):
            zg = _dot(hb, wg_ref[:, bi * d:(bi + 1) * d]) + bg_ref[:, bi * d:(bi + 1) * d]
            term = jax.nn.sigmoid(zg) * proj
            mixed = term if mixed is None else mixed + term
        xn = x + _dot(mixed.astype(BF16), wo_ref[...])
        xo_ref[rs, :] = xn
        h2 = _rms(xn, g2_ref[...])
        if route:
            for cb in range(d // LANES):
                h2_ref[pl.ds(r0 * ROW_TILE + cb, rows, stride=ROW_TILE), :] = h2[:, cb * LANES:(cb + 1) * LANES]
        else:
            h2_ref[rs, :] = h2.astype(h2_ref.dtype)

        if route:
            h_hi = h2.astype(BF16)
            h_lo = (h2 - h_hi.astype(F32)).astype(BF16)
            logits = _dot(h_hi, wrh_ref[...]) + (_dot(h_lo, wrh_ref[...]) + _dot(h_hi, wrl_ref[...]))
            lane = lax.broadcasted_iota(jnp.int32, (rows, LANES), 1)
            lane_f = lane.astype(F32)
            logits = jnp.where(lane < N_EXPERTS, logits, -jnp.inf)
            m1 = jnp.max(logits, axis=-1, keepdims=True)
            i1 = jnp.min(jnp.where(logits == m1, lane_f, float(LANES)), axis=-1, keepdims=True)
            rest = jnp.where(lane_f == i1, -jnp.inf, logits)
            m2 = jnp.max(rest, axis=-1, keepdims=True)
            i2 = jnp.min(jnp.where(rest == m2, lane_f, float(LANES)), axis=-1, keepdims=True)
            e21 = jnp.exp(m2 - m1)
            p1 = 1.0 / (1.0 + e21)
            p2 = e21 * p1
            sel1 = lane_f == i1
            sel2 = lane_f == i2
            chosen = jnp.where(sel1 | sel2, 1.0, 0.0)

            r = lax.broadcasted_iota(jnp.int32, (rows, rows), 0)
            c = lax.broadcasted_iota(jnp.int32, (rows, rows), 1)
            before = _dot(jnp.where(c < r, 1.0, 0.0).astype(BF16), chosen.astype(BF16)) + carry_ref[...]
            rank1 = jnp.sum(jnp.where(sel1, before, 0.0), axis=-1, keepdims=True)
            rank2 = jnp.sum(jnp.where(sel2, before, 0.0), axis=-1, keepdims=True)
            carry_ref[...] = carry_ref[...] + jnp.sum(chosen, axis=0, keepdims=True)
            packed = jnp.where(lane == 0, i1, jnp.where(lane == 1, i2,
                               jnp.where(lane == 2, rank1, jnp.where(lane == 3, rank2, 0.0))))
            ri_ref[rs, :] = packed.astype(jnp.int32)
            rp_ref[rs, :] = jnp.where(lane == 0, p1, jnp.where(lane == 1, p2, 0.0))

    zc_ref[0:POOL_HALO, :] = zc_ref[tq:tq + POOL_HALO, :]
    if route:
        cnt_ref[...] = jnp.broadcast_to(carry_ref[...], cnt_ref.shape)


def _merge(x2, g1, w_c, w_g, b_g, oa, lse, ob, pool_w, pool_scale, wpa, wpb, wpc, wo, g2,
           router, s, tq, h2_dtype):
    t, d = x2.shape
    route = router is not None
    row = lambda i: (i, 0)
    const = lambda i: (0, 0)
    resident = lambda a: pl.BlockSpec(a.shape, lambda i: (0,) * a.ndim, pipeline_mode=pl.Buffered(1))
    args = [x2, g1, w_c, w_g, b_g, *oa, *lse, ob, pool_w, pool_scale, wpa, wpb, wpc, wo, g2]
    in_specs = [pl.BlockSpec((tq, d), row), pl.BlockSpec((1, d), const), resident(w_c), resident(w_g),
                pl.BlockSpec(b_g.shape, const)]
    in_specs += [pl.BlockSpec((tq // dil, dil * A_GROUP_WIDTH), row) for (_, dil) in DILATED_PATTERNS] * 2
    in_specs += [pl.BlockSpec((tq, B_WIDTH), row), resident(pool_w), pl.BlockSpec(pool_scale.shape, const),
                 resident(wpa), resident(wpb), resident(wpc), resident(wo), pl.BlockSpec((1, d), const)]
    if route:
        assert d == ROW_TILE * LANES, "a token row must fill exactly one (8,128) f32 tile"
        h2_spec = pl.BlockSpec((tq * ROW_TILE, LANES), row)
        h2_shape = jax.ShapeDtypeStruct((t * ROW_TILE, LANES), F32)
    else:
        h2_spec = pl.BlockSpec((tq, d), row)
        h2_shape = jax.ShapeDtypeStruct((t, d), h2_dtype)
    out_specs = [pl.BlockSpec((tq, d), row), h2_spec]
    out_shape = [jax.ShapeDtypeStruct((t, d), F32), h2_shape]
    scratch = [pltpu.VMEM((POOL_HALO + tq, C_WIDTH), F32), pltpu.VMEM((2 * A_GROUPS * (A_GROUP_WIDTH // LANES), tq, LANES), F32)]
    if route:
        args += list(router)
        in_specs += [resident(router[0]), resident(router[1])]
        out_specs += [pl.BlockSpec((tq, LANES), row), pl.BlockSpec((tq, LANES), row),
                      pl.BlockSpec((8, LANES), const)]
        out_shape += [jax.ShapeDtypeStruct((t, LANES), jnp.int32), jax.ShapeDtypeStruct((t, LANES), F32),
                      jax.ShapeDtypeStruct((8, LANES), F32)]
        scratch += [pltpu.VMEM((1, LANES), F32)]
    return pl.pallas_call(
        functools.partial(_merge_kernel, tq=tq, rows=min(256, tq), tiles_per_seq=s // tq, route=route),
        grid=(t // tq,),
        in_specs=in_specs,
        out_specs=out_specs,
        out_shape=out_shape,
        scratch_shapes=scratch,
        compiler_params=_params(("arbitrary",)),
        name="merge_route" if route else "merge",
    )(*args)


def _swiglu_kernel(te_ref, tv_ref, ts_ref, *refs, residual, nsteps):
    if residual:
        x_ref, res_ref, wg_ref, wu_ref, wd_ref, o_ref, acc_ref = refs
    else:
        x_ref, wg_ref, wu_ref, wd_ref, o_ref, acc_ref = refs
    j = pl.program_id(0)
    c = pl.program_id(1)
    tm, d = acc_ref.shape
    nchunk = d // LANES

    @pl.when((tv_ref[j] == 0) & (c == 0))
    def _():
        o_ref[...] = jnp.zeros_like(o_ref)

    def step(first, last):
        if residual:
            xb = x_ref[...].astype(BF16)
        else:
            xb = jnp.concatenate([x_ref[pl.ds(cb, tm, stride=ROW_TILE), :] for cb in range(nchunk)],
                                 axis=1).astype(BF16)
        gate = _dot(xb, wg_ref[...])
        up = _dot(xb, wu_ref[...])
        mid = (gate * jax.nn.sigmoid(gate) * up).astype(BF16)
        part = _dot(mid, wd_ref[...])
        if not first:
            part = acc_ref[...] + part
        if not last:
            acc_ref[...] = part
        elif residual:
            o_ref[...] = res_ref[...] + part
        else:
            for cb in range(nchunk):
                o_ref[pl.ds(cb, tm, stride=ROW_TILE), :] = part[:, cb * LANES:(cb + 1) * LANES]

    valid = tv_ref[j] > 0
    if nsteps == 1:
        pl.when(valid)(lambda: step(True, True))
    else:
        pl.when(valid & (c == 0))(lambda: step(True, False))
        pl.when(valid & (c == nsteps - 1))(lambda: step(False, True))
        if nsteps > 2:
            pl.when(valid & (c > 0) & (c < nsteps - 1))(lambda: step(False, False))


def _grouped_swiglu(xs, res, wg, wu, wd, tile_expert, tile_valid, tile_src, tm, cf):
    d = wg.shape[1]
    ff = wg.shape[-1]
    residual = res is not None
    xmap = lambda j, c, te, tv, ts: (ts[j], 0)
    if residual:
        n = xs.shape[0]
        row_block = (tm, d)
        in_specs = [pl.BlockSpec(row_block, xmap), pl.BlockSpec(row_block, xmap)]
        args = [xs, res]
    else:
        n = xs.shape[0] // ROW_TILE
        row_block = (tm * ROW_TILE, LANES)
        in_specs = [pl.BlockSpec(row_block, xmap)]
        args = [xs]
    in_specs += [
        pl.BlockSpec((None, d, cf), lambda j, c, te, tv, ts: (te[j], 0, c * tv[j])),
        pl.BlockSpec((None, d, cf), lambda j, c, te, tv, ts: (te[j], 0, c * tv[j])),
        pl.BlockSpec((None, cf, d), lambda j, c, te, tv, ts: (te[j], c * tv[j], 0)),
    ]
    args += [wg, wu, wd]
    return pl.pallas_call(
        functools.partial(_swiglu_kernel, residual=residual, nsteps=ff // cf),
        grid_spec=pltpu.PrefetchScalarGridSpec(
            num_scalar_prefetch=3,
            grid=(n // tm, ff // cf),
            in_specs=in_specs,
            out_specs=pl.BlockSpec(row_block, lambda j, c, te, tv, ts: (j, 0)),
            scratch_shapes=[pltpu.VMEM((tm, d), F32)],
        ),
        out_shape=jax.ShapeDtypeStruct(xs.shape, F32),
        compiler_params=_params(("arbitrary", "arbitrary")),
        name="grouped_swiglu_res" if residual else "grouped_swiglu",
    )(tile_expert, tile_valid, tile_src, *args)


def _dispatch_kernel(zt_ref, slot_ref, h_ref, xs_ref, zero_ref, sem, zsem, *, tq, tm):
    @pl.when(pl.program_id(0) == 0)
    def _():
        zero_ref[...] = jnp.zeros_like(zero_ref)
        for z in range(zt_ref.shape[0]):
            start = pl.multiple_of(zt_ref[z] * (tm * ROW_TILE), tm * ROW_TILE)
            fill = pltpu.make_async_copy(zero_ref, xs_ref.at[pl.ds(start, tm * ROW_TILE), :], zsem.at[0])
            fill.start()
            fill.wait()

    def row_copy(g, u, k):
        src = pl.multiple_of(g * (ISSUE_GROUP * ROW_TILE), ISSUE_GROUP * ROW_TILE) + u * ROW_TILE
        dst = pl.multiple_of(slot_ref[0, k, g, u] * ROW_TILE, ROW_TILE)
        return pltpu.make_async_copy(h_ref.at[pl.ds(src, ROW_TILE), :],
                                     xs_ref.at[pl.ds(dst, ROW_TILE), :], sem.at[k])

    def issue(g, carry):
        for u in range(ISSUE_GROUP):
            row_copy(g, u, 0).start(priority=0)
            row_copy(g, u, 1).start(priority=1)
        return carry

    lax.fori_loop(0, tq // ISSUE_GROUP, issue, 0)
    for k in range(TOP_K):
        pltpu.make_async_copy(h_ref, xs_ref.at[pl.ds(0, tq * ROW_TILE), :], sem.at[k]).wait()


def _moe_dispatch(h2, slots, zero_tiles, nslot, tq, tm):
    d = LANES
    return pl.pallas_call(
        functools.partial(_dispatch_kernel, tq=tq, tm=tm),
        grid_spec=pltpu.PrefetchScalarGridSpec(
            num_scalar_prefetch=1,
            grid=(h2.shape[0] // (tq * ROW_TILE),),
            in_specs=[
                pl.BlockSpec((1, TOP_K, tq // ISSUE_GROUP, ISSUE_GROUP), lambda i, zt: (i, 0, 0, 0),
                             memory_space=pltpu.SMEM),
                pl.BlockSpec((tq * ROW_TILE, d), lambda i, zt: (i, 0)),
            ],
            out_specs=pl.BlockSpec(memory_space=pl.ANY),
            scratch_shapes=[pltpu.VMEM((tm * ROW_TILE, d), h2.dtype), pltpu.SemaphoreType.DMA((TOP_K,)),
                            pltpu.SemaphoreType.DMA((1,))],
        ),
        out_shape=jax.ShapeDtypeStruct((nslot * ROW_TILE, d), h2.dtype),
        compiler_params=_params(("arbitrary",)),
        name="moe_dispatch",
    )(zero_tiles, slots, h2)


def _combine_kernel(slot_ref, x_ref, rp_ref, g_ref, ys_ref, o_ref, y_ref, sem, *, tq, normalize):
    def row_copy(g, u, k):
        src = pl.multiple_of(slot_ref[0, k, g, u] * ROW_TILE, ROW_TILE)
        dst = pl.multiple_of(g * (ISSUE_GROUP * ROW_TILE), ISSUE_GROUP * ROW_TILE) + u * ROW_TILE
        return pltpu.make_async_copy(ys_ref.at[pl.ds(src, ROW_TILE), :],
                                     y_ref.at[k, pl.ds(dst, ROW_TILE), :], sem.at[k])

    def issue(g, carry):
        for u in range(ISSUE_GROUP):
            row_copy(g, u, 0).start(priority=0)
            row_copy(g, u, 1).start(priority=1)
        return carry

    lax.fori_loop(0, tq // ISSUE_GROUP, issue, 0)
    for k in range(TOP_K):
        pltpu.make_async_copy(ys_ref.at[pl.ds(0, tq * ROW_TILE), :], y_ref.at[k], sem.at[k]).wait()
    rp = rp_ref[...]
    y = [jnp.concatenate([y_ref[k, pl.ds(cb, tq, stride=ROW_TILE), :] for cb in range(x_ref.shape[1] // LANES)],
                         axis=1) for k in range(TOP_K)]
    xn = x_ref[...] + rp[:, 0:1] * y[0] + rp[:, 1:2] * y[1]
    o_ref[...] = _rms(xn, g_ref[...]) if normalize else xn


def _moe_combine(x2, route_p, slots, ys, g, tq, normalize):
    t, d = x2.shape
    return pl.pallas_call(
        functools.partial(_combine_kernel, tq=tq, normalize=normalize),
        grid=(t // tq,),
        in_specs=[
            pl.BlockSpec((1, TOP_K, tq // ISSUE_GROUP, ISSUE_GROUP), lambda i: (i, 0, 0, 0),
                         memory_space=pltpu.SMEM),
            pl.BlockSpec((tq, d), lambda i: (i, 0)),
            pl.BlockSpec((tq, LANES), lambda i: (i, 0)),
            pl.BlockSpec((1, d), lambda i: (0, 0)),
            pl.BlockSpec(memory_space=pl.ANY),
        ],
        out_specs=pl.BlockSpec((tq, d), lambda i: (i, 0)),
        out_shape=jax.ShapeDtypeStruct((t, d), F32),
        scratch_shapes=[pltpu.VMEM((TOP_K, tq * ROW_TILE, LANES), F32), pltpu.SemaphoreType.DMA((TOP_K,))],
        compiler_params=_params(("arbitrary",)),
        name="moe_combine",
    )(slots, x2, route_p, g, ys)


def _final_norm_kernel(x_ref, g_ref, o_ref):
    o_ref[...] = _rms(x_ref[...], g_ref[...])


def _final_norm(x2, g, tq):
    t, d = x2.shape
    return pl.pallas_call(
        _final_norm_kernel,
        grid=(t // tq,),
        in_specs=[pl.BlockSpec((tq, d), lambda i: (i, 0)), pl.BlockSpec((1, d), lambda i: (0, 0))],
        out_specs=pl.BlockSpec((tq, d), lambda i: (i, 0)),
        out_shape=jax.ShapeDtypeStruct((t, d), F32),
        compiler_params=_params(("arbitrary",)),
        name="final_norm",
    )(x2, g)


def _rope_lane_tables(positions):
    inv_freq = ROPE_THETA ** (-jnp.arange(0, ROT_DIM, 2, dtype=F32) / ROT_DIM)
    ang = positions.astype(F32).reshape(-1, 1) * inv_freq
    cos, sin = jnp.cos(ang), jnp.sin(ang)
    t = ang.shape[0]
    rest = HEAD_DIM - ROT_DIM
    z8 = jnp.zeros((t, ROT_HALF), F32)
    cos_h = jnp.concatenate([cos, cos, jnp.ones((t, rest), F32)], axis=-1)
    lo_h = jnp.concatenate([-sin, z8, jnp.zeros((t, rest), F32)], axis=-1)
    hi_h = jnp.concatenate([z8, sin, jnp.zeros((t, rest), F32)], axis=-1)
    rep = LANES // HEAD_DIM
    return jnp.tile(cos_h, (1, rep)), jnp.tile(lo_h, (1, rep)), jnp.tile(hi_h, (1, rep))


def _moe_plan(route_i, counts, t, tm):
    cnt = counts[0, :N_EXPERTS].astype(jnp.int32)
    padded = ((cnt + tm - 1) // tm) * tm
    ends = jnp.cumsum(padded)
    offs = ends - padded
    slot = offs[route_i[:, 0:TOP_K]] + route_i[:, TOP_K:2 * TOP_K]
    ntile = (TOP_K * t) // tm + N_EXPERTS
    starts = jnp.arange(ntile, dtype=jnp.int32) * tm
    valid = (starts < ends[-1]).astype(jnp.int32)
    last = jnp.maximum(ends[-1] // tm - 1, 0)
    src = jnp.minimum(jnp.arange(ntile, dtype=jnp.int32), last)
    expert = jnp.sum(((src * tm)[:, None] >= ends[None, :]).astype(jnp.int32), axis=1)
    expert = jnp.minimum(expert, N_EXPERTS - 1)
    last_tile = jnp.where(padded > 0, ends // tm - 1, ntile - 1).astype(jnp.int32)
    tail = jnp.arange((TOP_K * t) // tm, ntile, dtype=jnp.int32)
    return slot, expert, valid, src, jnp.concatenate([last_tile, tail]), ntile * tm


def kernel(x, positions, norm1_g, w_in, b_gate, diff_lambda, diff_subln_g, pool_w, pool_scale,
           w_proj_a, w_proj_b, w_proj_c, w_out, norm2_g, ffn_w_gate, ffn_w_up, ffn_w_down,
           moe_router, moe_w_gate, moe_w_up, moe_w_down, final_norm_g):
    b, s, d = x.shape
    t = b * s
    depth = w_in.shape[0]
    tq = min(512, s)
    tm = min(512, s)
    ff = ffn_w_gate.shape[-1]
    cf = ff // 2 if ff % 512 == 0 and ff >= 1024 else ff
    x2 = x.reshape(t, d)
    cos_t, sinlo_t, sinhi_t = _rope_lane_tables(positions)
    out = None
    for l in range(depth):
        w_l = w_in[l]
        qkv = _qkv_proj(x2, norm1_g[l].reshape(1, d), w_l[:, :OFF_C].astype(BF16),
                        cos_t, sinlo_t, sinhi_t, tq)
        qa, ka, va = qkv[0:3], qkv[3:6], qkv[6:9]
        qb, kb, vb = qkv[9:12]
        oa, lse = [], []
        for g, (window, dil) in enumerate(DILATED_PATTERNS):
            assert window // dil == BLOCK
            o_g, lse_g = _dilated_attn(qa[g], ka[g], va[g], b, s, dil, 512)
            oa.append(o_g)
            lse.append(lse_g)
        lambda_init = 0.8 - 0.6 * math.exp(-0.3 * l)
        tq_b = min(1024, s)
        ob = _diff_attn(qb, kb, vb, diff_lambda[l], diff_subln_g[l].reshape(1, B_HEAD_WIDTH),
                        b, s, tq_b, min(512, tq_b // 2), min(256, tq_b // 2), lambda_init)

        dense = l % 2 == 0
        router = None
        if not dense:
            wr = jnp.zeros((d, LANES), F32).at[:, :N_EXPERTS].set(moe_router[l // 2])
            wr_hi = wr.astype(BF16)
            router = (wr_hi, (wr - wr_hi.astype(F32)).astype(BF16))
        merged = _merge(
            x2, norm1_g[l].reshape(1, d), w_l[:, OFF_C:OFF_G].astype(BF16), w_l[:, OFF_G:].astype(BF16),
            b_gate[l].reshape(1, N_BRANCH * d), oa, lse, ob, pool_w[l].astype(BF16),
            pool_scale[l].reshape(1, C_WIDTH), w_proj_a[l].astype(BF16), w_proj_b[l].astype(BF16),
            w_proj_c[l].astype(BF16), w_out[l].astype(BF16), norm2_g[l].reshape(1, d),
            router, s, tq, BF16 if dense else F32)
        if dense:
            xn, h2 = merged
            i = l // 2
            ntile = t // tm
            ident = jnp.arange(ntile, dtype=jnp.int32)
            x2 = _grouped_swiglu(h2, xn, ffn_w_gate[i:i + 1].astype(BF16), ffn_w_up[i:i + 1].astype(BF16),
                                 ffn_w_down[i:i + 1].astype(BF16), jnp.zeros((ntile,), jnp.int32),
                                 jnp.ones((ntile,), jnp.int32), ident, tm, cf)
            out = None
        else:
            xn, h2, route_i, route_p, counts = merged
            i = l // 2
            slot, expert, valid, src, zero_tiles, nslot = _moe_plan(route_i, counts, t, tm)
            tq_d, tq_c = min(2048, t), min(1024, t)
            tiled = lambda n: slot.reshape(t // n, n, TOP_K).transpose(0, 2, 1).reshape(
                t // n, TOP_K, n // ISSUE_GROUP, ISSUE_GROUP)
            xs = _moe_dispatch(h2, tiled(tq_d), zero_tiles, nslot, tq_d, tm)
            ys = _grouped_swiglu(xs, None, moe_w_gate[i].astype(BF16), moe_w_up[i].astype(BF16),
                                 moe_w_down[i].astype(BF16), expert, valid, src, tm, cf)
            last = l == depth - 1
            res = _moe_combine(xn, route_p, tiled(tq_c), ys, final_norm_g.reshape(1, d), tq_c, last)
            if last:
                out = res
            else:
                x2 = res
    if out is None:
        out = _final_norm(x2, final_norm_g.reshape(1, d), tq)
    return out.reshape(b, s, d)
```

```python
import functools
import math

import jax
import jax.numpy as jnp
from jax import lax
from jax.experimental import pallas as pl
from jax.experimental.pallas import tpu as pltpu

F32 = jnp.float32
BF16 = jnp.bfloat16

HEAD_DIM = 64
ROPE_THETA = 500000.0
ROT_DIM = HEAD_DIM // 4
ROT_HALF = ROT_DIM // 2
BLOCK = 128
EPS = 1e-6
NEG = -1e30

DILATED_PATTERNS = ((128, 1), (512, 4), (2048, 16))
A_GROUPS = len(DILATED_PATTERNS)
A_HEADS_PER_GROUP = 4
A_GROUP_WIDTH = A_HEADS_PER_GROUP * HEAD_DIM
A_WIDTH = A_GROUPS * A_GROUP_WIDTH

B_HEADS = 4
B_HEAD_WIDTH = 2 * HEAD_DIM
B_WIDTH = B_HEADS * B_HEAD_WIDTH
B_VEXT = 2 * B_HEAD_WIDTH

POOL_WINDOWS = (2, 4, 8, 16)
C_GROUP_DIM = 128
C_WIDTH = len(POOL_WINDOWS) * C_GROUP_DIM
POOL_HALO = 16

N_BRANCH = 3
N_EXPERTS = 8
TOP_K = 2

OFF_QA = 0
OFF_KA = OFF_QA + A_WIDTH
OFF_VA = OFF_KA + A_WIDTH
OFF_QB = OFF_VA + A_WIDTH
OFF_KB = OFF_QB + B_WIDTH
OFF_VB = OFF_KB + B_WIDTH
OFF_C = OFF_VB + B_WIDTH
OFF_G = OFF_C + C_WIDTH

LANES = 128
MXU_WIDTH = 256
ROW_TILE = 8
ISSUE_GROUP = LANES
V7X_VMEM_BYTES = 64 * 1024 * 1024
VMEM_LIMIT = V7X_VMEM_BYTES - 8 * 1024 * 1024

QK_SCALE = HEAD_DIM ** -0.5
LOG2_E = math.log2(math.e)


def _params(semantics):
    return pltpu.CompilerParams(dimension_semantics=semantics, vmem_limit_bytes=VMEM_LIMIT)


def _rms(x, g):
    return x * lax.rsqrt(jnp.mean(x * x, axis=-1, keepdims=True) + EPS) * g


def _dot(a, b):
    return jnp.dot(a, b, preferred_element_type=F32)


def _dot_nt(a, b):
    return lax.dot_general(a, b, (((1,), (1,)), ((), ())), preferred_element_type=F32)


def _qkv_kernel(x_ref, g_ref, w_ref, cos_ref, sinlo_ref, sinhi_ref,
                qa0, qa1, qa2, ka0, ka1, ka2, va0, va1, va2, qb, kb, vb, zs_ref):
    tq = x_ref.shape[0]
    hb = _rms(x_ref[...], g_ref[...]).astype(BF16)
    cos = cos_ref[...]
    sinlo = sinlo_ref[...]
    sinhi = sinhi_ref[...]

    def rope(z):
        return (z * cos + pltpu.roll(z, LANES - ROT_HALF, 1) * sinlo
                + pltpu.roll(z, ROT_HALF, 1) * sinhi)

    def project(off, width, out_ref, rotary, scale, dil=1):
        for c in range(0, width, MXU_WIDTH):
            zz = _dot(hb, w_ref[:, off + c:off + c + MXU_WIDTH])
            for p in range(0, MXU_WIDTH, LANES):
                z = zz[:, p:p + LANES]
                if rotary:
                    z = rope(z)
                if scale != 1.0:
                    z = z * scale
                if dil == 1:
                    out_ref[:, c + p:c + p + LANES] = z.astype(out_ref.dtype)
                else:
                    zs_ref[(c + p) // LANES] = z
        if dil > 1:
            for r in range(dil):
                for sl in range(width // LANES):
                    out_ref[:, r * width + sl * LANES:r * width + (sl + 1) * LANES] = (
                        zs_ref[sl, pl.ds(r, tq // dil, stride=dil), :].astype(out_ref.dtype))

    for g, (q_ref, k_ref, v_ref) in enumerate(((qa0, ka0, va0), (qa1, ka1, va1), (qa2, ka2, va2))):
        dil = DILATED_PATTERNS[g][1]
        project(OFF_QA + g * A_GROUP_WIDTH, A_GROUP_WIDTH, q_ref, True, QK_SCALE * LOG2_E, dil)
        project(OFF_KA + g * A_GROUP_WIDTH, A_GROUP_WIDTH, k_ref, True, 1.0, dil)
        project(OFF_VA + g * A_GROUP_WIDTH, A_GROUP_WIDTH, v_ref, False, 1.0, dil)
    project(OFF_QB, B_WIDTH, qb, True, QK_SCALE * LOG2_E)
    project(OFF_KB, B_WIDTH, kb, True, 1.0)
    lane = lax.broadcasted_iota(jnp.int32, (x_ref.shape[0], B_HEAD_WIDTH), 1)
    ones_col = jnp.where(lane == 0, 1.0, 0.0).astype(vb.dtype)
    heads_per_dot = MXU_WIDTH // B_HEAD_WIDTH
    for h0 in range(0, B_HEADS, heads_per_dot):
        zz = _dot(hb, w_ref[:, OFF_VB + h0 * B_HEAD_WIDTH:OFF_VB + (h0 + heads_per_dot) * B_HEAD_WIDTH])
        for hh in range(heads_per_dot):
            h = h0 + hh
            z = zz[:, hh * B_HEAD_WIDTH:(hh + 1) * B_HEAD_WIDTH]
            vb[:, h * B_VEXT:h * B_VEXT + B_HEAD_WIDTH] = z.astype(vb.dtype)
            vb[:, h * B_VEXT + B_HEAD_WIDTH:(h + 1) * B_VEXT] = ones_col


def _qkv_proj(x2, g, w_qkv, cos_t, sinlo_t, sinhi_t, tq):
    t, d = x2.shape
    row = lambda i: (i, 0)
    const = lambda i: (0, 0)
    dils = [dil for _ in range(3) for (_, dil) in DILATED_PATTERNS]
    shapes = [(t // dil, dil * A_GROUP_WIDTH, tq // dil) for dil in dils]
    shapes += [(t, B_WIDTH, tq), (t, B_WIDTH, tq), (t, B_HEADS * B_VEXT, tq)]
    return pl.pallas_call(
        _qkv_kernel,
        grid=(t // tq,),
        in_specs=[
            pl.BlockSpec((tq, d), row),
            pl.BlockSpec((1, d), const),
            pl.BlockSpec(w_qkv.shape, const, pipeline_mode=pl.Buffered(1)),
            pl.BlockSpec((tq, LANES), row),
            pl.BlockSpec((tq, LANES), row),
            pl.BlockSpec((tq, LANES), row),
        ],
        out_specs=[pl.BlockSpec((rows, w), row) for (_, w, rows) in shapes],
        out_shape=[jax.ShapeDtypeStruct((n, w), BF16) for (n, w, _) in shapes],
        scratch_shapes=[pltpu.VMEM((A_GROUP_WIDTH // LANES, tq, LANES), F32)],
        compiler_params=_params(("arbitrary",)),
        name="qkv_proj",
    )(x2, g, w_qkv, cos_t, sinlo_t, sinhi_t)


def _dilated_kernel(q_ref, kp_ref, kc_ref, vp_ref, vc_ref, o_ref, lse_ref, *, nsub):
    n = pl.program_id(2)
    a = lax.broadcasted_iota(jnp.int32, (BLOCK, 2 * BLOCK), 0)
    j = lax.broadcasted_iota(jnp.int32, (BLOCK, 2 * BLOCK), 1)
    band = (j >= a) & (j <= a + BLOCK)
    band_first = band & ((j >= BLOCK) | (n > 0))
    low_k = lax.broadcasted_iota(jnp.int32, (2 * BLOCK, LANES), 1) < HEAD_DIM
    low_o = lax.broadcasted_iota(jnp.int32, (BLOCK, LANES), 1) < HEAD_DIM
    ones_slab = jnp.ones((2 * BLOCK, LANES), BF16)
    for sb in range(nsub):
        rows = slice(sb * BLOCK, (sb + 1) * BLOCK)
        if sb == 0:
            kcat = jnp.concatenate([kp_ref[...], kc_ref[rows, :]], axis=0)
            vcat = jnp.concatenate([vp_ref[...], vc_ref[rows, :]], axis=0)
            mask = band_first
        else:
            kcat = kc_ref[(sb - 1) * BLOCK:(sb + 1) * BLOCK, :]
            vcat = vc_ref[(sb - 1) * BLOCK:(sb + 1) * BLOCK, :]
            mask = band
        q = q_ref[rows, :]
        for hp in range(A_HEADS_PER_GROUP // 2):
            pair = slice(hp * LANES, (hp + 1) * LANES)
            q_pair = q[:, pair]
            k_pair = kcat[:, pair]
            v_ext = jnp.concatenate([vcat[:, pair], ones_slab], axis=1)
            o_half, lse_half = [], []
            for half in range(2):
                k_h = jnp.where(low_k if half == 0 else ~low_k, k_pair, jnp.zeros_like(k_pair))
                s = jnp.where(mask, _dot_nt(q_pair, k_h), NEG)
                m = jnp.max(jnp.maximum(s[:, :LANES], s[:, LANES:]), axis=-1, keepdims=True)
                p = jnp.concatenate([jnp.exp2(s[:, :LANES] - m), jnp.exp2(s[:, LANES:] - m)], axis=1)
                pv = _dot(p.astype(BF16), v_ext)
                l = pv[:, LANES:]
                o_half.append(pv[:, :LANES] / l)
                lse_half.append(m + jnp.log2(l))
            o_ref[rows, pair] = jnp.where(low_o, o_half[0], o_half[1]).astype(o_ref.dtype)
            lse_ref[rows, pair] = jnp.where(low_o, lse_half[0], lse_half[1])


def _dilated_attn(q, k, v, b, s, dil, qrows):
    w = A_GROUP_WIDTH
    l = s // dil
    qrows = min(qrows, l)
    nsub = qrows // BLOCK
    view = lambda t: t.reshape(b, l, dil * w)
    cur = lambda bi, r, n: (bi, n, r)
    prev = lambda bi, r, n: (bi, jnp.maximum(n * nsub - 1, 0), r)
    o, lse = pl.pallas_call(
        functools.partial(_dilated_kernel, nsub=nsub),
        grid=(b, dil, l // qrows),
        in_specs=[
            pl.BlockSpec((None, qrows, w), cur),
            pl.BlockSpec((None, BLOCK, w), prev),
            pl.BlockSpec((None, qrows, w), cur),
            pl.BlockSpec((None, BLOCK, w), prev),
            pl.BlockSpec((None, qrows, w), cur),
        ],
        out_specs=[pl.BlockSpec((None, qrows, w), cur), pl.BlockSpec((None, qrows, w), cur)],
        out_shape=[jax.ShapeDtypeStruct((b, l, dil * w), BF16),
                   jax.ShapeDtypeStruct((b, l, dil * w), F32)],
        compiler_params=_params(("arbitrary", "arbitrary", "arbitrary")),
        name=f"dilated_attn_d{dil}",
    )(view(q), view(k), view(k), view(v), view(v))
    return o.reshape(b * l, dil * w), lse.reshape(b * l, dil * w)


def _diff_kernel(lam_ref, g_ref, q_ref, k_ref, v_ref, o_ref, *scratch, tq, tk, rows, lambda_init):
    i = pl.program_id(2)
    nd = tq // tk
    q = q_ref[...]
    streams = [(mp, rb) for rb in range(tq // rows) for mp in range(2)]
    acc_all, m_all = scratch[0], scratch[1]
    acc_all[...] = jnp.zeros_like(acc_all)
    m_all[...] = jnp.full_like(m_all, NEG)
    acc_refs = {(mp, rb): acc_all.at[mp, rb] for mp, rb in streams}
    m_refs = {(mp, rb): m_all.at[mp, rb] for mp, rb in streams}
    sa_ref = {(mp, rb): scratch[2].at[mp, rb] for mp, rb in streams}
    sb_ref = {(mp, rb): scratch[3].at[mp, rb] for mp, rb in streams}

    def visible(rb, diag):
        if diag is None:
            return tk
        return max(0, min(tk, (rb + 1) * rows - diag * tk))

    def scores(jc, dst, diag=None):
        k = k_ref[pl.ds(pl.multiple_of(jc * tk, tk), tk), :]
        for mp, rb in streams:
            if visible(rb, diag) == 0:
                continue
            cols = slice(mp * HEAD_DIM, (mp + 1) * HEAD_DIM)
            rs = slice(rb * rows, (rb + 1) * rows)
            dst[mp, rb][...] = _dot_nt(q[rs, cols], k[:, cols])

    def consume(jc, src, diag=None):
        start = pl.multiple_of(jc * tk, tk)
        for mp, rb in streams:
            vis = visible(rb, diag)
            if vis == 0:
                continue
            acc_ref, m_ref = acc_refs[mp, rb], m_refs[mp, rb]
            s = src[mp, rb][:, :vis]
            if diag is not None and diag * tk + vis > rb * rows + 1:
                r = lax.broadcasted_iota(jnp.int32, (rows, vis), 0) + rb * rows
                c = lax.broadcasted_iota(jnp.int32, (rows, vis), 1) + diag * tk
                s = jnp.where(c <= r, s, NEG)
            slabs = [s[:, n * LANES:(n + 1) * LANES] for n in range(vis // LANES)]
            smax = slabs[0]
            for sl in slabs[1:]:
                smax = jnp.maximum(smax, sl)
            m_old = m_ref[...]
            m_new = jnp.maximum(m_old, jnp.max(smax, axis=-1, keepdims=True))
            alpha = jnp.exp2(m_old - m_new)
            p = jnp.concatenate([jnp.exp2(sl - m_new) for sl in slabs], axis=1).astype(BF16)
            pv = _dot(p, v_ref[pl.ds(start, vis), :])
            acc = acc_ref[...]
            acc_ref[...] = jnp.concatenate(
                [acc[:, n * LANES:(n + 1) * LANES] * alpha for n in range(B_VEXT // LANES)], axis=1) + pv
            m_ref[...] = m_new

    scores(0, sa_ref)

    def pair(t, carry):
        jc = 2 * t
        scores(jc + 1, sb_ref)
        consume(jc, sa_ref)
        scores(jc + 2, sa_ref)
        consume(jc + 1, sb_ref)
        return carry

    lax.fori_loop(0, i * (nd // 2), pair, 0)
    first = nd * i
    bufs = (sa_ref, sb_ref)
    for dg in range(nd):
        if dg + 1 < nd:
            scores(first + dg + 1, bufs[(dg + 1) % 2], diag=dg + 1)
        consume(first + dg, bufs[dg % 2], diag=dg)

    lp = lam_ref[...]
    lam = (jnp.exp(jnp.sum(lp[0:1] * lp[1:2], axis=-1, keepdims=True))
           - jnp.exp(jnp.sum(lp[2:3] * lp[3:4], axis=-1, keepdims=True)) + lambda_init)
    for rb in range(tq // rows):
        acc1 = acc_refs[0, rb][...]
        acc2 = acc_refs[1, rb][...]
        o1 = acc1[:, :B_HEAD_WIDTH] / acc1[:, B_HEAD_WIDTH:B_HEAD_WIDTH + 1]
        o2 = acc2[:, :B_HEAD_WIDTH] / acc2[:, B_HEAD_WIDTH:B_HEAD_WIDTH + 1]
        o = o1 - lam * o2
        o_ref[rb * rows:(rb + 1) * rows, :] = (_rms(o, g_ref[...]) * (1.0 - lambda_init)).astype(o_ref.dtype)


def _diff_attn(qb, kb, vb, lam_p, subln_g, b, s, tq, tk, rows, lambda_init):
    hw = B_HEAD_WIDTH
    q3 = qb.reshape(b, s, B_WIDTH)
    k3 = kb.reshape(b, s, B_WIDTH)
    v3 = vb.reshape(b, s, B_HEADS * B_VEXT)
    assert tq % (2 * tk) == 0 and tk % rows == 0
    scratch = ([pltpu.VMEM((2, tq // rows, rows, B_VEXT), F32), pltpu.VMEM((2, tq // rows, rows, LANES), F32)]
               + [pltpu.VMEM((2, tq // rows, rows, tk), F32)] * 2)
    out = pl.pallas_call(
        functools.partial(_diff_kernel, tq=tq, tk=tk, rows=rows, lambda_init=lambda_init),
        grid=(b, B_HEADS, s // tq),
        in_specs=[
            pl.BlockSpec(lam_p.shape, lambda bi, h, i: (0, 0)),
            pl.BlockSpec((1, hw), lambda bi, h, i: (0, 0)),
            pl.BlockSpec((None, tq, hw), lambda bi, h, i: (bi, i, h)),
            pl.BlockSpec((None, s, hw), lambda bi, h, i: (bi, 0, h)),
            pl.BlockSpec((None, s, B_VEXT), lambda bi, h, i: (bi, 0, h)),
        ],
        out_specs=pl.BlockSpec((None, tq, hw), lambda bi, h, i: (bi, i, h)),
        out_shape=jax.ShapeDtypeStruct((b, s, B_WIDTH), BF16),
        scratch_shapes=scratch,
        compiler_params=_params(("arbitrary", "arbitrary", "arbitrary")),
        name="diff_attn",
    )(lam_p, subln_g, q3, k3, v3)
    return out.reshape(b * s, B_WIDTH)


def _merge_kernel(*refs, tq, rows, tiles_per_seq, route):
    (x_ref, g1_ref, wc_ref, wg_ref, bg_ref, oa0, oa1, oa2, ls0, ls1, ls2, ob_ref,
     pw_ref, ps_ref, wpa_ref, wpb_ref, wpc_ref, wo_ref, g2_ref) = refs[:19]
    if route:
        wrh_ref, wrl_ref = refs[19:21]
        xo_ref, h2_ref, ri_ref, rp_ref, cnt_ref, zc_ref, tm_ref, carry_ref = refs[21:]
    else:
        xo_ref, h2_ref, zc_ref, tm_ref = refs[19:]
    step = pl.program_id(0)
    seq_tile = step % tiles_per_seq
    d = x_ref.shape[-1]
    dils = [dil for (_, dil) in DILATED_PATTERNS]
    slabs = A_GROUP_WIDTH // LANES

    @pl.when(seq_tile == 0)
    def _():
        zc_ref[0:POOL_HALO, :] = jnp.zeros((POOL_HALO, C_WIDTH), F32)

    if route:
        @pl.when(step == 0)
        def _():
            carry_ref[...] = jnp.zeros_like(carry_ref)

    for part in range(tq // rows):
        r0 = part * rows
        rs = slice(r0, r0 + rows)
        x = x_ref[rs, :]
        hb = _rms(x, g1_ref[...]).astype(BF16)

        zc_ref[POOL_HALO + r0:POOL_HALO + r0 + rows, :] = _dot(hb, wc_ref[...])
        pos = seq_tile * tq + r0 + lax.broadcasted_iota(jnp.int32, (rows, C_GROUP_DIM), 0)
        pooled = []
        for gi, win in enumerate(POOL_WINDOWS):
            cols = slice(gi * C_GROUP_DIM, (gi + 1) * C_GROUP_DIM)
            tok = zc_ref[POOL_HALO + r0:POOL_HALO + r0 + rows, cols]
            tot = tok
            for back in range(1, win):
                tot = tot + zc_ref[POOL_HALO + r0 - back:POOL_HALO + r0 - back + rows, cols]
            cnt = jnp.minimum(pos + 1, win).astype(F32)
            dmean = tot / cnt - tok
            pooled.append(_dot(dmean.astype(BF16), pw_ref[gi]))
        out_c = (jnp.concatenate(pooled, axis=-1) * ps_ref[...]).astype(BF16)

        def token_major(src_ref, slot, dil):
            if dil == 1:
                return src_ref[rs, :].astype(F32)
            src_rows = slice(r0 // dil, (r0 + rows) // dil)
            for r in range(dil):
                for sl in range(slabs):
                    col = r * A_GROUP_WIDTH + sl * LANES
                    tm_ref[slot * slabs + sl, pl.ds(r0 + r, rows // dil, stride=dil), :] = (
                        src_ref[src_rows, col:col + LANES].astype(F32))
            return jnp.concatenate([tm_ref[slot * slabs + sl, rs, :] for sl in range(slabs)], axis=1)

        l0, l1, l2 = (token_major(ref, gi, dil) for gi, (ref, dil) in enumerate(zip((ls0, ls1, ls2), dils)))
        o0, o1, o2 = (token_major(ref, A_GROUPS + gi, dil)
                      for gi, (ref, dil) in enumerate(zip((oa0, oa1, oa2), dils)))
        lm = jnp.maximum(jnp.maximum(l0, l1), l2)
        e0, e1, e2 = jnp.exp2(l0 - lm), jnp.exp2(l1 - lm), jnp.exp2(l2 - lm)
        out_a = ((e0 * o0 + e1 * o1 + e2 * o2) / (e0 + e1 + e2)).astype(BF16)

        branches = (_dot(out_a, wpa_ref[...]), _dot(ob_ref[rs, :], wpb_ref[...]), _dot(out_c, wpc_ref[...]))
        mixed = None
        for bi, proj in enumerate(branches):
            zg = _dot(hb, wg_ref[:, bi * d:(bi + 1) * d]) + bg_ref[:, bi * d:(bi + 1) * d]
            term = jax.nn.sigmoid(zg) * proj
            mixed = term if mixed is None else mixed + term
        xn = x + _dot(mixed.astype(BF16), wo_ref[...])
        xo_ref[rs, :] = xn
        h2 = _rms(xn, g2_ref[...])
        if route:
            for cb in range(d // LANES):
                h2_ref[pl.ds(r0 * ROW_TILE + cb, rows, stride=ROW_TILE), :] = h2[:, cb * LANES:(cb + 1) * LANES]
        else:
            h2_ref[rs, :] = h2.astype(h2_ref.dtype)

        if route:
            h_hi = h2.astype(BF16)
            h_lo = (h2 - h_hi.astype(F32)).astype(BF16)
            logits = _dot(h_hi, wrh_ref[...]) + (_dot(h_lo, wrh_ref[...]) + _dot(h_hi, wrl_ref[...]))
            lane = lax.broadcasted_iota(jnp.int32, (rows, LANES), 1)
            lane_f = lane.astype(F32)
            logits = jnp.where(lane < N_EXPERTS, logits, -jnp.inf)
            m1 = jnp.max(logits, axis=-1, keepdims=True)
            i1 = jnp.min(jnp.where(logits == m1, lane_f, float(LANES)), axis=-1, keepdims=True)
            rest = jnp.where(lane_f == i1, -jnp.inf, logits)
            m2 = jnp.max(rest, axis=-1, keepdims=True)
            i2 = jnp.min(jnp.where(rest == m2, lane_f, float(LANES)), axis=-1, keepdims=True)
            e21 = jnp.exp(m2 - m1)
            p1 = 1.0 / (1.0 + e21)
            p2 = e21 * p1
            sel1 = lane_f == i1
            sel2 = lane_f == i2
            chosen = jnp.where(sel1 | sel2, 1.0, 0.0)

            r = lax.broadcasted_iota(jnp.int32, (rows, rows), 0)
            c = lax.broadcasted_iota(jnp.int32, (rows, rows), 1)
            before = _dot(jnp.where(c < r, 1.0, 0.0).astype(BF16), chosen.astype(BF16)) + carry_ref[...]
            rank1 = jnp.sum(jnp.where(sel1, before, 0.0), axis=-1, keepdims=True)
            rank2 = jnp.sum(jnp.where(sel2, before, 0.0), axis=-1, keepdims=True)
            carry_ref[...] = carry_ref[...] + jnp.sum(chosen, axis=0, keepdims=True)
            packed = jnp.where(lane == 0, i1, jnp.where(lane == 1, i2,
                               jnp.where(lane == 2, rank1, jnp.where(lane == 3, rank2, 0.0))))
            ri_ref[rs, :] = packed.astype(jnp.int32)
            rp_ref[rs, :] = jnp.where(lane == 0, p1, jnp.where(lane == 1, p2, 0.0))

    zc_ref[0:POOL_HALO, :] = zc_ref[tq:tq + POOL_HALO, :]
    if route:
        cnt_ref[...] = jnp.broadcast_to(carry_ref[...], cnt_ref.shape)


def _merge(x2, g1, w_c, w_g, b_g, oa, lse, ob, pool_w, pool_scale, wpa, wpb, wpc, wo, g2,
           router, s, tq, h2_dtype):
    t, d = x2.shape
    route = router is not None
    row = lambda i: (i, 0)
    const = lambda i: (0, 0)
    resident = lambda a: pl.BlockSpec(a.shape, lambda i: (0,) * a.ndim, pipeline_mode=pl.Buffered(1))
    args = [x2, g1, w_c, w_g, b_g, *oa, *lse, ob, pool_w, pool_scale, wpa, wpb, wpc, wo, g2]
    in_specs = [pl.BlockSpec((tq, d), row), pl.BlockSpec((1, d), const), resident(w_c), resident(w_g),
                pl.BlockSpec(b_g.shape, const)]
    in_specs += [pl.BlockSpec((tq // dil, dil * A_GROUP_WIDTH), row) for (_, dil) in DILATED_PATTERNS] * 2
    in_specs += [pl.BlockSpec((tq, B_WIDTH), row), resident(pool_w), pl.BlockSpec(pool_scale.shape, const),
                 resident(wpa), resident(wpb), resident(wpc), resident(wo), pl.BlockSpec((1, d), const)]
    if route:
        assert d == ROW_TILE * LANES, "a token row must fill exactly one (8,128) f32 tile"
        h2_spec = pl.BlockSpec((tq * ROW_TILE, LANES), row)
        h2_shape = jax.ShapeDtypeStruct((t * ROW_TILE, LANES), F32)
    else:
        h2_spec = pl.BlockSpec((tq, d), row)
        h2_shape = jax.ShapeDtypeStruct((t, d), h2_dtype)
    out_specs = [pl.BlockSpec((tq, d), row), h2_spec]
    out_shape = [jax.ShapeDtypeStruct((t, d), F32), h2_shape]
    scratch = [pltpu.VMEM((POOL_HALO + tq, C_WIDTH), F32), pltpu.VMEM((2 * A_GROUPS * (A_GROUP_WIDTH // LANES), tq, LANES), F32)]
    if route:
        args += list(router)
        in_specs += [resident(router[0]), resident(router[1])]
        out_specs += [pl.BlockSpec((tq, LANES), row), pl.BlockSpec((tq, LANES), row),
                      pl.BlockSpec((8, LANES), const)]
        out_shape += [jax.ShapeDtypeStruct((t, LANES), jnp.int32), jax.ShapeDtypeStruct((t, LANES), F32),
                      jax.ShapeDtypeStruct((8, LANES), F32)]
        scratch += [pltpu.VMEM((1, LANES), F32)]
    return pl.pallas_call(
        functools.partial(_merge_kernel, tq=tq, rows=min(256, tq), tiles_per_seq=s // tq, route=route),
        grid=(t // tq,),
        in_specs=in_specs,
        out_specs=out_specs,
        out_shape=out_shape,
        scratch_shapes=scratch,
        compiler_params=_params(("arbitrary",)),
        name="merge_route" if route else "merge",
    )(*args)


def _swiglu_kernel(te_ref, tv_ref, ts_ref, *refs, residual, nsteps):
    if residual:
        x_ref, res_ref, wg_ref, wu_ref, wd_ref, o_ref, acc_ref = refs
    else:
        x_ref, wg_ref, wu_ref, wd_ref, o_ref, acc_ref = refs
    j = pl.program_id(0)
    c = pl.program_id(1)
    tm, d = acc_ref.shape
    nchunk = d // LANES

    @pl.when((tv_ref[j] == 0) & (c == 0))
    def _():
        o_ref[...] = jnp.zeros_like(o_ref)

    def step(first, last):
        if residual:
            xb = x_ref[...].astype(BF16)
        else:
            xb = jnp.concatenate([x_ref[pl.ds(cb, tm, stride=ROW_TILE), :] for cb in range(nchunk)],
                                 axis=1).astype(BF16)
        gate = _dot(xb, wg_ref[...])
        up = _dot(xb, wu_ref[...])
        mid = (gate * jax.nn.sigmoid(gate) * up).astype(BF16)
        part = _dot(mid, wd_ref[...])
        if not first:
            part = acc_ref[...] + part
        if not last:
            acc_ref[...] = part
        elif residual:
            o_ref[...] = res_ref[...] + part
        else:
            for cb in range(nchunk):
                o_ref[pl.ds(cb, tm, stride=ROW_TILE), :] = part[:, cb * LANES:(cb + 1) * LANES]

    valid = tv_ref[j] > 0
    if nsteps == 1:
        pl.when(valid)(lambda: step(True, True))
    else:
        pl.when(valid & (c == 0))(lambda: step(True, False))
        pl.when(valid & (c == nsteps - 1))(lambda: step(False, True))
        if nsteps > 2:
            pl.when(valid & (c > 0) & (c < nsteps - 1))(lambda: step(False, False))


def _grouped_swiglu(xs, res, wg, wu, wd, tile_expert, tile_valid, tile_src, tm, cf):
    d = wg.shape[1]
    ff = wg.shape[-1]
    residual = res is not None
    xmap = lambda j, c, te, tv, ts: (ts[j], 0)
    if residual:
        n = xs.shape[0]
        row_block = (tm, d)
        in_specs = [pl.BlockSpec(row_block, xmap), pl.BlockSpec(row_block, xmap)]
        args = [xs, res]
    else:
        n = xs.shape[0] // ROW_TILE
        row_block = (tm * ROW_TILE, LANES)
        in_specs = [pl.BlockSpec(row_block, xmap)]
        args = [xs]
    in_specs += [
        pl.BlockSpec((None, d, cf), lambda j, c, te, tv, ts: (te[j], 0, c * tv[j])),
        pl.BlockSpec((None, d, cf), lambda j, c, te, tv, ts: (te[j], 0, c * tv[j])),
        pl.BlockSpec((None, cf, d), lambda j, c, te, tv, ts: (te[j], c * tv[j], 0)),
    ]
    args += [wg, wu, wd]
    return pl.pallas_call(
        functools.partial(_swiglu_kernel, residual=residual, nsteps=ff // cf),
        grid_spec=pltpu.PrefetchScalarGridSpec(
            num_scalar_prefetch=3,
            grid=(n // tm, ff // cf),
            in_specs=in_specs,
            out_specs=pl.BlockSpec(row_block, lambda j, c, te, tv, ts: (j, 0)),
            scratch_shapes=[pltpu.VMEM((tm, d), F32)],
        ),
        out_shape=jax.ShapeDtypeStruct(xs.shape, F32),
        compiler_params=_params(("arbitrary", "arbitrary")),
        name="grouped_swiglu_res" if residual else "grouped_swiglu",
    )(tile_expert, tile_valid, tile_src, *args)


def _dispatch_kernel(zt_ref, slot_ref, h_ref, xs_ref, zero_ref, sem, zsem, *, tq, tm):
    @pl.when(pl.program_id(0) == 0)
    def _():
        zero_ref[...] = jnp.zeros_like(zero_ref)
        for z in range(zt_ref.shape[0]):
            start = pl.multiple_of(zt_ref[z] * (tm * ROW_TILE), tm * ROW_TILE)
            fill = pltpu.make_async_copy(zero_ref, xs_ref.at[pl.ds(start, tm * ROW_TILE), :], zsem.at[0])
            fill.start()
            fill.wait()

    def row_copy(g, u, k):
        src = pl.multiple_of(g * (ISSUE_GROUP * ROW_TILE), ISSUE_GROUP * ROW_TILE) + u * ROW_TILE
        dst = pl.multiple_of(slot_ref[0, k, g, u] * ROW_TILE, ROW_TILE)
        return pltpu.make_async_copy(h_ref.at[pl.ds(src, ROW_TILE), :],
                                     xs_ref.at[pl.ds(dst, ROW_TILE), :], sem.at[k])

    def issue(g, carry):
        for u in range(ISSUE_GROUP):
            row_copy(g, u, 0).start(priority=0)
            row_copy(g, u, 1).start(priority=1)
        return carry

    lax.fori_loop(0, tq // ISSUE_GROUP, issue, 0)
    for k in range(TOP_K):
        pltpu.make_async_copy(h_ref, xs_ref.at[pl.ds(0, tq * ROW_TILE), :], sem.at[k]).wait()


def _moe_dispatch(h2, slots, zero_tiles, nslot, tq, tm):
    d = LANES
    return pl.pallas_call(
        functools.partial(_dispatch_kernel, tq=tq, tm=tm),
        grid_spec=pltpu.PrefetchScalarGridSpec(
            num_scalar_prefetch=1,
            grid=(h2.shape[0] // (tq * ROW_TILE),),
            in_specs=[
                pl.BlockSpec((1, TOP_K, tq // ISSUE_GROUP, ISSUE_GROUP), lambda i, zt: (i, 0, 0, 0),
                             memory_space=pltpu.SMEM),
                pl.BlockSpec((tq * ROW_TILE, d), lambda i, zt: (i, 0)),
            ],
            out_specs=pl.BlockSpec(memory_space=pl.ANY),
            scratch_shapes=[pltpu.VMEM((tm * ROW_TILE, d), h2.dtype), pltpu.SemaphoreType.DMA((TOP_K,)),
                            pltpu.SemaphoreType.DMA((1,))],
        ),
        out_shape=jax.ShapeDtypeStruct((nslot * ROW_TILE, d), h2.dtype),
        compiler_params=_params(("arbitrary",)),
        name="moe_dispatch",
    )(zero_tiles, slots, h2)


def _combine_kernel(slot_ref, next_slot_ref, x_ref, rp_ref, g_ref, ys_ref, o_ref, y_ref, sem,
                    *, tq, normalize):
    i = pl.program_id(0)
    cur = i % 2

    def gather(slots, buf):
        def row_copy(g, u, k):
            src = pl.multiple_of(slots[0, k, g, u] * ROW_TILE, ROW_TILE)
            dst = pl.multiple_of(g * (ISSUE_GROUP * ROW_TILE), ISSUE_GROUP * ROW_TILE) + u * ROW_TILE
            return pltpu.make_async_copy(ys_ref.at[pl.ds(src, ROW_TILE), :],
                                         y_ref.at[buf, k, pl.ds(dst, ROW_TILE), :], sem.at[buf, k])

        def issue(g, carry):
            for u in range(ISSUE_GROUP):
                row_copy(g, u, 0).start(priority=0)
                row_copy(g, u, 1).start(priority=1)
            return carry

        lax.fori_loop(0, tq // ISSUE_GROUP, issue, 0)

    @pl.when(i == 0)
    def _():
        gather(slot_ref, 0)

    @pl.when(i + 1 < pl.num_programs(0))
    def _():
        gather(next_slot_ref, 1 - cur)

    for k in range(TOP_K):
        pltpu.make_async_copy(ys_ref.at[pl.ds(0, tq * ROW_TILE), :], y_ref.at[cur, k], sem.at[cur, k]).wait()
    rp = rp_ref[...]
    y = [jnp.concatenate([y_ref[cur, k, pl.ds(cb, tq, stride=ROW_TILE), :]
                          for cb in range(x_ref.shape[1] // LANES)], axis=1) for k in range(TOP_K)]
    xn = x_ref[...] + rp[:, 0:1] * y[0] + rp[:, 1:2] * y[1]
    o_ref[...] = _rms(xn, g_ref[...]) if normalize else xn


def _moe_combine(x2, route_p, slots, ys, g, tq, normalize):
    t, d = x2.shape
    slot_block = (1, TOP_K, tq // ISSUE_GROUP, ISSUE_GROUP)
    last = t // tq - 1
    return pl.pallas_call(
        functools.partial(_combine_kernel, tq=tq, normalize=normalize),
        grid=(t // tq,),
        in_specs=[
            pl.BlockSpec(slot_block, lambda i: (i, 0, 0, 0), memory_space=pltpu.SMEM),
            pl.BlockSpec(slot_block, lambda i: (jnp.minimum(i + 1, last), 0, 0, 0), memory_space=pltpu.SMEM),
            pl.BlockSpec((tq, d), lambda i: (i, 0)),
            pl.BlockSpec((tq, LANES), lambda i: (i, 0)),
            pl.BlockSpec((1, d), lambda i: (0, 0)),
            pl.BlockSpec(memory_space=pl.ANY),
        ],
        out_specs=pl.BlockSpec((tq, d), lambda i: (i, 0)),
        out_shape=jax.ShapeDtypeStruct((t, d), F32),
        scratch_shapes=[pltpu.VMEM((2, TOP_K, tq * ROW_TILE, LANES), F32), pltpu.SemaphoreType.DMA((2, TOP_K))],
        compiler_params=_params(("arbitrary",)),
        name="moe_combine",
    )(slots, slots, x2, route_p, g, ys)


def _final_norm_kernel(x_ref, g_ref, o_ref):
    o_ref[...] = _rms(x_ref[...], g_ref[...])


def _final_norm(x2, g, tq):
    t, d = x2.shape
    return pl.pallas_call(
        _final_norm_kernel,
        grid=(t // tq,),
        in_specs=[pl.BlockSpec((tq, d), lambda i: (i, 0)), pl.BlockSpec((1, d), lambda i: (0, 0))],
        out_specs=pl.BlockSpec((tq, d), lambda i: (i, 0)),
        out_shape=jax.ShapeDtypeStruct((t, d), F32),
        compiler_params=_params(("arbitrary",)),
        name="final_norm",
    )(x2, g)


def _rope_lane_tables(positions):
    inv_freq = ROPE_THETA ** (-jnp.arange(0, ROT_DIM, 2, dtype=F32) / ROT_DIM)
    ang = positions.astype(F32).reshape(-1, 1) * inv_freq
    cos, sin = jnp.cos(ang), jnp.sin(ang)
    t = ang.shape[0]
    rest = HEAD_DIM - ROT_DIM
    z8 = jnp.zeros((t, ROT_HALF), F32)
    cos_h = jnp.concatenate([cos, cos, jnp.ones((t, rest), F32)], axis=-1)
    lo_h = jnp.concatenate([-sin, z8, jnp.zeros((t, rest), F32)], axis=-1)
    hi_h = jnp.concatenate([z8, sin, jnp.zeros((t, rest), F32)], axis=-1)
    rep = LANES // HEAD_DIM
    return jnp.tile(cos_h, (1, rep)), jnp.tile(lo_h, (1, rep)), jnp.tile(hi_h, (1, rep))


def _moe_plan(route_i, counts, t, tm):
    cnt = counts[0, :N_EXPERTS].astype(jnp.int32)
    padded = ((cnt + tm - 1) // tm) * tm
    ends = jnp.cumsum(padded)
    offs = ends - padded
    slot = offs[route_i[:, 0:TOP_K]] + route_i[:, TOP_K:2 * TOP_K]
    ntile = (TOP_K * t) // tm + N_EXPERTS
    starts = jnp.arange(ntile, dtype=jnp.int32) * tm
    valid = (starts < ends[-1]).astype(jnp.int32)
    last = jnp.maximum(ends[-1] // tm - 1, 0)
    src = jnp.minimum(jnp.arange(ntile, dtype=jnp.int32), last)
    expert = jnp.sum(((src * tm)[:, None] >= ends[None, :]).astype(jnp.int32), axis=1)
    expert = jnp.minimum(expert, N_EXPERTS - 1)
    last_tile = jnp.where(padded > 0, ends // tm - 1, ntile - 1).astype(jnp.int32)
    tail = jnp.arange((TOP_K * t) // tm, ntile, dtype=jnp.int32)
    return slot, expert, valid, src, jnp.concatenate([last_tile, tail]), ntile * tm


def kernel(x, positions, norm1_g, w_in, b_gate, diff_lambda, diff_subln_g, pool_w, pool_scale,
           w_proj_a, w_proj_b, w_proj_c, w_out, norm2_g, ffn_w_gate, ffn_w_up, ffn_w_down,
           moe_router, moe_w_gate, moe_w_up, moe_w_down, final_norm_g):
    b, s, d = x.shape
    t = b * s
    depth = w_in.shape[0]
    tq = min(512, s)
    tm = min(512, s)
    ff = ffn_w_gate.shape[-1]
    cf = ff // 2 if ff % 512 == 0 and ff >= 1024 else ff
    x2 = x.reshape(t, d)
    cos_t, sinlo_t, sinhi_t = _rope_lane_tables(positions)
    out = None
    for l in range(depth):
        w_l = w_in[l]
        qkv = _qkv_proj(x2, norm1_g[l].reshape(1, d), w_l[:, :OFF_C].astype(BF16),
                        cos_t, sinlo_t, sinhi_t, tq)
        qa, ka, va = qkv[0:3], qkv[3:6], qkv[6:9]
        qb, kb, vb = qkv[9:12]
        oa, lse = [], []
        for g, (window, dil) in enumerate(DILATED_PATTERNS):
            assert window // dil == BLOCK
            o_g, lse_g = _dilated_attn(qa[g], ka[g], va[g], b, s, dil, 512)
            oa.append(o_g)
            lse.append(lse_g)
        lambda_init = 0.8 - 0.6 * math.exp(-0.3 * l)
        tq_b = min(1024, s)
        ob = _diff_attn(qb, kb, vb, diff_lambda[l], diff_subln_g[l].reshape(1, B_HEAD_WIDTH),
                        b, s, tq_b, min(512, tq_b // 2), min(256, tq_b // 2), lambda_init)

        dense = l % 2 == 0
        router = None
        if not dense:
            wr = jnp.zeros((d, LANES), F32).at[:, :N_EXPERTS].set(moe_router[l // 2])
            wr_hi = wr.astype(BF16)
            router = (wr_hi, (wr - wr_hi.astype(F32)).astype(BF16))
        merged = _merge(
            x2, norm1_g[l].reshape(1, d), w_l[:, OFF_C:OFF_G].astype(BF16), w_l[:, OFF_G:].astype(BF16),
            b_gate[l].reshape(1, N_BRANCH * d), oa, lse, ob, pool_w[l].astype(BF16),
            pool_scale[l].reshape(1, C_WIDTH), w_proj_a[l].astype(BF16), w_proj_b[l].astype(BF16),
            w_proj_c[l].astype(BF16), w_out[l].astype(BF16), norm2_g[l].reshape(1, d),
            router, s, tq, BF16 if dense else F32)
        if dense:
            xn, h2 = merged
            i = l // 2
            ntile = t // tm
            ident = jnp.arange(ntile, dtype=jnp.int32)
            x2 = _grouped_swiglu(h2, xn, ffn_w_gate[i:i + 1].astype(BF16), ffn_w_up[i:i + 1].astype(BF16),
                                 ffn_w_down[i:i + 1].astype(BF16), jnp.zeros((ntile,), jnp.int32),
                                 jnp.ones((ntile,), jnp.int32), ident, tm, cf)
            out = None
        else:
            xn, h2, route_i, route_p, counts = merged
            i = l // 2
            slot, expert, valid, src, zero_tiles, nslot = _moe_plan(route_i, counts, t, tm)
            tq_d, tq_c = min(2048, t), min(1024, t)
            tiled = lambda n: slot.reshape(t // n, n, TOP_K).transpose(0, 2, 1).reshape(
                t // n, TOP_K, n // ISSUE_GROUP, ISSUE_GROUP)
            xs = _moe_dispatch(h2, tiled(tq_d), zero_tiles, nslot, tq_d, tm)
            ys = _grouped_swiglu(xs, None, moe_w_gate[i].astype(BF16), moe_w_up[i].astype(BF16),
                                 moe_w_down[i].astype(BF16), expert, valid, src, tm, cf)
            last = l == depth - 1
            res = _moe_combine(xn, route_p, tiled(tq_c), ys, final_norm_g.reshape(1, d), tq_c, last)
            if last:
                out = res
            else:
                x2 = res
    if out is None:
        out = _final_norm(x2, final_norm_g.reshape(1, d), tq)
    return out.reshape(b, s, d)
```

```python
import functools
import math

import jax
import jax.numpy as jnp
from jax import lax
from jax.experimental import pallas as pl
from jax.experimental.pallas import tpu as pltpu

F32 = jnp.float32
BF16 = jnp.bfloat16

HEAD_DIM = 64
ROPE_THETA = 500000.0
ROT_DIM = HEAD_DIM // 4
ROT_HALF = ROT_DIM // 2
BLOCK = 128
EPS = 1e-6
NEG = -1e30

DILATED_PATTERNS = ((128, 1), (512, 4), (2048, 16))
A_GROUPS = len(DILATED_PATTERNS)
A_HEADS_PER_GROUP = 4
A_GROUP_WIDTH = A_HEADS_PER_GROUP * HEAD_DIM
A_WIDTH = A_GROUPS * A_GROUP_WIDTH

B_HEADS = 4
B_HEAD_WIDTH = 2 * HEAD_DIM
B_WIDTH = B_HEADS * B_HEAD_WIDTH
B_VEXT = 2 * B_HEAD_WIDTH

POOL_WINDOWS = (2, 4, 8, 16)
C_GROUP_DIM = 128
C_WIDTH = len(POOL_WINDOWS) * C_GROUP_DIM
POOL_HALO = 16

N_BRANCH = 3
N_EXPERTS = 8
TOP_K = 2

OFF_QA = 0
OFF_KA = OFF_QA + A_WIDTH
OFF_VA = OFF_KA + A_WIDTH
OFF_QB = OFF_VA + A_WIDTH
OFF_KB = OFF_QB + B_WIDTH
OFF_VB = OFF_KB + B_WIDTH
OFF_C = OFF_VB + B_WIDTH
OFF_G = OFF_C + C_WIDTH

LANES = 128
MXU_WIDTH = 256
ROW_TILE = 8
ISSUE_GROUP = LANES
V7X_VMEM_BYTES = 64 * 1024 * 1024
VMEM_LIMIT = V7X_VMEM_BYTES - 8 * 1024 * 1024

QK_SCALE = HEAD_DIM ** -0.5
LOG2_E = math.log2(math.e)


def _params(semantics):
    return pltpu.CompilerParams(dimension_semantics=semantics, vmem_limit_bytes=VMEM_LIMIT)


def _rms(x, g):
    return x * lax.rsqrt(jnp.mean(x * x, axis=-1, keepdims=True) + EPS) * g


def _dot(a, b):
    return jnp.dot(a, b, preferred_element_type=F32)


def _dot_nt(a, b):
    return lax.dot_general(a, b, (((1,), (1,)), ((), ())), preferred_element_type=F32)


def _qkv_kernel(x_ref, g_ref, w_ref, cos_ref, sin_ref,
                qa0, qa1, qa2, ka0, ka1, ka2, va0, va1, va2, qb, kb, vb, zs_ref):
    tq = x_ref.shape[0]
    hb = _rms(x_ref[...], g_ref[...]).astype(BF16)
    cos = cos_ref[...]
    sin = sin_ref[...]
    first_half = (lax.broadcasted_iota(jnp.int32, sin.shape, 1) % HEAD_DIM) < ROT_HALF
    sinlo = jnp.where(first_half, -sin, 0.0)
    sinhi = jnp.where(first_half, 0.0, sin)

    def rope(z):
        return (z * cos + pltpu.roll(z, LANES - ROT_HALF, 1) * sinlo
                + pltpu.roll(z, ROT_HALF, 1) * sinhi)

    def project(off, width, out_ref, rotary, scale, dil=1):
        for c in range(0, width, MXU_WIDTH):
            zz = _dot(hb, w_ref[:, off + c:off + c + MXU_WIDTH])
            for p in range(0, MXU_WIDTH, LANES):
                z = zz[:, p:p + LANES]
                if rotary:
                    z = rope(z)
                if scale != 1.0:
                    z = z * scale
                if dil == 1:
                    out_ref[:, c + p:c + p + LANES] = z.astype(out_ref.dtype)
                else:
                    zs_ref[(c + p) // LANES] = z
        if dil > 1:
            for r in range(dil):
                for sl in range(width // LANES):
                    out_ref[:, r * width + sl * LANES:r * width + (sl + 1) * LANES] = (
                        zs_ref[sl, pl.ds(r, tq // dil, stride=dil), :].astype(out_ref.dtype))

    for g, (q_ref, k_ref, v_ref) in enumerate(((qa0, ka0, va0), (qa1, ka1, va1), (qa2, ka2, va2))):
        dil = DILATED_PATTERNS[g][1]
        project(OFF_QA + g * A_GROUP_WIDTH, A_GROUP_WIDTH, q_ref, True, QK_SCALE * LOG2_E, dil)
        project(OFF_KA + g * A_GROUP_WIDTH, A_GROUP_WIDTH, k_ref, True, 1.0, dil)
        project(OFF_VA + g * A_GROUP_WIDTH, A_GROUP_WIDTH, v_ref, False, 1.0, dil)
    project(OFF_QB, B_WIDTH, qb, True, QK_SCALE * LOG2_E)
    project(OFF_KB, B_WIDTH, kb, True, 1.0)
    lane = lax.broadcasted_iota(jnp.int32, (x_ref.shape[0], B_HEAD_WIDTH), 1)
    ones_col = jnp.where(lane == 0, 1.0, 0.0).astype(vb.dtype)
    heads_per_dot = MXU_WIDTH // B_HEAD_WIDTH
    for h0 in range(0, B_HEADS, heads_per_dot):
        zz = _dot(hb, w_ref[:, OFF_VB + h0 * B_HEAD_WIDTH:OFF_VB + (h0 + heads_per_dot) * B_HEAD_WIDTH])
        for hh in range(heads_per_dot):
            h = h0 + hh
            z = zz[:, hh * B_HEAD_WIDTH:(hh + 1) * B_HEAD_WIDTH]
            vb[:, h * B_VEXT:h * B_VEXT + B_HEAD_WIDTH] = z.astype(vb.dtype)
            vb[:, h * B_VEXT + B_HEAD_WIDTH:(h + 1) * B_VEXT] = ones_col


def _qkv_proj(x2, g, w_qkv, cos_t, sin_t, tq):
    t, d = x2.shape
    row = lambda i: (i, 0)
    const = lambda i: (0, 0)
    dils = [dil for _ in range(3) for (_, dil) in DILATED_PATTERNS]
    shapes = [(t // dil, dil * A_GROUP_WIDTH, tq // dil) for dil in dils]
    shapes += [(t, B_WIDTH, tq), (t, B_WIDTH, tq), (t, B_HEADS * B_VEXT, tq)]
    return pl.pallas_call(
        _qkv_kernel,
        grid=(t // tq,),
        in_specs=[
            pl.BlockSpec((tq, d), row),
            pl.BlockSpec((1, d), const),
            pl.BlockSpec(w_qkv.shape, const, pipeline_mode=pl.Buffered(1)),
            pl.BlockSpec((tq, LANES), row),
            pl.BlockSpec((tq, LANES), row),
        ],
        out_specs=[pl.BlockSpec((rows, w), row) for (_, w, rows) in shapes],
        out_shape=[jax.ShapeDtypeStruct((n, w), BF16) for (n, w, _) in shapes],
        scratch_shapes=[pltpu.VMEM((A_GROUP_WIDTH // LANES, tq, LANES), F32)],
        compiler_params=_params(("arbitrary",)),
        name="qkv_proj",
    )(x2, g, w_qkv, cos_t, sin_t)


def _dilated_kernel(q_ref, kp_ref, kc_ref, vp_ref, vc_ref, o_ref, lse_ref, *, nsub):
    n = pl.program_id(2)
    a = lax.broadcasted_iota(jnp.int32, (BLOCK, 2 * BLOCK), 0)
    j = lax.broadcasted_iota(jnp.int32, (BLOCK, 2 * BLOCK), 1)
    band = (j >= a) & (j <= a + BLOCK)
    band_first = band & ((j >= BLOCK) | (n > 0))
    low_k = lax.broadcasted_iota(jnp.int32, (2 * BLOCK, LANES), 1) < HEAD_DIM
    low_o = lax.broadcasted_iota(jnp.int32, (BLOCK, LANES), 1) < HEAD_DIM
    ones_slab = jnp.ones((2 * BLOCK, LANES), BF16)
    for sb in range(nsub):
        rows = slice(sb * BLOCK, (sb + 1) * BLOCK)
        if sb == 0:
            kcat = jnp.concatenate([kp_ref[...], kc_ref[rows, :]], axis=0)
            vcat = jnp.concatenate([vp_ref[...], vc_ref[rows, :]], axis=0)
            mask = band_first
        else:
            kcat = kc_ref[(sb - 1) * BLOCK:(sb + 1) * BLOCK, :]
            vcat = vc_ref[(sb - 1) * BLOCK:(sb + 1) * BLOCK, :]
            mask = band
        q = q_ref[rows, :]
        for hp in range(A_HEADS_PER_GROUP // 2):
            pair = slice(hp * LANES, (hp + 1) * LANES)
            q_pair = q[:, pair]
            k_pair = kcat[:, pair]
            v_ext = jnp.concatenate([vcat[:, pair], ones_slab], axis=1)
            o_half, lse_half = [], []
            for half in range(2):
                k_h = jnp.where(low_k if half == 0 else ~low_k, k_pair, jnp.zeros_like(k_pair))
                s = jnp.where(mask, _dot_nt(q_pair, k_h), NEG)
                m = jnp.max(jnp.maximum(s[:, :LANES], s[:, LANES:]), axis=-1, keepdims=True)
                p = jnp.concatenate([jnp.exp2(s[:, :LANES] - m), jnp.exp2(s[:, LANES:] - m)], axis=1)
                pv = _dot(p.astype(BF16), v_ext)
                l = pv[:, LANES:]
                o_half.append(pv[:, :LANES] / l)
                lse_half.append(m + jnp.log2(l))
            o_ref[rows, pair] = jnp.where(low_o, o_half[0], o_half[1]).astype(o_ref.dtype)
            lse_ref[rows, pair] = jnp.where(low_o, lse_half[0], lse_half[1])


def _dilated_attn(q, k, v, b, s, dil, qrows):
    w = A_GROUP_WIDTH
    l = s // dil
    qrows = min(qrows, l)
    nsub = qrows // BLOCK
    view = lambda t: t.reshape(b, l, dil * w)
    cur = lambda bi, r, n: (bi, n, r)
    prev = lambda bi, r, n: (bi, jnp.maximum(n * nsub - 1, 0), r)
    o, lse = pl.pallas_call(
        functools.partial(_dilated_kernel, nsub=nsub),
        grid=(b, dil, l // qrows),
        in_specs=[
            pl.BlockSpec((None, qrows, w), cur),
            pl.BlockSpec((None, BLOCK, w), prev),
            pl.BlockSpec((None, qrows, w), cur),
            pl.BlockSpec((None, BLOCK, w), prev),
            pl.BlockSpec((None, qrows, w), cur),
        ],
        out_specs=[pl.BlockSpec((None, qrows, w), cur), pl.BlockSpec((None, qrows, w), cur)],
        out_shape=[jax.ShapeDtypeStruct((b, l, dil * w), BF16),
                   jax.ShapeDtypeStruct((b, l, dil * w), F32)],
        compiler_params=_params(("arbitrary", "arbitrary", "arbitrary")),
        name=f"dilated_attn_d{dil}",
    )(view(q), view(k), view(k), view(v), view(v))
    return o.reshape(b * l, dil * w), lse.reshape(b * l, dil * w)


def _diff_kernel(lam_ref, g_ref, q_ref, k_ref, v_ref, o_ref, *scratch, tq, tk, rows, nh, lambda_init):
    i = pl.program_id(2)
    nd = tq // tk
    q = q_ref[...]
    streams = [(mp, rb) for rb in range(tq // rows) for mp in range(2 * nh)]
    acc_all, m_all = scratch[0], scratch[1]
    acc_all[...] = jnp.zeros_like(acc_all)
    m_all[...] = jnp.full_like(m_all, NEG)
    acc_refs = {(mp, rb): acc_all.at[mp, rb] for mp, rb in streams}
    m_refs = {(mp, rb): m_all.at[mp, rb] for mp, rb in streams}
    sa_ref = {(mp, rb): scratch[2].at[mp, rb] for mp, rb in streams}
    sb_ref = {(mp, rb): scratch[3].at[mp, rb] for mp, rb in streams}

    def visible(rb, diag):
        if diag is None:
            return tk
        return max(0, min(tk, (rb + 1) * rows - diag * tk))

    def scores(jc, dst, diag=None):
        k = k_ref[pl.ds(pl.multiple_of(jc * tk, tk), tk), :]
        for mp, rb in streams:
            if visible(rb, diag) == 0:
                continue
            cols = slice(mp * HEAD_DIM, (mp + 1) * HEAD_DIM)
            rs = slice(rb * rows, (rb + 1) * rows)
            dst[mp, rb][...] = _dot_nt(q[rs, cols], k[:, cols])

    def consume(jc, src, diag=None):
        start = pl.multiple_of(jc * tk, tk)
        for mp, rb in streams:
            vis = visible(rb, diag)
            if vis == 0:
                continue
            acc_ref, m_ref = acc_refs[mp, rb], m_refs[mp, rb]
            s = src[mp, rb][:, :vis]
            if diag is not None and diag * tk + vis > rb * rows + 1:
                r = lax.broadcasted_iota(jnp.int32, (rows, vis), 0) + rb * rows
                c = lax.broadcasted_iota(jnp.int32, (rows, vis), 1) + diag * tk
                s = jnp.where(c <= r, s, NEG)
            slabs = [s[:, n * LANES:(n + 1) * LANES] for n in range(vis // LANES)]
            smax = slabs[0]
            for sl in slabs[1:]:
                smax = jnp.maximum(smax, sl)
            m_old = m_ref[...]
            m_new = jnp.maximum(m_old, jnp.max(smax, axis=-1, keepdims=True))
            alpha = jnp.exp2(m_old - m_new)
            p = jnp.concatenate([jnp.exp2(sl - m_new) for sl in slabs], axis=1).astype(BF16)
            vcols = slice((mp // 2) * B_VEXT, (mp // 2 + 1) * B_VEXT)
            pv = _dot(p, v_ref[pl.ds(start, vis), vcols])
            acc = acc_ref[...]
            acc_ref[...] = jnp.concatenate(
                [acc[:, n * LANES:(n + 1) * LANES] * alpha for n in range(B_VEXT // LANES)], axis=1) + pv
            m_ref[...] = m_new

    scores(0, sa_ref)

    def pair(t, carry):
        jc = 2 * t
        scores(jc + 1, sb_ref)
        consume(jc, sa_ref)
        scores(jc + 2, sa_ref)
        consume(jc + 1, sb_ref)
        return carry

    lax.fori_loop(0, i * (nd // 2), pair, 0)
    first = nd * i
    bufs = (sa_ref, sb_ref)
    for dg in range(nd):
        if dg + 1 < nd:
            scores(first + dg + 1, bufs[(dg + 1) % 2], diag=dg + 1)
        consume(first + dg, bufs[dg % 2], diag=dg)

    lp = lam_ref[...]
    lam = (jnp.exp(jnp.sum(lp[0:1] * lp[1:2], axis=-1, keepdims=True))
           - jnp.exp(jnp.sum(lp[2:3] * lp[3:4], axis=-1, keepdims=True)) + lambda_init)
    for hd in range(nh):
        for rb in range(tq // rows):
            acc1 = acc_refs[2 * hd, rb][...]
            acc2 = acc_refs[2 * hd + 1, rb][...]
            o1 = acc1[:, :B_HEAD_WIDTH] / acc1[:, B_HEAD_WIDTH:B_HEAD_WIDTH + 1]
            o2 = acc2[:, :B_HEAD_WIDTH] / acc2[:, B_HEAD_WIDTH:B_HEAD_WIDTH + 1]
            o = o1 - lam * o2
            o_ref[rb * rows:(rb + 1) * rows, hd * B_HEAD_WIDTH:(hd + 1) * B_HEAD_WIDTH] = (
                _rms(o, g_ref[...]) * (1.0 - lambda_init)).astype(o_ref.dtype)


def _diff_attn(qb, kb, vb, lam_p, subln_g, b, s, tq, tk, rows, nh, lambda_init):
    hw = nh * B_HEAD_WIDTH
    q3 = qb.reshape(b, s, B_WIDTH)
    k3 = kb.reshape(b, s, B_WIDTH)
    v3 = vb.reshape(b, s, B_HEADS * B_VEXT)
    assert tq % (2 * tk) == 0 and tk % rows == 0 and B_HEADS % nh == 0
    nmap = 2 * nh
    scratch = ([pltpu.VMEM((nmap, tq // rows, rows, B_VEXT), F32), pltpu.VMEM((nmap, tq // rows, rows, LANES), F32)]
               + [pltpu.VMEM((nmap, tq // rows, rows, tk), F32)] * 2)
    out = pl.pallas_call(
        functools.partial(_diff_kernel, tq=tq, tk=tk, rows=rows, nh=nh, lambda_init=lambda_init),
        grid=(b, B_HEADS // nh, s // tq),
        in_specs=[
            pl.BlockSpec(lam_p.shape, lambda bi, h, i: (0, 0)),
            pl.BlockSpec((1, B_HEAD_WIDTH), lambda bi, h, i: (0, 0)),
            pl.BlockSpec((None, tq, hw), lambda bi, h, i: (bi, i, h)),
            pl.BlockSpec((None, s, hw), lambda bi, h, i: (bi, 0, h)),
            pl.BlockSpec((None, s, nh * B_VEXT), lambda bi, h, i: (bi, 0, h)),
        ],
        out_specs=pl.BlockSpec((None, tq, hw), lambda bi, h, i: (bi, i, h)),
        out_shape=jax.ShapeDtypeStruct((b, s, B_WIDTH), BF16),
        scratch_shapes=scratch,
        compiler_params=_params(("arbitrary", "arbitrary", "arbitrary")),
        name="diff_attn",
    )(lam_p, subln_g, q3, k3, v3)
    return out.reshape(b * s, B_WIDTH)


def _merge_kernel(*refs, tq, rows, tiles_per_seq, route):
    (x_ref, g1_ref, wc_ref, wg_ref, bg_ref, oa0, oa1, oa2, ls0, ls1, ls2, ob_ref,
     pw_ref, ps_ref, wpa_ref, wpb_ref, wpc_ref, wo_ref, g2_ref) = refs[:19]
    if route:
        wrc_ref = refs[19]
        xo_ref, h2_ref, ri_ref, rp_ref, cnt_ref, zc_ref, tm_ref, carry_ref = refs[20:]
    else:
        xo_ref, h2_ref, zc_ref, tm_ref = refs[19:]
    step = pl.program_id(0)
    seq_tile = step % tiles_per_seq
    d = x_ref.shape[-1]
    dils = [dil for (_, dil) in DILATED_PATTERNS]
    slabs = A_GROUP_WIDTH // LANES

    @pl.when(seq_tile == 0)
    def _():
        zc_ref[0:POOL_HALO, :] = jnp.zeros((POOL_HALO, C_WIDTH), F32)

    if route:
        @pl.when(step == 0)
        def _():
            carry_ref[...] = jnp.zeros_like(carry_ref)

    for part in range(tq // rows):
        r0 = part * rows
        rs = slice(r0, r0 + rows)
        x = x_ref[rs, :]
        hb = _rms(x, g1_ref[...]).astype(BF16)

        zc_ref[POOL_HALO + r0:POOL_HALO + r0 + rows, :] = _dot(hb, wc_ref[...])
        pos = seq_tile * tq + r0 + lax.broadcasted_iota(jnp.int32, (rows, C_GROUP_DIM), 0)
        pooled = []
        for gi, win in enumerate(POOL_WINDOWS):
            cols = slice(gi * C_GROUP_DIM, (gi + 1) * C_GROUP_DIM)
            tok = zc_ref[POOL_HALO + r0:POOL_HALO + r0 + rows, cols]
            tot = tok
            for back in range(1, win):
                tot = tot + zc_ref[POOL_HALO + r0 - back:POOL_HALO + r0 - back + rows, cols]
            cnt = jnp.minimum(pos + 1, win).astype(F32)
            dmean = tot / cnt - tok
            pooled.append(_dot(dmean.astype(BF16), pw_ref[gi]))
        out_c = (jnp.concatenate(pooled, axis=-1) * ps_ref[...]).astype(BF16)

        def token_major(src_ref, slot, dil):
            if dil == 1:
                return src_ref[rs, :].astype(F32)
            src_rows = slice(r0 // dil, (r0 + rows) // dil)
            for r in range(dil):
                for sl in range(slabs):
                    col = r * A_GROUP_WIDTH + sl * LANES
                    tm_ref[slot * slabs + sl, pl.ds(r0 + r, rows // dil, stride=dil), :] = (
                        src_ref[src_rows, col:col + LANES].astype(F32))
            return jnp.concatenate([tm_ref[slot * slabs + sl, rs, :] for sl in range(slabs)], axis=1)

        l0, l1, l2 = (token_major(ref, gi, dil) for gi, (ref, dil) in enumerate(zip((ls0, ls1, ls2), dils)))
        o0, o1, o2 = (token_major(ref, A_GROUPS + gi, dil)
                      for gi, (ref, dil) in enumerate(zip((oa0, oa1, oa2), dils)))
        lm = jnp.maximum(jnp.maximum(l0, l1), l2)
        e0, e1, e2 = jnp.exp2(l0 - lm), jnp.exp2(l1 - lm), jnp.exp2(l2 - lm)
        out_a = ((e0 * o0 + e1 * o1 + e2 * o2) / (e0 + e1 + e2)).astype(BF16)

        branches = (_dot(out_a, wpa_ref[...]), _dot(ob_ref[rs, :], wpb_ref[...]), _dot(out_c, wpc_ref[...]))
        mixed = None
        for bi, proj in enumerate(branches):
            zg = _dot(hb, wg_ref[:, bi * d:(bi + 1) * d]) + bg_ref[:, bi * d:(bi + 1) * d]
            term = jax.nn.sigmoid(zg) * proj
            mixed = term if mixed is None else mixed + term
        xn = x + _dot(mixed.astype(BF16), wo_ref[...])
        xo_ref[rs, :] = xn
        h2 = _rms(xn, g2_ref[...])
        if route:
            for cb in range(d // LANES):
                h2_ref[pl.ds(r0 * ROW_TILE + cb, rows, stride=ROW_TILE), :] = h2[:, cb * LANES:(cb + 1) * LANES]
        else:
            h2_ref[rs, :] = h2.astype(h2_ref.dtype)

        if route:
            h_hi = h2.astype(BF16)
            h_lo = (h2 - h_hi.astype(F32)).astype(BF16)
            hi_both = _dot(h_hi, wrc_ref[...])
            logits = hi_both[:, :LANES] + (_dot(h_lo, wrc_ref[:, :LANES]) + hi_both[:, LANES:])
            lane = lax.broadcasted_iota(jnp.int32, (rows, LANES), 1)
            lane_f = lane.astype(F32)
            logits = jnp.where(lane < N_EXPERTS, logits, -jnp.inf)
            m1 = jnp.max(logits, axis=-1, keepdims=True)
            i1 = jnp.min(jnp.where(logits == m1, lane_f, float(LANES)), axis=-1, keepdims=True)
            rest = jnp.where(lane_f == i1, -jnp.inf, logits)
            m2 = jnp.max(rest, axis=-1, keepdims=True)
            i2 = jnp.min(jnp.where(rest == m2, lane_f, float(LANES)), axis=-1, keepdims=True)
            e21 = jnp.exp(m2 - m1)
            p1 = 1.0 / (1.0 + e21)
            p2 = e21 * p1
            sel1 = lane_f == i1
            sel2 = lane_f == i2
            chosen = jnp.where(sel1 | sel2, 1.0, 0.0)

            r = lax.broadcasted_iota(jnp.int32, (rows, rows), 0)
            c = lax.broadcasted_iota(jnp.int32, (rows, rows), 1)
            before = _dot(jnp.where(c < r, 1.0, 0.0).astype(BF16), chosen.astype(BF16)) + carry_ref[...]
            rank1 = jnp.sum(jnp.where(sel1, before, 0.0), axis=-1, keepdims=True)
            rank2 = jnp.sum(jnp.where(sel2, before, 0.0), axis=-1, keepdims=True)
            carry_ref[...] = carry_ref[...] + jnp.sum(chosen, axis=0, keepdims=True)
            packed = jnp.where(lane == 0, i1, jnp.where(lane == 1, i2,
                               jnp.where(lane == 2, rank1, jnp.where(lane == 3, rank2, 0.0))))
            ri_ref[rs, :] = packed.astype(jnp.int32)
            rp_ref[rs, :] = jnp.where(lane == 0, p1, jnp.where(lane == 1, p2, 0.0))

    zc_ref[0:POOL_HALO, :] = zc_ref[tq:tq + POOL_HALO, :]
    if route:
        cnt_ref[...] = jnp.broadcast_to(carry_ref[...], cnt_ref.shape)


def _merge(x2, g1, w_c, w_g, b_g, oa, lse, ob, pool_w, pool_scale, wpa, wpb, wpc, wo, g2,
           router, s, tq, h2_dtype):
    t, d = x2.shape
    route = router is not None
    row = lambda i: (i, 0)
    const = lambda i: (0, 0)
    resident = lambda a: pl.BlockSpec(a.shape, lambda i: (0,) * a.ndim, pipeline_mode=pl.Buffered(1))
    args = [x2, g1, w_c, w_g, b_g, *oa, *lse, ob, pool_w, pool_scale, wpa, wpb, wpc, wo, g2]
    in_specs = [pl.BlockSpec((tq, d), row), pl.BlockSpec((1, d), const), resident(w_c), resident(w_g),
                pl.BlockSpec(b_g.shape, const)]
    in_specs += [pl.BlockSpec((tq // dil, dil * A_GROUP_WIDTH), row) for (_, dil) in DILATED_PATTERNS] * 2
    in_specs += [pl.BlockSpec((tq, B_WIDTH), row), resident(pool_w), pl.BlockSpec(pool_scale.shape, const),
                 resident(wpa), resident(wpb), resident(wpc), resident(wo), pl.BlockSpec((1, d), const)]
    if route:
        assert d == ROW_TILE * LANES, "a token row must fill exactly one (8,128) f32 tile"
        h2_spec = pl.BlockSpec((tq * ROW_TILE, LANES), row)
        h2_shape = jax.ShapeDtypeStruct((t * ROW_TILE, LANES), F32)
    else:
        h2_spec = pl.BlockSpec((tq, d), row)
        h2_shape = jax.ShapeDtypeStruct((t, d), h2_dtype)
    out_specs = [pl.BlockSpec((tq, d), row), h2_spec]
    out_shape = [jax.ShapeDtypeStruct((t, d), F32), h2_shape]
    scratch = [pltpu.VMEM((POOL_HALO + tq, C_WIDTH), F32), pltpu.VMEM((2 * A_GROUPS * (A_GROUP_WIDTH // LANES), tq, LANES), F32)]
    if route:
        args += [router]
        in_specs += [resident(router)]
        out_specs += [pl.BlockSpec((tq, LANES), row), pl.BlockSpec((tq, LANES), row),
                      pl.BlockSpec((8, LANES), const)]
        out_shape += [jax.ShapeDtypeStruct((t, LANES), jnp.int32), jax.ShapeDtypeStruct((t, LANES), F32),
                      jax.ShapeDtypeStruct((8, LANES), F32)]
        scratch += [pltpu.VMEM((1, LANES), F32)]
    return pl.pallas_call(
        functools.partial(_merge_kernel, tq=tq, rows=min(256, tq), tiles_per_seq=s // tq, route=route),
        grid=(t // tq,),
        in_specs=in_specs,
        out_specs=out_specs,
        out_shape=out_shape,
        scratch_shapes=scratch,
        compiler_params=_params(("arbitrary",)),
        name="merge_route" if route else "merge",
    )(*args)


def _swiglu_kernel(te_ref, tv_ref, ts_ref, *refs, residual, nsteps):
    if residual:
        x_ref, res_ref, wg_ref, wu_ref, wd_ref, o_ref, acc_ref = refs
    else:
        x_ref, wg_ref, wu_ref, wd_ref, o_ref, acc_ref = refs
    j = pl.program_id(0)
    c = pl.program_id(1)
    tm, d = acc_ref.shape
    nchunk = d // LANES

    @pl.when((tv_ref[j] == 0) & (c == 0))
    def _():
        o_ref[...] = jnp.zeros_like(o_ref)

    def step(first, last):
        if residual:
            xb = x_ref[...].astype(BF16)
        else:
            xb = jnp.concatenate([x_ref[pl.ds(cb, tm, stride=ROW_TILE), :] for cb in range(nchunk)],
                                 axis=1).astype(BF16)
        gate = _dot(xb, wg_ref[...])
        up = _dot(xb, wu_ref[...])
        mid = (gate * jax.nn.sigmoid(gate) * up).astype(BF16)
        part = _dot(mid, wd_ref[...])
        if not first:
            part = acc_ref[...] + part
        if not last:
            acc_ref[...] = part
        elif residual:
            o_ref[...] = res_ref[...] + part
        else:
            for cb in range(nchunk):
                o_ref[pl.ds(cb, tm, stride=ROW_TILE), :] = part[:, cb * LANES:(cb + 1) * LANES]

    valid = tv_ref[j] > 0
    if nsteps == 1:
        pl.when(valid)(lambda: step(True, True))
    else:
        pl.when(valid & (c == 0))(lambda: step(True, False))
        pl.when(valid & (c == nsteps - 1))(lambda: step(False, True))
        if nsteps > 2:
            pl.when(valid & (c > 0) & (c < nsteps - 1))(lambda: step(False, False))


def _grouped_swiglu(xs, res, wg, wu, wd, tile_expert, tile_valid, tile_src, tm, cf):
    d = wg.shape[1]
    ff = wg.shape[-1]
    residual = res is not None
    xmap = lambda j, c, te, tv, ts: (ts[j], 0)
    if residual:
        n = xs.shape[0]
        row_block = (tm, d)
        in_specs = [pl.BlockSpec(row_block, xmap), pl.BlockSpec(row_block, xmap)]
        args = [xs, res]
    else:
        n = xs.shape[0] // ROW_TILE
        row_block = (tm * ROW_TILE, LANES)
        in_specs = [pl.BlockSpec(row_block, xmap)]
        args = [xs]
    in_specs += [
        pl.BlockSpec((None, d, cf), lambda j, c, te, tv, ts: (te[j], 0, c * tv[j])),
        pl.BlockSpec((None, d, cf), lambda j, c, te, tv, ts: (te[j], 0, c * tv[j])),
        pl.BlockSpec((None, cf, d), lambda j, c, te, tv, ts: (te[j], c * tv[j], 0)),
    ]
    args += [wg, wu, wd]
    return pl.pallas_call(
        functools.partial(_swiglu_kernel, residual=residual, nsteps=ff // cf),
        grid_spec=pltpu.PrefetchScalarGridSpec(
            num_scalar_prefetch=3,
            grid=(n // tm, ff // cf),
            in_specs=in_specs,
            out_specs=pl.BlockSpec(row_block, lambda j, c, te, tv, ts: (j, 0)),
            scratch_shapes=[pltpu.VMEM((tm, d), F32)],
        ),
        out_shape=jax.ShapeDtypeStruct(xs.shape, F32),
        compiler_params=_params(("arbitrary", "arbitrary")),
        name="grouped_swiglu_res" if residual else "grouped_swiglu",
    )(tile_expert, tile_valid, tile_src, *args)


def _dispatch_kernel(zt_ref, slot_ref, h_ref, xs_ref, zero_ref, sem, zsem, *, tq, tm):
    @pl.when(pl.program_id(0) == 0)
    def _():
        zero_ref[...] = jnp.zeros_like(zero_ref)
        for z in range(zt_ref.shape[0]):
            start = pl.multiple_of(zt_ref[z] * (tm * ROW_TILE), tm * ROW_TILE)
            fill = pltpu.make_async_copy(zero_ref, xs_ref.at[pl.ds(start, tm * ROW_TILE), :], zsem.at[0])
            fill.start()
            fill.wait()

    def row_copy(g, u, k):
        src = pl.multiple_of(g * (ISSUE_GROUP * ROW_TILE), ISSUE_GROUP * ROW_TILE) + u * ROW_TILE
        dst = pl.multiple_of(slot_ref[0, k, g, u] * ROW_TILE, ROW_TILE)
        return pltpu.make_async_copy(h_ref.at[pl.ds(src, ROW_TILE), :],
                                     xs_ref.at[pl.ds(dst, ROW_TILE), :], sem.at[k])

    def issue(g, carry):
        for u in range(ISSUE_GROUP):
            row_copy(g, u, 0).start(priority=0)
            row_copy(g, u, 1).start(priority=1)
        return carry

    lax.fori_loop(0, tq // ISSUE_GROUP, issue, 0)
    for k in range(TOP_K):
        pltpu.make_async_copy(h_ref, xs_ref.at[pl.ds(0, tq * ROW_TILE), :], sem.at[k]).wait()


def _moe_dispatch(h2, slots, zero_tiles, nslot, tq, tm):
    d = LANES
    return pl.pallas_call(
        functools.partial(_dispatch_kernel, tq=tq, tm=tm),
        grid_spec=pltpu.PrefetchScalarGridSpec(
            num_scalar_prefetch=1,
            grid=(h2.shape[0] // (tq * ROW_TILE),),
            in_specs=[
                pl.BlockSpec((1, TOP_K, tq // ISSUE_GROUP, ISSUE_GROUP), lambda i, zt: (i, 0, 0, 0),
                             memory_space=pltpu.SMEM),
                pl.BlockSpec((tq * ROW_TILE, d), lambda i, zt: (i, 0)),
            ],
            out_specs=pl.BlockSpec(memory_space=pl.ANY),
            scratch_shapes=[pltpu.VMEM((tm * ROW_TILE, d), h2.dtype), pltpu.SemaphoreType.DMA((TOP_K,)),
                            pltpu.SemaphoreType.DMA((1,))],
        ),
        out_shape=jax.ShapeDtypeStruct((nslot * ROW_TILE, d), h2.dtype),
        compiler_params=_params(("arbitrary",)),
        name="moe_dispatch",
    )(zero_tiles, slots, h2)


def _combine_kernel(slot_ref, next_slot_ref, x_ref, rp_ref, g_ref, ys_ref, o_ref, y_ref, sem,
                    *, tq, normalize):
    i = pl.program_id(0)
    cur = i % 2

    def gather(slots, buf):
        def row_copy(g, u, k):
            src = pl.multiple_of(slots[0, k, g, u] * ROW_TILE, ROW_TILE)
            dst = pl.multiple_of(g * (ISSUE_GROUP * ROW_TILE), ISSUE_GROUP * ROW_TILE) + u * ROW_TILE
            return pltpu.make_async_copy(ys_ref.at[pl.ds(src, ROW_TILE), :],
                                         y_ref.at[buf, k, pl.ds(dst, ROW_TILE), :], sem.at[buf, k])

        def issue(g, carry):
            for u in range(ISSUE_GROUP):
                row_copy(g, u, 0).start(priority=0)
                row_copy(g, u, 1).start(priority=1)
            return carry

        lax.fori_loop(0, tq // ISSUE_GROUP, issue, 0)

    @pl.when(i == 0)
    def _():
        gather(slot_ref, 0)

    @pl.when(i + 1 < pl.num_programs(0))
    def _():
        gather(next_slot_ref, 1 - cur)

    for k in range(TOP_K):
        pltpu.make_async_copy(ys_ref.at[pl.ds(0, tq * ROW_TILE), :], y_ref.at[cur, k], sem.at[cur, k]).wait()
    rp = rp_ref[...]
    y = [jnp.concatenate([y_ref[cur, k, pl.ds(cb, tq, stride=ROW_TILE), :]
                          for cb in range(x_ref.shape[1] // LANES)], axis=1) for k in range(TOP_K)]
    xn = x_ref[...] + rp[:, 0:1] * y[0] + rp[:, 1:2] * y[1]
    o_ref[...] = _rms(xn, g_ref[...]) if normalize else xn


def _moe_combine(x2, route_p, slots, ys, g, tq, normalize):
    t, d = x2.shape
    slot_block = (1, TOP_K, tq // ISSUE_GROUP, ISSUE_GROUP)
    last = t // tq - 1
    return pl.pallas_call(
        functools.partial(_combine_kernel, tq=tq, normalize=normalize),
        grid=(t // tq,),
        in_specs=[
            pl.BlockSpec(slot_block, lambda i: (i, 0, 0, 0), memory_space=pltpu.SMEM),
            pl.BlockSpec(slot_block, lambda i: (jnp.minimum(i + 1, last), 0, 0, 0), memory_space=pltpu.SMEM),
            pl.BlockSpec((tq, d), lambda i: (i, 0)),
            pl.BlockSpec((tq, LANES), lambda i: (i, 0)),
            pl.BlockSpec((1, d), lambda i: (0, 0)),
            pl.BlockSpec(memory_space=pl.ANY),
        ],
        out_specs=pl.BlockSpec((tq, d), lambda i: (i, 0)),
        out_shape=jax.ShapeDtypeStruct((t, d), F32),
        scratch_shapes=[pltpu.VMEM((2, TOP_K, tq * ROW_TILE, LANES), F32), pltpu.SemaphoreType.DMA((2, TOP_K))],
        compiler_params=_params(("arbitrary",)),
        name="moe_combine",
    )(slots, slots, x2, route_p, g, ys)


def _final_norm_kernel(x_ref, g_ref, o_ref):
    o_ref[...] = _rms(x_ref[...], g_ref[...])


def _final_norm(x2, g, tq):
    t, d = x2.shape
    return pl.pallas_call(
        _final_norm_kernel,
        grid=(t // tq,),
        in_specs=[pl.BlockSpec((tq, d), lambda i: (i, 0)), pl.BlockSpec((1, d), lambda i: (0, 0))],
        out_specs=pl.BlockSpec((tq, d), lambda i: (i, 0)),
        out_shape=jax.ShapeDtypeStruct((t, d), F32),
        compiler_params=_params(("arbitrary",)),
        name="final_norm",
    )(x2, g)


def _rope_lane_tables(positions):
    inv_freq = ROPE_THETA ** (-jnp.arange(0, ROT_DIM, 2, dtype=F32) / ROT_DIM)
    ang = positions.astype(F32).reshape(-1, 1) * inv_freq
    cos, sin = jnp.cos(ang), jnp.sin(ang)
    t = ang.shape[0]
    rest = HEAD_DIM - ROT_DIM
    cos_h = jnp.concatenate([cos, cos, jnp.ones((t, rest), F32)], axis=-1)
    sin_h = jnp.concatenate([sin, sin, jnp.zeros((t, rest), F32)], axis=-1)
    rep = LANES // HEAD_DIM
    return jnp.tile(cos_h, (1, rep)), jnp.tile(sin_h, (1, rep))


def _moe_plan(route_i, counts, t, tm):
    cnt = counts[0, :N_EXPERTS].astype(jnp.int32)
    padded = ((cnt + tm - 1) // tm) * tm
    ends = jnp.cumsum(padded)
    offs = ends - padded
    slot = offs[route_i[:, 0:TOP_K]] + route_i[:, TOP_K:2 * TOP_K]
    ntile = (TOP_K * t) // tm + N_EXPERTS
    starts = jnp.arange(ntile, dtype=jnp.int32) * tm
    valid = (starts < ends[-1]).astype(jnp.int32)
    last = jnp.maximum(ends[-1] // tm - 1, 0)
    src = jnp.minimum(jnp.arange(ntile, dtype=jnp.int32), last)
    expert = jnp.sum(((src * tm)[:, None] >= ends[None, :]).astype(jnp.int32), axis=1)
    expert = jnp.minimum(expert, N_EXPERTS - 1)
    last_tile = jnp.where(padded > 0, ends // tm - 1, ntile - 1).astype(jnp.int32)
    tail = jnp.arange((TOP_K * t) // tm, ntile, dtype=jnp.int32)
    return slot, expert, valid, src, jnp.concatenate([last_tile, tail]), ntile * tm


def kernel(x, positions, norm1_g, w_in, b_gate, diff_lambda, diff_subln_g, pool_w, pool_scale,
           w_proj_a, w_proj_b, w_proj_c, w_out, norm2_g, ffn_w_gate, ffn_w_up, ffn_w_down,
           moe_router, moe_w_gate, moe_w_up, moe_w_down, final_norm_g):
    b, s, d = x.shape
    t = b * s
    depth = w_in.shape[0]
    tq = min(512, s)
    tm = min(512, s)
    ff = ffn_w_gate.shape[-1]
    cf = ff // 2 if ff % 512 == 0 and ff >= 1024 else ff
    x2 = x.reshape(t, d)
    cos_t, sin_t = _rope_lane_tables(positions)
    out = None
    for l in range(depth):
        w_l = w_in[l]
        qkv = _qkv_proj(x2, norm1_g[l].reshape(1, d), w_l[:, :OFF_C].astype(BF16),
                        cos_t, sin_t, tq)
        qa, ka, va = qkv[0:3], qkv[3:6], qkv[6:9]
        qb, kb, vb = qkv[9:12]
        oa, lse = [], []
        for g, (window, dil) in enumerate(DILATED_PATTERNS):
            assert window // dil == BLOCK
            o_g, lse_g = _dilated_attn(qa[g], ka[g], va[g], b, s, dil, 512)
            oa.append(o_g)
            lse.append(lse_g)
        lambda_init = 0.8 - 0.6 * math.exp(-0.3 * l)
        tq_b = min(1024, s)
        ob = _diff_attn(qb, kb, vb, diff_lambda[l], diff_subln_g[l].reshape(1, B_HEAD_WIDTH),
                        b, s, tq_b, min(512, tq_b // 2), min(256, tq_b // 2), 1, lambda_init)

        dense = l % 2 == 0
        router = None
        if not dense:
            wr = jnp.zeros((d, LANES), F32).at[:, :N_EXPERTS].set(moe_router[l // 2])
            wr_hi = wr.astype(BF16)
            router = jnp.concatenate([wr_hi, (wr - wr_hi.astype(F32)).astype(BF16)], axis=1)
        merged = _merge(
            x2, norm1_g[l].reshape(1, d), w_l[:, OFF_C:OFF_G].astype(BF16), w_l[:, OFF_G:].astype(BF16),
            b_gate[l].reshape(1, N_BRANCH * d), oa, lse, ob, pool_w[l].astype(BF16),
            pool_scale[l].reshape(1, C_WIDTH), w_proj_a[l].astype(BF16), w_proj_b[l].astype(BF16),
            w_proj_c[l].astype(BF16), w_out[l].astype(BF16), norm2_g[l].reshape(1, d),
            router, s, tq, BF16 if dense else F32)
        if dense:
            xn, h2 = merged
            i = l // 2
            ntile = t // tm
            ident = jnp.arange(ntile, dtype=jnp.int32)
            x2 = _grouped_swiglu(h2, xn, ffn_w_gate[i:i + 1].astype(BF16), ffn_w_up[i:i + 1].astype(BF16),
                                 ffn_w_down[i:i + 1].astype(BF16), jnp.zeros((ntile,), jnp.int32),
                                 jnp.ones((ntile,), jnp.int32), ident, tm, cf)
            out = None
        else:
            xn, h2, route_i, route_p, counts = merged
            i = l // 2
            slot, expert, valid, src, zero_tiles, nslot = _moe_plan(route_i, counts, t, tm)
            tq_d, tq_c = min(2048, t), min(1024, t)
            tiled = lambda n: slot.reshape(t // n, n, TOP_K).transpose(0, 2, 1).reshape(
                t // n, TOP_K, n // ISSUE_GROUP, ISSUE_GROUP)
            xs = _moe_dispatch(h2, tiled(tq_d), zero_tiles, nslot, tq_d, tm)
            ys = _grouped_swiglu(xs, None, moe_w_gate[i].astype(BF16), moe_w_up[i].astype(BF16),
                                 moe_w_down[i].astype(BF16), expert, valid, src, tm, cf)
            last = l == depth - 1
            res = _moe_combine(xn, route_p, tiled(tq_c), ys, final_norm_g.reshape(1, d), tq_c, last)
            if last:
                out = res
            else:
                x2 = res
    if out is None:
        out = _final_norm(x2, final_norm_g.reshape(1, d), tq)
    return out.reshape(b, s, d)
```

```python
import functools
import math

import jax
import jax.numpy as jnp
from jax import lax
from jax.experimental import pallas as pl
from jax.experimental.pallas import tpu as pltpu

F32 = jnp.float32
BF16 = jnp.bfloat16

HEAD_DIM = 64
ROPE_THETA = 500000.0
ROT_DIM = HEAD_DIM // 4
ROT_HALF = ROT_DIM // 2
BLOCK = 128
EPS = 1e-6
NEG = -1e30

DILATED_PATTERNS = ((128, 1), (512, 4), (2048, 16))
A_GROUPS = len(DILATED_PATTERNS)
A_HEADS_PER_GROUP = 4
A_GROUP_WIDTH = A_HEADS_PER_GROUP * HEAD_DIM
A_WIDTH = A_GROUPS * A_GROUP_WIDTH

B_HEADS = 4
B_HEAD_WIDTH = 2 * HEAD_DIM
B_WIDTH = B_HEADS * B_HEAD_WIDTH
B_VEXT = 2 * B_HEAD_WIDTH

POOL_WINDOWS = (2, 4, 8, 16)
C_GROUP_DIM = 128
C_WIDTH = len(POOL_WINDOWS) * C_GROUP_DIM
POOL_HALO = 16

N_BRANCH = 3
N_EXPERTS = 8
TOP_K = 2

OFF_QA = 0
OFF_KA = OFF_QA + A_WIDTH
OFF_VA = OFF_KA + A_WIDTH
OFF_QB = OFF_VA + A_WIDTH
OFF_KB = OFF_QB + B_WIDTH
OFF_VB = OFF_KB + B_WIDTH
OFF_C = OFF_VB + B_WIDTH
OFF_G = OFF_C + C_WIDTH

LANES = 128
MXU_WIDTH = 256
ROW_TILE = 8
ISSUE_GROUP = LANES
V7X_VMEM_BYTES = 64 * 1024 * 1024
VMEM_LIMIT = V7X_VMEM_BYTES - 8 * 1024 * 1024

QK_SCALE = HEAD_DIM ** -0.5
LOG2_E = math.log2(math.e)


def _params(semantics):
    return pltpu.CompilerParams(dimension_semantics=semantics, vmem_limit_bytes=VMEM_LIMIT)


def _rms(x, g):
    return x * lax.rsqrt(jnp.mean(x * x, axis=-1, keepdims=True) + EPS) * g


def _dot(a, b):
    return jnp.dot(a, b, preferred_element_type=F32)


def _dot_nt(a, b):
    return lax.dot_general(a, b, (((1,), (1,)), ((), ())), preferred_element_type=F32)


def _qkv_kernel(x_ref, g_ref, w_ref, cos_ref, sin_ref,
                qa0, qa1, qa2, ka0, ka1, ka2, va0, va1, va2, qb, kb, vb, zs_ref):
    tq = x_ref.shape[0]
    hb = _rms(x_ref[...], g_ref[...]).astype(BF16)
    cos = cos_ref[...]
    sin = sin_ref[...]
    first_half = (lax.broadcasted_iota(jnp.int32, sin.shape, 1) % HEAD_DIM) < ROT_HALF
    sinlo = jnp.where(first_half, -sin, 0.0)
    sinhi = jnp.where(first_half, 0.0, sin)

    def rope(z):
        return (z * cos + pltpu.roll(z, LANES - ROT_HALF, 1) * sinlo
                + pltpu.roll(z, ROT_HALF, 1) * sinhi)

    def project(off, width, out_ref, rotary, scale, dil=1):
        for c in range(0, width, MXU_WIDTH):
            zz = _dot(hb, w_ref[:, off + c:off + c + MXU_WIDTH])
            for p in range(0, MXU_WIDTH, LANES):
                z = zz[:, p:p + LANES]
                if rotary:
                    z = rope(z)
                if scale != 1.0:
                    z = z * scale
                if dil == 1:
                    out_ref[:, c + p:c + p + LANES] = z.astype(out_ref.dtype)
                else:
                    zs_ref[(c + p) // LANES] = z
        if dil > 1:
            for r in range(dil):
                for sl in range(width // LANES):
                    out_ref[:, r * width + sl * LANES:r * width + (sl + 1) * LANES] = (
                        zs_ref[sl, pl.ds(r, tq // dil, stride=dil), :].astype(out_ref.dtype))

    for g, (q_ref, k_ref, v_ref) in enumerate(((qa0, ka0, va0), (qa1, ka1, va1), (qa2, ka2, va2))):
        dil = DILATED_PATTERNS[g][1]
        project(OFF_QA + g * A_GROUP_WIDTH, A_GROUP_WIDTH, q_ref, True, QK_SCALE * LOG2_E, dil)
        project(OFF_KA + g * A_GROUP_WIDTH, A_GROUP_WIDTH, k_ref, True, 1.0, dil)
        project(OFF_VA + g * A_GROUP_WIDTH, A_GROUP_WIDTH, v_ref, False, 1.0, dil)
    project(OFF_QB, B_WIDTH, qb, True, QK_SCALE * LOG2_E)
    project(OFF_KB, B_WIDTH, kb, True, 1.0)
    lane = lax.broadcasted_iota(jnp.int32, (x_ref.shape[0], B_HEAD_WIDTH), 1)
    ones_col = jnp.where(lane == 0, 1.0, 0.0).astype(vb.dtype)
    heads_per_dot = MXU_WIDTH // B_HEAD_WIDTH
    for h0 in range(0, B_HEADS, heads_per_dot):
        zz = _dot(hb, w_ref[:, OFF_VB + h0 * B_HEAD_WIDTH:OFF_VB + (h0 + heads_per_dot) * B_HEAD_WIDTH])
        for hh in range(heads_per_dot):
            h = h0 + hh
            z = zz[:, hh * B_HEAD_WIDTH:(hh + 1) * B_HEAD_WIDTH]
            vb[:, h * B_VEXT:h * B_VEXT + B_HEAD_WIDTH] = z.astype(vb.dtype)
            vb[:, h * B_VEXT + B_HEAD_WIDTH:(h + 1) * B_VEXT] = ones_col


def _qkv_proj(x2, g, w_qkv, cos_t, sin_t, tq):
    t, d = x2.shape
    row = lambda i: (i, 0)
    const = lambda i: (0, 0)
    dils = [dil for _ in range(3) for (_, dil) in DILATED_PATTERNS]
    shapes = [(t // dil, dil * A_GROUP_WIDTH, tq // dil) for dil in dils]
    shapes += [(t, B_WIDTH, tq), (t, B_WIDTH, tq), (t, B_HEADS * B_VEXT, tq)]
    return pl.pallas_call(
        _qkv_kernel,
        grid=(t // tq,),
        in_specs=[
            pl.BlockSpec((tq, d), row),
            pl.BlockSpec((1, d), const),
            pl.BlockSpec(w_qkv.shape, const, pipeline_mode=pl.Buffered(1)),
            pl.BlockSpec((tq, LANES), row),
            pl.BlockSpec((tq, LANES), row),
        ],
        out_specs=[pl.BlockSpec((rows, w), row) for (_, w, rows) in shapes],
        out_shape=[jax.ShapeDtypeStruct((n, w), BF16) for (n, w, _) in shapes],
        scratch_shapes=[pltpu.VMEM((A_GROUP_WIDTH // LANES, tq, LANES), F32)],
        compiler_params=_params(("arbitrary",)),
        name="qkv_proj",
    )(x2, g, w_qkv, cos_t, sin_t)


def _dilated_kernel(q_ref, kp_ref, kc_ref, vp_ref, vc_ref, o_ref, lse_ref, *, nsub):
    n = pl.program_id(2)
    a = lax.broadcasted_iota(jnp.int32, (BLOCK, 2 * BLOCK), 0)
    j = lax.broadcasted_iota(jnp.int32, (BLOCK, 2 * BLOCK), 1)
    band = (j >= a) & (j <= a + BLOCK)
    band_first = band & ((j >= BLOCK) | (n > 0))
    low_k = lax.broadcasted_iota(jnp.int32, (2 * BLOCK, LANES), 1) < HEAD_DIM
    low_o = lax.broadcasted_iota(jnp.int32, (BLOCK, LANES), 1) < HEAD_DIM
    ones_slab = jnp.ones((2 * BLOCK, LANES), BF16)
    for sb in range(nsub):
        rows = slice(sb * BLOCK, (sb + 1) * BLOCK)
        if sb == 0:
            kcat = jnp.concatenate([kp_ref[...], kc_ref[rows, :]], axis=0)
            vcat = jnp.concatenate([vp_ref[...], vc_ref[rows, :]], axis=0)
            mask = band_first
        else:
            kcat = kc_ref[(sb - 1) * BLOCK:(sb + 1) * BLOCK, :]
            vcat = vc_ref[(sb - 1) * BLOCK:(sb + 1) * BLOCK, :]
            mask = band
        q = q_ref[rows, :]
        for hp in range(A_HEADS_PER_GROUP // 2):
            pair = slice(hp * LANES, (hp + 1) * LANES)
            q_pair = q[:, pair]
            k_pair = kcat[:, pair]
            v_ext = jnp.concatenate([vcat[:, pair], ones_slab], axis=1)
            o_half, lse_half = [], []
            for half in range(2):
                k_h = jnp.where(low_k if half == 0 else ~low_k, k_pair, jnp.zeros_like(k_pair))
                s = jnp.where(mask, _dot_nt(q_pair, k_h), NEG)
                m = jnp.max(jnp.maximum(s[:, :LANES], s[:, LANES:]), axis=-1, keepdims=True)
                p = jnp.concatenate([jnp.exp2(s[:, :LANES] - m), jnp.exp2(s[:, LANES:] - m)], axis=1)
                pv = _dot(p.astype(BF16), v_ext)
                l = pv[:, LANES:]
                o_half.append(pv[:, :LANES] / l)
                lse_half.append(m + jnp.log2(l))
            o_ref[rows, pair] = jnp.where(low_o, o_half[0], o_half[1]).astype(o_ref.dtype)
            lse_ref[rows, pair] = jnp.where(low_o, lse_half[0], lse_half[1])


def _dilated_attn(q, k, v, b, s, dil, qrows):
    w = A_GROUP_WIDTH
    l = s // dil
    qrows = min(qrows, l)
    nsub = qrows // BLOCK
    view = lambda t: t.reshape(b, l, dil * w)
    cur = lambda bi, r, n: (bi, n, r)
    prev = lambda bi, r, n: (bi, jnp.maximum(n * nsub - 1, 0), r)
    o, lse = pl.pallas_call(
        functools.partial(_dilated_kernel, nsub=nsub),
        grid=(b, dil, l // qrows),
        in_specs=[
            pl.BlockSpec((None, qrows, w), cur),
            pl.BlockSpec((None, BLOCK, w), prev),
            pl.BlockSpec((None, qrows, w), cur),
            pl.BlockSpec((None, BLOCK, w), prev),
            pl.BlockSpec((None, qrows, w), cur),
        ],
        out_specs=[pl.BlockSpec((None, qrows, w), cur), pl.BlockSpec((None, qrows, w), cur)],
        out_shape=[jax.ShapeDtypeStruct((b, l, dil * w), BF16),
                   jax.ShapeDtypeStruct((b, l, dil * w), F32)],
        compiler_params=_params(("arbitrary", "arbitrary", "arbitrary")),
        name=f"dilated_attn_d{dil}",
    )(view(q), view(k), view(k), view(v), view(v))
    return o.reshape(b * l, dil * w), lse.reshape(b * l, dil * w)


def _diff_kernel(lam_ref, g_ref, q_ref, k_ref, v_ref, o_ref, *scratch, tq, tk, rows, nh, lambda_init):
    i = pl.program_id(2)
    nd = tq // tk
    q = q_ref[...]
    streams = [(mp, rb) for mp in range(2 * nh) for rb in range(tq // rows)]
    acc_all, m_all = scratch[0], scratch[1]
    acc_all[...] = jnp.zeros_like(acc_all)
    m_all[...] = jnp.full_like(m_all, NEG)
    acc_refs = {(mp, rb): acc_all.at[mp, rb] for mp, rb in streams}
    m_refs = {(mp, rb): m_all.at[mp, rb] for mp, rb in streams}
    sa_ref ={(mp, rb): scratch[2].at[mp, rb] for mp, rb in streams}
    sb_ref = {(mp, rb): scratch[3].at[mp, rb] for mp, rb in streams}

    def visible(rb, diag):
        if diag is None:
            return tk
        return max(0, min(tk, (rb + 1) * rows - diag * tk))

    def scores(jc, dst, diag=None, only=None):
        k = k_ref[pl.ds(pl.multiple_of(jc * tk, tk), tk), :]
        for mp, rb in (only or streams):
            if visible(rb, diag) == 0:
                continue
            cols = slice(mp * HEAD_DIM, (mp + 1) * HEAD_DIM)
            rs = slice(rb * rows, (rb + 1) * rows)
            dst[mp, rb][...] = _dot_nt(q[rs, cols], k[:, cols])

    def consume(jc, src, diag=None, only=None):
        start = pl.multiple_of(jc * tk, tk)
        for mp, rb in (only or streams):
            vis = visible(rb, diag)
            if vis == 0:
                continue
            acc_ref, m_ref = acc_refs[mp, rb], m_refs[mp, rb]
            s = src[mp, rb][:, :vis]
            if diag is not None and diag * tk + vis > rb * rows + 1:
                r = lax.broadcasted_iota(jnp.int32, (rows, vis), 0) + rb * rows
                c = lax.broadcasted_iota(jnp.int32, (rows, vis), 1) + diag * tk
                s = jnp.where(c <= r, s, NEG)
            slabs = [s[:, n * LANES:(n + 1) * LANES] for n in range(vis // LANES)]
            smax = slabs[0]
            for sl in slabs[1:]:
                smax = jnp.maximum(smax, sl)
            m_old = m_ref[...]
            m_new = jnp.maximum(m_old, jnp.max(smax, axis=-1, keepdims=True))
            alpha = jnp.exp2(m_old - m_new)
            p = jnp.concatenate([jnp.exp2(sl - m_new) for sl in slabs], axis=1).astype(BF16)
            vcols = slice((mp // 2) * B_VEXT, (mp // 2 + 1) * B_VEXT)
            pv = _dot(p, v_ref[pl.ds(start, vis), vcols])
            acc = acc_ref[...]
            acc_ref[...] = jnp.concatenate(
                [acc[:, n * LANES:(n + 1) * LANES] * alpha for n in range(B_VEXT // LANES)], axis=1) + pv
            m_ref[...] = m_new

    scores(0, sa_ref)

    def pair(t, carry):
        jc = 2 * t
        for st in streams:
            scores(jc + 1, sb_ref, only=[st])
            consume(jc, sa_ref, only=[st])
        for st in streams:
            scores(jc + 2, sa_ref, only=[st])
            consume(jc + 1, sb_ref, only=[st])
        return carry

    lax.fori_loop(0, i * (nd // 2), pair, 0)
    first = nd * i
    bufs = (sa_ref, sb_ref)
    for dg in range(nd):
        for st in streams:
            if dg + 1 < nd:
                scores(first + dg + 1, bufs[(dg + 1) % 2], diag=dg + 1, only=[st])
            consume(first + dg, bufs[dg % 2], diag=dg, only=[st])

    lp = lam_ref[...]
    lam = (jnp.exp(jnp.sum(lp[0:1] * lp[1:2], axis=-1, keepdims=True))
           - jnp.exp(jnp.sum(lp[2:3] * lp[3:4], axis=-1, keepdims=True)) + lambda_init)
    for hd in range(nh):
        for rb in range(tq // rows):
            acc1 = acc_refs[2 * hd, rb][...]
            acc2 = acc_refs[2 * hd + 1, rb][...]
            o1 = acc1[:, :B_HEAD_WIDTH] / acc1[:, B_HEAD_WIDTH:B_HEAD_WIDTH + 1]
            o2 = acc2[:, :B_HEAD_WIDTH] / acc2[:, B_HEAD_WIDTH:B_HEAD_WIDTH + 1]
            o = o1 - lam * o2
            o_ref[rb * rows:(rb + 1) * rows, hd * B_HEAD_WIDTH:(hd + 1) * B_HEAD_WIDTH] = (
                _rms(o, g_ref[...]) * (1.0 - lambda_init)).astype(o_ref.dtype)


def _diff_attn(qb, kb, vb, lam_p, subln_g, b, s, tq, tk, rows, nh, lambda_init):
    hw = nh * B_HEAD_WIDTH
    q3 = qb.reshape(b, s, B_WIDTH)
    k3 = kb.reshape(b, s, B_WIDTH)
    v3 = vb.reshape(b, s, B_HEADS * B_VEXT)
    assert tq % (2 * tk) == 0 and tk % rows == 0 and B_HEADS % nh == 0
    nmap = 2 * nh
    scratch = ([pltpu.VMEM((nmap, tq // rows, rows, B_VEXT), F32), pltpu.VMEM((nmap, tq // rows, rows, LANES), F32)]
               + [pltpu.VMEM((nmap, tq // rows, rows, tk), F32)] * 2)
    out = pl.pallas_call(
        functools.partial(_diff_kernel, tq=tq, tk=tk, rows=rows, nh=nh, lambda_init=lambda_init),
        grid=(b, B_HEADS // nh, s // tq),
        in_specs=[
            pl.BlockSpec(lam_p.shape, lambda bi, h, i: (0, 0)),
            pl.BlockSpec((1, B_HEAD_WIDTH), lambda bi, h, i: (0, 0)),
            pl.BlockSpec((None, tq, hw), lambda bi, h, i: (bi, i, h)),
            pl.BlockSpec((None, s, hw), lambda bi, h, i: (bi, 0, h)),
            pl.BlockSpec((None, s, nh * B_VEXT), lambda bi, h, i: (bi, 0, h)),
        ],
        out_specs=pl.BlockSpec((None, tq, hw), lambda bi, h, i: (bi, i, h)),
        out_shape=jax.ShapeDtypeStruct((b, s, B_WIDTH), BF16),
        scratch_shapes=scratch,
        compiler_params=_params(("arbitrary", "arbitrary", "arbitrary")),
        name="diff_attn",
    )(lam_p, subln_g, q3, k3, v3)
    return out.reshape(b * s, B_WIDTH)


def _merge_kernel(*refs, tq, rows, tiles_per_seq, route):
    (x_ref, g1_ref, wc_ref, wg_ref, bg_ref, oa0, oa1, oa2, ls0, ls1, ls2, ob_ref,
     pw_ref, ps_ref, wpa_ref, wpb_ref, wpc_ref, wo_ref, g2_ref) = refs[:19]
    if route:
        wrc_ref = refs[19]
        xo_ref, h2_ref, ri_ref, rp_ref, cnt_ref, zc_ref, tm_ref, carry_ref = refs[20:]
    else:
        xo_ref, h2_ref, zc_ref, tm_ref = refs[19:]
    step = pl.program_id(0)
    seq_tile = step % tiles_per_seq
    d = x_ref.shape[-1]
    dils = [dil for (_, dil) in DILATED_PATTERNS]
    slabs = A_GROUP_WIDTH // LANES

    @pl.when(seq_tile == 0)
    def _():
        zc_ref[0:POOL_HALO, :] = jnp.zeros((POOL_HALO, C_WIDTH), F32)

    if route:
        @pl.when(step == 0)
        def _():
            carry_ref[...] = jnp.zeros_like(carry_ref)

    for part in range(tq // rows):
        r0 = part * rows
        rs = slice(r0, r0 + rows)
        x = x_ref[rs, :]
        hb = _rms(x, g1_ref[...]).astype(BF16)

        zc_ref[POOL_HALO + r0:POOL_HALO + r0 + rows, :] = _dot(hb, wc_ref[...])
        pos = seq_tile * tq + r0 + lax.broadcasted_iota(jnp.int32, (rows, C_GROUP_DIM), 0)
        pooled = []
        for gi, win in enumerate(POOL_WINDOWS):
            cols = slice(gi * C_GROUP_DIM, (gi + 1) * C_GROUP_DIM)
            tok = zc_ref[POOL_HALO + r0:POOL_HALO + r0 + rows, cols]
            tot = tok
            for back in range(1, win):
                tot = tot + zc_ref[POOL_HALO + r0 - back:POOL_HALO + r0 - back + rows, cols]
            cnt = jnp.minimum(pos + 1, win).astype(F32)
            dmean = tot / cnt - tok
            pooled.append(_dot(dmean.astype(BF16), pw_ref[gi]))
        out_c = (jnp.concatenate(pooled, axis=-1) * ps_ref[...]).astype(BF16)

        def token_major(src_ref, slot, dil):
            if dil == 1:
                return src_ref[rs, :].astype(F32)
            src_rows = slice(r0 // dil, (r0 + rows) // dil)
            for r in range(dil):
                for sl in range(slabs):
                    col = r * A_GROUP_WIDTH + sl * LANES
                    tm_ref[slot * slabs + sl, pl.ds(r0 + r, rows // dil, stride=dil), :] = (
                        src_ref[src_rows, col:col + LANES].astype(F32))
            return jnp.concatenate([tm_ref[slot * slabs + sl, rs, :] for sl in range(slabs)], axis=1)

        l0, l1, l2 = (token_major(ref, gi, dil) for gi, (ref, dil) in enumerate(zip((ls0, ls1, ls2), dils)))
        o0, o1, o2 = (token_major(ref, A_GROUPS + gi, dil)
                      for gi, (ref, dil) in enumerate(zip((oa0, oa1, oa2), dils)))
        lm = jnp.maximum(jnp.maximum(l0, l1), l2)
        e0, e1, e2 = jnp.exp2(l0 - lm), jnp.exp2(l1 - lm), jnp.exp2(l2 - lm)
        out_a = ((e0 * o0 + e1 * o1 + e2 * o2) / (e0 + e1 + e2)).astype(BF16)

        branches = (_dot(out_a, wpa_ref[...]), _dot(ob_ref[rs, :], wpb_ref[...]), _dot(out_c, wpc_ref[...]))
        mixed = None
        for bi, proj in enumerate(branches):
            zg = _dot(hb, wg_ref[:, bi * d:(bi + 1) * d]) + bg_ref[:, bi * d:(bi + 1) * d]
            term = jax.nn.sigmoid(zg) * proj
            mixed = term if mixed is None else mixed + term
        xn = x + _dot(mixed.astype(BF16), wo_ref[...])
        xo_ref[rs, :] = xn
        h2 = _rms(xn, g2_ref[...])
        if route:
            for cb in range(d // LANES):
                h2_ref[pl.ds(r0 * ROW_TILE + cb, rows, stride=ROW_TILE), :] = h2[:, cb * LANES:(cb + 1) * LANES]
        else:
            h2_ref[rs, :] = h2.astype(h2_ref.dtype)

        if route:
            h_hi = h2.astype(BF16)
            h_lo = (h2 - h_hi.astype(F32)).astype(BF16)
            hi_both = _dot(h_hi, wrc_ref[...])
            logits = hi_both[:, :LANES] + (_dot(h_lo, wrc_ref[:, :LANES]) + hi_both[:, LANES:])
            lane = lax.broadcasted_iota(jnp.int32, (rows, LANES), 1)
            lane_f = lane.astype(F32)
            logits = jnp.where(lane < N_EXPERTS, logits, -jnp.inf)
            m1 = jnp.max(logits, axis=-1, keepdims=True)
            i1 = jnp.min(jnp.where(logits == m1, lane_f, float(LANES)), axis=-1, keepdims=True)
            rest = jnp.where(lane_f == i1, -jnp.inf, logits)
            m2 = jnp.max(rest, axis=-1, keepdims=True)
            i2 = jnp.min(jnp.where(rest == m2, lane_f, float(LANES)), axis=-1, keepdims=True)
            e21 = jnp.exp(m2 - m1)
            p1 = 1.0 / (1.0 + e21)
            p2 = e21 * p1
            sel1 = lane_f == i1
            sel2 = lane_f == i2
            chosen = jnp.where(sel1 | sel2, 1.0, 0.0)

            r = lax.broadcasted_iota(jnp.int32, (rows, rows), 0)
            c = lax.broadcasted_iota(jnp.int32, (rows, rows), 1)
            before = _dot(jnp.where(c < r, 1.0, 0.0).astype(BF16), chosen.astype(BF16)) + carry_ref[...]
            rank1 = jnp.sum(jnp.where(sel1, before, 0.0), axis=-1, keepdims=True)
            rank2 = jnp.sum(jnp.where(sel2, before, 0.0), axis=-1, keepdims=True)
            carry_ref[...] = carry_ref[...] + jnp.sum(chosen, axis=0, keepdims=True)
            packed = jnp.where(lane == 0, i1, jnp.where(lane == 1, i2,
                               jnp.where(lane == 2, rank1, jnp.where(lane == 3, rank2, 0.0))))
            ri_ref[rs, :] = packed.astype(jnp.int32)
            rp_ref[rs, :] = jnp.where(lane == 0, p1, jnp.where(lane == 1, p2, 0.0))

    zc_ref[0:POOL_HALO, :] = zc_ref[tq:tq + POOL_HALO, :]
    if route:
        cnt_ref[...] = jnp.broadcast_to(carry_ref[...], cnt_ref.shape)


def _merge(x2, g1, w_c, w_g, b_g, oa, lse, ob, pool_w, pool_scale, wpa, wpb, wpc, wo, g2,
           router, s, tq, h2_dtype):
    t, d = x2.shape
    route = router is not None
    row = lambda i: (i, 0)
    const = lambda i: (0, 0)
    resident = lambda a: pl.BlockSpec(a.shape, lambda i: (0,) * a.ndim, pipeline_mode=pl.Buffered(1))
    args = [x2, g1, w_c, w_g, b_g, *oa, *lse, ob, pool_w, pool_scale, wpa, wpb, wpc, wo, g2]
    in_specs = [pl.BlockSpec((tq, d), row), pl.BlockSpec((1, d), const), resident(w_c), resident(w_g),
                pl.BlockSpec(b_g.shape, const)]
    in_specs += [pl.BlockSpec((tq // dil, dil * A_GROUP_WIDTH), row) for (_, dil) in DILATED_PATTERNS] * 2
    in_specs += [pl.BlockSpec((tq, B_WIDTH), row), resident(pool_w), pl.BlockSpec(pool_scale.shape, const),
                 resident(wpa), resident(wpb), resident(wpc), resident(wo), pl.BlockSpec((1, d), const)]
    if route:
        assert d == ROW_TILE * LANES, "a token row must fill exactly one (8,128) f32 tile"
        h2_spec = pl.BlockSpec((tq * ROW_TILE, LANES), row)
        h2_shape = jax.ShapeDtypeStruct((t * ROW_TILE, LANES), F32)
    else:
        h2_spec = pl.BlockSpec((tq, d), row)
        h2_shape = jax.ShapeDtypeStruct((t, d), h2_dtype)
    out_specs = [pl.BlockSpec((tq, d), row), h2_spec]
    out_shape = [jax.ShapeDtypeStruct((t, d), F32), h2_shape]
    scratch = [pltpu.VMEM((POOL_HALO + tq, C_WIDTH), F32), pltpu.VMEM((2 * A_GROUPS * (A_GROUP_WIDTH // LANES), tq, LANES), F32)]
    if route:
        args += [router]
        in_specs += [resident(router)]
        out_specs += [pl.BlockSpec((tq, LANES), row), pl.BlockSpec((tq, LANES), row),
                      pl.BlockSpec((ROW_TILE, LANES), const)]
        out_shape += [jax.ShapeDtypeStruct((t, LANES), jnp.int32), jax.ShapeDtypeStruct((t, LANES), F32),
                      jax.ShapeDtypeStruct((ROW_TILE, LANES), F32)]
        scratch += [pltpu.VMEM((1, LANES), F32)]
    return pl.pallas_call(
        functools.partial(_merge_kernel, tq=tq, rows=min(256, tq), tiles_per_seq=s // tq, route=route),
        grid=(t // tq,),
        in_specs=in_specs,
        out_specs=out_specs,
        out_shape=out_shape,
        scratch_shapes=scratch,
        compiler_params=_params(("arbitrary",)),
        name="merge_route" if route else "merge",
    )(*args)


def _swiglu_kernel(te_ref, tv_ref, ts_ref, *refs, residual, nsteps):
    if residual:
        x_ref, res_ref, wg_ref, wu_ref, wd_ref, o_ref, acc_ref = refs
    else:
        x_ref, wg_ref, wu_ref, wd_ref, o_ref, acc_ref = refs
    j = pl.program_id(0)
    c = pl.program_id(1)
    tm, d = acc_ref.shape
    nchunk = d // LANES

    @pl.when((tv_ref[j] == 0) & (c == 0))
    def _():
        o_ref[...] = jnp.zeros_like(o_ref)

    def step(first, last):
        if residual:
            xb = x_ref[...].astype(BF16)
        else:
            xb = jnp.concatenate([x_ref[pl.ds(cb, tm, stride=ROW_TILE), :] for cb in range(nchunk)],
                                 axis=1).astype(BF16)
        gate = _dot(xb, wg_ref[...])
        up = _dot(xb, wu_ref[...])
        mid = (gate * jax.nn.sigmoid(gate) * up).astype(BF16)
        part = _dot(mid, wd_ref[...])
        if not first:
            part = acc_ref[...] + part
        if not last:
            acc_ref[...] = part
        elif residual:
            o_ref[...] = res_ref[...] + part
        else:
            for cb in range(nchunk):
                o_ref[pl.ds(cb, tm, stride=ROW_TILE), :] = part[:, cb * LANES:(cb + 1) * LANES]

    valid = tv_ref[j] > 0
    if nsteps == 1:
        pl.when(valid)(lambda: step(True, True))
    else:
        pl.when(valid & (c == 0))(lambda: step(True, False))
        pl.when(valid & (c == nsteps - 1))(lambda: step(False, True))
        if nsteps > 2:
            pl.when(valid & (c > 0) & (c < nsteps - 1))(lambda: step(False, False))


def _grouped_swiglu(xs, res, wg, wu, wd, tile_expert, tile_valid, tile_src, tm, cf):
    d = wg.shape[1]
    ff = wg.shape[-1]
    residual = res is not None
    xmap = lambda j, c, te, tv, ts: (ts[j], 0)
    if residual:
        n = xs.shape[0]
        row_block = (tm, d)
        in_specs = [pl.BlockSpec(row_block, xmap), pl.BlockSpec(row_block, xmap)]
        args = [xs, res]
    else:
        n = xs.shape[0] // ROW_TILE
        row_block = (tm * ROW_TILE, LANES)
        in_specs = [pl.BlockSpec(row_block, xmap)]
        args = [xs]
    in_specs += [
        pl.BlockSpec((None, d, cf), lambda j, c, te, tv, ts: (te[j], 0, c * tv[j])),
        pl.BlockSpec((None, d, cf), lambda j, c, te, tv, ts: (te[j], 0, c * tv[j])),
        pl.BlockSpec((None, cf, d), lambda j, c, te, tv, ts: (te[j], c * tv[j], 0)),
    ]
    args += [wg, wu, wd]
    return pl.pallas_call(
        functools.partial(_swiglu_kernel, residual=residual, nsteps=ff // cf),
        grid_spec=pltpu.PrefetchScalarGridSpec(
            num_scalar_prefetch=3,
            grid=(n // tm, ff // cf),
            in_specs=in_specs,
            out_specs=pl.BlockSpec(row_block, lambda j, c, te, tv, ts: (j, 0)),
            scratch_shapes=[pltpu.VMEM((tm, d), F32)],
        ),
        out_shape=jax.ShapeDtypeStruct(xs.shape, F32),
        compiler_params=_params(("arbitrary", "arbitrary")),
        name="grouped_swiglu_res" if residual else "grouped_swiglu",
    )(tile_expert, tile_valid, tile_src, *args)


def _dispatch_kernel(zt_ref, slot_ref, h_ref, xs_ref, zero_ref, sem, zsem, *, tq, tm):
    @pl.when(pl.program_id(0) == 0)
    def _():
        zero_ref[...] = jnp.zeros_like(zero_ref)
        for z in range(zt_ref.shape[0]):
            start = pl.multiple_of(zt_ref[z] * (tm * ROW_TILE), tm * ROW_TILE)
            fill = pltpu.make_async_copy(zero_ref, xs_ref.at[pl.ds(start, tm * ROW_TILE), :], zsem.at[0])
            fill.start()
            fill.wait()

    def row_copy(g, u, k):
        src = pl.multiple_of(g * (ISSUE_GROUP * ROW_TILE), ISSUE_GROUP * ROW_TILE) + u * ROW_TILE
        dst = pl.multiple_of(slot_ref[0, k, g, u] * ROW_TILE, ROW_TILE)
        return pltpu.make_async_copy(h_ref.at[pl.ds(src, ROW_TILE), :],
                                     xs_ref.at[pl.ds(dst, ROW_TILE), :], sem.at[k])

    def issue(g, carry):
        for u in range(ISSUE_GROUP):
            row_copy(g, u, 0).start(priority=0)
            row_copy(g, u, 1).start(priority=1)
        return carry

    lax.fori_loop(0, tq // ISSUE_GROUP, issue, 0)
    for k in range(TOP_K):
        pltpu.make_async_copy(h_ref, xs_ref.at[pl.ds(0, tq * ROW_TILE), :], sem.at[k]).wait()


def _moe_dispatch(h2, slots, zero_tiles, nslot, tq, tm):
    d = LANES
    return pl.pallas_call(
        functools.partial(_dispatch_kernel, tq=tq, tm=tm),
        grid_spec=pltpu.PrefetchScalarGridSpec(
            num_scalar_prefetch=1,
            grid=(h2.shape[0] // (tq * ROW_TILE),),
            in_specs=[
                pl.BlockSpec((1, TOP_K, tq // ISSUE_GROUP, ISSUE_GROUP), lambda i, zt: (i, 0, 0, 0),
                             memory_space=pltpu.SMEM),
                pl.BlockSpec((tq * ROW_TILE, d), lambda i, zt: (i, 0)),
            ],
            out_specs=pl.BlockSpec(memory_space=pl.ANY),
            scratch_shapes=[pltpu.VMEM((tm * ROW_TILE, d), h2.dtype), pltpu.SemaphoreType.DMA((TOP_K,)),
                            pltpu.SemaphoreType.DMA((1,))],
        ),
        out_shape=jax.ShapeDtypeStruct((nslot * ROW_TILE, d), h2.dtype),
        compiler_params=_params(("arbitrary",)),
        name="moe_dispatch",
    )(zero_tiles, slots, h2)


def _combine_kernel(slot_ref, next_slot_ref, x_ref, rp_ref, g_ref, ys_ref, o_ref, y_ref, sem,
                    *, tq, normalize):
    i = pl.program_id(0)
    cur = i % 2

    def gather(slots, buf):
        def row_copy(g, u, k):
            src = pl.multiple_of(slots[0, k, g, u] * ROW_TILE, ROW_TILE)
            dst = pl.multiple_of(g * (ISSUE_GROUP * ROW_TILE), ISSUE_GROUP * ROW_TILE) + u * ROW_TILE
            return pltpu.make_async_copy(ys_ref.at[pl.ds(src, ROW_TILE), :],
                                         y_ref.at[buf, k, pl.ds(dst, ROW_TILE), :], sem.at[buf, k])

        def issue(g, carry):
            for u in range(ISSUE_GROUP):
                row_copy(g, u, 0).start(priority=0)
                row_copy(g, u, 1).start(priority=1)
            return carry

        lax.fori_loop(0, tq // ISSUE_GROUP, issue, 0)

    @pl.when(i == 0)
    def _():
        gather(slot_ref, 0)

    @pl.when(i + 1 < pl.num_programs(0))
    def _():
        gather(next_slot_ref, 1 - cur)

    for k in range(TOP_K):
        pltpu.make_async_copy(ys_ref.at[pl.ds(0, tq * ROW_TILE), :], y_ref.at[cur, k], sem.at[cur, k]).wait()
    rp = rp_ref[...]
    y = [jnp.concatenate([y_ref[cur, k, pl.ds(cb, tq, stride=ROW_TILE), :]
                          for cb in range(x_ref.shape[1] // LANES)], axis=1) for k in range(TOP_K)]
    xn = x_ref[...] + rp[:, 0:1] * y[0] + rp[:, 1:2] * y[1]
    o_ref[...] = _rms(xn, g_ref[...]) if normalize else xn


def _moe_combine(x2, route_p, slots, ys, g, tq, normalize):
    t, d = x2.shape
    slot_block = (1, TOP_K, tq // ISSUE_GROUP, ISSUE_GROUP)
    last = t // tq - 1
    return pl.pallas_call(
        functools.partial(_combine_kernel, tq=tq, normalize=normalize),
        grid=(t // tq,),
        in_specs=[
            pl.BlockSpec(slot_block, lambda i: (i, 0, 0, 0), memory_space=pltpu.SMEM),
            pl.BlockSpec(slot_block, lambda i: (jnp.minimum(i + 1, last), 0, 0, 0), memory_space=pltpu.SMEM),
            pl.BlockSpec((tq, d), lambda i: (i, 0)),
            pl.BlockSpec((tq, LANES), lambda i: (i, 0)),
            pl.BlockSpec((1, d), lambda i: (0, 0)),
            pl.BlockSpec(memory_space=pl.ANY),
        ],
        out_specs=pl.BlockSpec((tq, d), lambda i: (i, 0)),
        out_shape=jax.ShapeDtypeStruct((t, d), F32),
        scratch_shapes=[pltpu.VMEM((2, TOP_K, tq * ROW_TILE, LANES), F32), pltpu.SemaphoreType.DMA((2, TOP_K))],
        compiler_params=_params(("arbitrary",)),
        name="moe_combine",
    )(slots, slots, x2, route_p, g, ys)


def _final_norm_kernel(x_ref, g_ref, o_ref):
    o_ref[...] = _rms(x_ref[...], g_ref[...])


def _final_norm(x2, g, tq):
    t, d = x2.shape
    return pl.pallas_call(
        _final_norm_kernel,
        grid=(t // tq,),
        in_specs=[pl.BlockSpec((tq, d), lambda i: (i, 0)), pl.BlockSpec((1, d), lambda i: (0, 0))],
        out_specs=pl.BlockSpec((tq, d), lambda i: (i, 0)),
        out_shape=jax.ShapeDtypeStruct((t, d), F32),
        compiler_params=_params(("arbitrary",)),
        name="final_norm",
    )(x2, g)


def _rope_lane_tables(positions):
    inv_freq = ROPE_THETA ** (-jnp.arange(0, ROT_DIM, 2, dtype=F32) / ROT_DIM)
    ang = positions.astype(F32).reshape(-1, 1) * inv_freq
    cos, sin = jnp.cos(ang), jnp.sin(ang)
    t = ang.shape[0]
    rest = HEAD_DIM - ROT_DIM
    cos_h = jnp.concatenate([cos, cos, jnp.ones((t, rest), F32)], axis=-1)
    sin_h = jnp.concatenate([sin, sin, jnp.zeros((t, rest), F32)], axis=-1)
    rep = LANES // HEAD_DIM
    return jnp.tile(cos_h, (1, rep)), jnp.tile(sin_h, (1, rep))


def _moe_plan(route_i, counts, t, tm):
    cnt = counts[0, :N_EXPERTS].astype(jnp.int32)
    padded = ((cnt + tm - 1) // tm) * tm
    ends = jnp.cumsum(padded)
    offs = ends - padded
    slot = offs[route_i[:, 0:TOP_K]] + route_i[:, TOP_K:2 * TOP_K]
    ntile = (TOP_K * t) // tm + N_EXPERTS
    starts = jnp.arange(ntile, dtype=jnp.int32) * tm
    valid = (starts < ends[-1]).astype(jnp.int32)
    last = jnp.maximum(ends[-1] // tm - 1, 0)
    src = jnp.minimum(jnp.arange(ntile, dtype=jnp.int32), last)
    expert = jnp.sum(((src * tm)[:, None] >= ends[None, :]).astype(jnp.int32), axis=1)
    expert = jnp.minimum(expert, N_EXPERTS - 1)
    last_tile = jnp.where(padded > 0, ends // tm - 1, ntile - 1).astype(jnp.int32)
    tail = jnp.arange((TOP_K * t) // tm, ntile, dtype=jnp.int32)
    return slot, expert, valid, src, jnp.concatenate([last_tile, tail]), ntile * tm


def kernel(x, positions, norm1_g, w_in, b_gate, diff_lambda, diff_subln_g, pool_w, pool_scale,
           w_proj_a, w_proj_b, w_proj_c, w_out, norm2_g, ffn_w_gate, ffn_w_up, ffn_w_down,
           moe_router, moe_w_gate, moe_w_up, moe_w_down, final_norm_g):
    b, s, d = x.shape
    t = b * s
    depth = w_in.shape[0]
    tq = min(512, s)
    tm = min(512, s)
    ff = ffn_w_gate.shape[-1]
    cf = ff // 2 if ff % 512 == 0 and ff >= 1024 else ff
    x2 = x.reshape(t, d)
    cos_t, sin_t = _rope_lane_tables(positions)
    out = None
    for l in range(depth):
        w_l = w_in[l]
        qkv = _qkv_proj(x2, norm1_g[l].reshape(1, d), w_l[:, :OFF_C].astype(BF16),
                        cos_t, sin_t, tq)
        qa, ka, va = qkv[0:3], qkv[3:6], qkv[6:9]
        qb, kb, vb = qkv[9:12]
        oa, lse = [], []
        for g, (window, dil) in enumerate(DILATED_PATTERNS):
            assert window // dil == BLOCK
            o_g, lse_g = _dilated_attn(qa[g], ka[g], va[g], b, s, dil, 512)
            oa.append(o_g)
            lse.append(lse_g)
        lambda_init = 0.8 - 0.6 * math.exp(-0.3 * l)
        tq_b = min(1024, s)
        ob = _diff_attn(qb, kb, vb, diff_lambda[l], diff_subln_g[l].reshape(1, B_HEAD_WIDTH),
                        b, s, tq_b, min(512, tq_b // 2), min(256, tq_b // 2), 1, lambda_init)

        dense = l % 2 == 0
        router = None
        if not dense:
            wr = jnp.zeros((d, LANES), F32).at[:, :N_EXPERTS].set(moe_router[l // 2])
            wr_hi = wr.astype(BF16)
            router = jnp.concatenate([wr_hi, (wr - wr_hi.astype(F32)).astype(BF16)], axis=1)
        merged = _merge(
            x2, norm1_g[l].reshape(1, d), w_l[:, OFF_C:OFF_G].astype(BF16), w_l[:, OFF_G:].astype(BF16),
            b_gate[l].reshape(1, N_BRANCH * d), oa, lse, ob, pool_w[l].astype(BF16),
            pool_scale[l].reshape(1, C_WIDTH), w_proj_a[l].astype(BF16), w_proj_b[l].astype(BF16),
            w_proj_c[l].astype(BF16), w_out[l].astype(BF16), norm2_g[l].reshape(1, d),
            router, s, tq, BF16 if dense else F32)
        if dense:
            xn, h2 = merged
            i = l // 2
            ntile = t // tm
            ident = jnp.arange(ntile, dtype=jnp.int32)
            x2 = _grouped_swiglu(h2, xn, ffn_w_gate[i:i + 1].astype(BF16), ffn_w_up[i:i + 1].astype(BF16),
                                 ffn_w_down[i:i + 1].astype(BF16), jnp.zeros((ntile,), jnp.int32),
                                 jnp.ones((ntile,), jnp.int32), ident, tm, cf)
            out = None
        else:
            xn, h2, route_i, route_p, counts = merged
            i = l // 2
            slot, expert, valid, src, zero_tiles, nslot = _moe_plan(route_i, counts, t, tm)
            tq_d, tq_c = min(2048, t), min(1024, t)
            tiled = lambda n: slot.reshape(t // n, n, TOP_K).transpose(0, 2, 1).reshape(
                t // n, TOP_K, n // ISSUE_GROUP, ISSUE_GROUP)
            xs = _moe_dispatch(h2, tiled(tq_d), zero_tiles, nslot, tq_d, tm)
            ys = _grouped_swiglu(xs, None, moe_w_gate[i].astype(BF16), moe_w_up[i].astype(BF16),
                                 moe_w_down[i].astype(BF16), expert, valid, src, tm, cf)
            last = l == depth - 1
            res = _moe_combine(xn, route_p, tiled(tq_c), ys, final_norm_g.reshape(1, d), tq_c, last)
            if last:
                out = res
            else:
                x2 = res
    if out is None:
        out = _final_norm(x2, final_norm_g.reshape(1, d), tq)
    return out.reshape(b, s, d)
```

```python
import functools
import math

import jax
import jax.numpy as jnp
from jax import lax
from jax.experimental import pallas as pl
from jax.experimental.pallas import tpu as pltpu

F32 = jnp.float32
BF16 = jnp.bfloat16

HEAD_DIM = 64
ROPE_THETA = 500000.0
ROT_DIM = HEAD_DIM // 4
ROT_HALF = ROT_DIM // 2
BLOCK = 128
EPS = 1e-6
NEG = -1e30

DILATED_PATTERNS = ((128, 1), (512, 4), (2048, 16))
A_GROUPS = len(DILATED_PATTERNS)
A_HEADS_PER_GROUP = 4
A_GROUP_WIDTH = A_HEADS_PER_GROUP * HEAD_DIM
A_WIDTH = A_GROUPS * A_GROUP_WIDTH

B_HEADS = 4
B_HEAD_WIDTH = 2 * HEAD_DIM
B_WIDTH = B_HEADS * B_HEAD_WIDTH
B_VEXT = 2 * B_HEAD_WIDTH

POOL_WINDOWS = (2, 4, 8, 16)
C_GROUP_DIM = 128
C_WIDTH = len(POOL_WINDOWS) * C_GROUP_DIM
POOL_HALO = 16

N_BRANCH = 3
N_EXPERTS = 8
TOP_K = 2

OFF_QA = 0
OFF_KA = OFF_QA + A_WIDTH
OFF_VA = OFF_KA + A_WIDTH
OFF_QB = OFF_VA + A_WIDTH
OFF_KB = OFF_QB + B_WIDTH
OFF_VB = OFF_KB + B_WIDTH
OFF_C = OFF_VB + B_WIDTH
OFF_G = OFF_C + C_WIDTH

LANES = 128
MXU_WIDTH = 256
ROW_TILE = 8
ISSUE_GROUP = LANES
V7X_VMEM_BYTES = 64 * 1024 * 1024
VMEM_LIMIT = V7X_VMEM_BYTES - 8 * 1024 * 1024

QK_SCALE = HEAD_DIM ** -0.5
LOG2_E = math.log2(math.e)


def _params(semantics):
    return pltpu.CompilerParams(dimension_semantics=semantics, vmem_limit_bytes=VMEM_LIMIT)


def _rms(x, g):
    return x * lax.rsqrt(jnp.mean(x * x, axis=-1, keepdims=True) + EPS) * g


def _dot(a, b):
    return jnp.dot(a, b, preferred_element_type=F32)


def _dot_nt(a, b):
    return lax.dot_general(a, b, (((1,), (1,)), ((), ())), preferred_element_type=F32)


def _qkv_kernel(x_ref, g_ref, w_ref, cos_ref, sin_ref,
                qa0, qa1, qa2, ka0, ka1, ka2, va0, va1, va2, qb, kb, vb, zs_ref):
    tq = x_ref.shape[0]
    hb = _rms(x_ref[...], g_ref[...]).astype(BF16)
    cos = cos_ref[...]
    sin = sin_ref[...]
    first_half = (lax.broadcasted_iota(jnp.int32, sin.shape, 1) % HEAD_DIM) < ROT_HALF
    sinlo = jnp.where(first_half, -sin, 0.0)
    sinhi = jnp.where(first_half, 0.0, sin)

    def rope(z):
        return (z * cos + pltpu.roll(z, LANES - ROT_HALF, 1) * sinlo
                + pltpu.roll(z, ROT_HALF, 1) * sinhi)

    def project(off, width, out_ref, rotary, scale, dil=1):
        for c in range(0, width, MXU_WIDTH):
            zz = _dot(hb, w_ref[:, off + c:off + c + MXU_WIDTH])
            for p in range(0, MXU_WIDTH, LANES):
                z = zz[:, p:p + LANES]
                if rotary:
                    z = rope(z)
                if scale != 1.0:
                    z = z * scale
                if dil == 1:
                    out_ref[:, c + p:c + p + LANES] = z.astype(out_ref.dtype)
                else:
                    zs_ref[(c + p) // LANES] = z
        if dil > 1:
            for r in range(dil):
                for sl in range(width // LANES):
                    out_ref[:, r * width + sl * LANES:r * width + (sl + 1) * LANES] = (
                        zs_ref[sl, pl.ds(r, tq // dil, stride=dil), :].astype(out_ref.dtype))

    for g, (q_ref, k_ref, v_ref) in enumerate(((qa0, ka0, va0), (qa1, ka1, va1), (qa2, ka2, va2))):
        dil = DILATED_PATTERNS[g][1]
        project(OFF_QA + g * A_GROUP_WIDTH, A_GROUP_WIDTH, q_ref, True, QK_SCALE * LOG2_E, dil)
        project(OFF_KA + g * A_GROUP_WIDTH, A_GROUP_WIDTH, k_ref, True, 1.0, dil)
        project(OFF_VA + g * A_GROUP_WIDTH, A_GROUP_WIDTH, v_ref, False, 1.0, dil)
    project(OFF_QB, B_WIDTH, qb, True, QK_SCALE * LOG2_E)
    project(OFF_KB, B_WIDTH, kb, True, 1.0)
    lane = lax.broadcasted_iota(jnp.int32, (x_ref.shape[0], B_HEAD_WIDTH), 1)
    ones_col = jnp.where(lane == 0, 1.0, 0.0).astype(vb.dtype)
    heads_per_dot = MXU_WIDTH // B_HEAD_WIDTH
    for h0 in range(0, B_HEADS, heads_per_dot):
        zz = _dot(hb, w_ref[:, OFF_VB + h0 * B_HEAD_WIDTH:OFF_VB + (h0 + heads_per_dot) * B_HEAD_WIDTH])
        for hh in range(heads_per_dot):
            h = h0 + hh
            z = zz[:, hh * B_HEAD_WIDTH:(hh + 1) * B_HEAD_WIDTH]
            vb[:, h * B_VEXT:h * B_VEXT + B_HEAD_WIDTH] = z.astype(vb.dtype)
            vb[:, h * B_VEXT + B_HEAD_WIDTH:(h + 1) * B_VEXT] = ones_col


def _qkv_proj(x2, g, w_qkv, cos_t, sin_t, tq):
    t, d = x2.shape
    row = lambda i: (i, 0)
    const = lambda i: (0, 0)
    dils = [dil for _ in range(3) for (_, dil) in DILATED_PATTERNS]
    shapes = [(t // dil, dil * A_GROUP_WIDTH, tq // dil) for dil in dils]
    shapes += [(t, B_WIDTH, tq), (t, B_WIDTH, tq), (t, B_HEADS * B_VEXT, tq)]
    return pl.pallas_call(
        _qkv_kernel,
        grid=(t // tq,),
        in_specs=[
            pl.BlockSpec((tq, d), row),
            pl.BlockSpec((1, d), const),
            pl.BlockSpec(w_qkv.shape, const, pipeline_mode=pl.Buffered(1)),
            pl.BlockSpec((tq, LANES), row),
            pl.BlockSpec((tq, LANES), row),
        ],
        out_specs=[pl.BlockSpec((rows, w), row) for (_, w, rows) in shapes],
        out_shape=[jax.ShapeDtypeStruct((n, w), BF16) for (n, w, _) in shapes],
        scratch_shapes=[pltpu.VMEM((A_GROUP_WIDTH // LANES, tq, LANES), F32)],
        compiler_params=_params(("arbitrary",)),
        name="qkv_proj",
    )(x2, g, w_qkv, cos_t, sin_t)


def _dilated_kernel(q_ref, kp_ref, kc_ref, vp_ref, vc_ref, o_ref, lse_ref, *, nsub):
    n = pl.program_id(2)
    a = lax.broadcasted_iota(jnp.int32, (BLOCK, 2 * BLOCK), 0)
    j = lax.broadcasted_iota(jnp.int32, (BLOCK, 2 * BLOCK), 1)
    band = (j >= a) & (j <= a + BLOCK)
    band_first = band & ((j >= BLOCK) | (n > 0))
    low_k = lax.broadcasted_iota(jnp.int32, (2 * BLOCK, LANES), 1) < HEAD_DIM
    low_o = lax.broadcasted_iota(jnp.int32, (BLOCK, LANES), 1) < HEAD_DIM
    ones_slab = jnp.ones((2 * BLOCK, LANES), BF16)
    for sb in range(nsub):
        rows = slice(sb * BLOCK, (sb + 1) * BLOCK)
        if sb == 0:
            kcat = jnp.concatenate([kp_ref[...], kc_ref[rows, :]], axis=0)
            vcat = jnp.concatenate([vp_ref[...], vc_ref[rows, :]], axis=0)
            mask = band_first
        else:
            kcat = kc_ref[(sb - 1) * BLOCK:(sb + 1) * BLOCK, :]
            vcat = vc_ref[(sb - 1) * BLOCK:(sb + 1) * BLOCK, :]
            mask = band
        q = q_ref[rows, :]
        for hp in range(A_HEADS_PER_GROUP // 2):
            pair = slice(hp * LANES, (hp + 1) * LANES)
            q_pair = q[:, pair]
            k_pair = kcat[:, pair]
            v_ext = jnp.concatenate([vcat[:, pair], ones_slab], axis=1)
            o_half, lse_half = [], []
            for half in range(2):
                k_h = jnp.where(low_k if half == 0 else ~low_k, k_pair, jnp.zeros_like(k_pair))
                s = jnp.where(mask, _dot_nt(q_pair, k_h), NEG)
                m = jnp.max(jnp.maximum(s[:, :LANES], s[:, LANES:]), axis=-1, keepdims=True)
                p = jnp.concatenate([jnp.exp2(s[:, :LANES] - m), jnp.exp2(s[:, LANES:] - m)], axis=1)
                pv = _dot(p.astype(BF16), v_ext)
                l = pv[:, LANES:]
                o_half.append(pv[:, :LANES] / l)
                lse_half.append(m + jnp.log2(l))
            o_ref[rows, pair] = jnp.where(low_o, o_half[0], o_half[1]).astype(o_ref.dtype)
            lse_ref[rows, pair] = jnp.where(low_o, lse_half[0], lse_half[1])


def _dilated_attn(q, k, v, b, s, dil, qrows):
    w = A_GROUP_WIDTH
    l = s // dil
    qrows = min(qrows, l)
    nsub = qrows // BLOCK
    view = lambda t: t.reshape(b, l, dil * w)
    cur = lambda bi, r, n: (bi, n, r)
    prev = lambda bi, r, n: (bi, jnp.maximum(n * nsub - 1, 0), r)
    o, lse = pl.pallas_call(
        functools.partial(_dilated_kernel, nsub=nsub),
        grid=(b, dil, l // qrows),
        in_specs=[
            pl.BlockSpec((None, qrows, w), cur),
            pl.BlockSpec((None, BLOCK, w), prev),
            pl.BlockSpec((None, qrows, w), cur),
            pl.BlockSpec((None, BLOCK, w), prev),
            pl.BlockSpec((None, qrows, w), cur),
        ],
        out_specs=[pl.BlockSpec((None, qrows, w), cur), pl.BlockSpec((None, qrows, w), cur)],
        out_shape=[jax.ShapeDtypeStruct((b, l, dil * w), BF16),
                   jax.ShapeDtypeStruct((b, l, dil * w), F32)],
        compiler_params=_params(("arbitrary", "arbitrary", "arbitrary")),
        name=f"dilated_attn_d{dil}",
    )(view(q), view(k), view(k), view(v), view(v))
    return o.reshape(b * l, dil * w), lse.reshape(b * l, dil * w)


def _diff_kernel(lam_ref, g_ref, q_ref, k_ref, v_ref, o_ref, *scratch, tq, tk, rows, nh, lambda_init):
    i = pl.program_id(2)
    nd = tq // tk
    q = q_ref[...]
    streams = [(mp, rb) for rb in range(tq // rows) for mp in range(2 * nh)]
    acc_all, m_all = scratch[0], scratch[1]
    acc_all[...] = jnp.zeros_like(acc_all)
    m_all[...] = jnp.full_like(m_all, NEG)
    acc_refs = {(mp, rb): acc_all.at[mp, rb] for mp, rb in streams}
    m_refs = {(mp, rb): m_all.at[mp, rb] for mp, rb in streams}
    sa_ref = {(mp, rb): scratch[2].at[mp, rb] for mp, rb in streams}
    sb_ref = {(mp, rb): scratch[3].at[mp, rb] for mp, rb in streams}

    def visible(rb, diag):
        if diag is None:
            return tk
        return max(0, min(tk, (rb + 1) * rows - diag * tk))

    def scores(jc, dst, diag=None):
        k = k_ref[pl.ds(pl.multiple_of(jc * tk, tk), tk), :]
        for mp, rb in streams:
            if visible(rb, diag) == 0:
                continue
            cols = slice(mp * HEAD_DIM, (mp + 1) * HEAD_DIM)
            rs = slice(rb * rows, (rb + 1) * rows)
            dst[mp, rb][...] = _dot_nt(q[rs, cols], k[:, cols])

    def consume(jc, src, diag=None):
        start = pl.multiple_of(jc * tk, tk)
        for mp, rb in streams:
            vis = visible(rb, diag)
            if vis == 0:
                continue
            acc_ref, m_ref = acc_refs[mp, rb], m_refs[mp, rb]
            s = src[mp, rb][:, :vis]
            if diag is not None and diag * tk + vis > rb * rows + 1:
                r = lax.broadcasted_iota(jnp.int32, (rows, vis), 0) + rb * rows
                c = lax.broadcasted_iota(jnp.int32, (rows, vis), 1) + diag * tk
                s = jnp.where(c <= r, s, NEG)
            slabs = [s[:, n * LANES:(n + 1) * LANES] for n in range(vis // LANES)]
            smax = slabs[0]
            for sl in slabs[1:]:
                smax = jnp.maximum(smax, sl)
            m_old = m_ref[...]
            m_new = jnp.maximum(m_old, jnp.max(smax, axis=-1, keepdims=True))
            alpha = jnp.exp2(m_old - m_new)
            p = jnp.concatenate([jnp.exp2(sl - m_new) for sl in slabs], axis=1).astype(BF16)
            vcols = slice((mp // 2) * B_VEXT, (mp // 2 + 1) * B_VEXT)
            pv = _dot(p, v_ref[pl.ds(start, vis), vcols])
            acc = acc_ref[...]
            acc_ref[...] = jnp.concatenate(
                [acc[:, n * LANES:(n + 1) * LANES] * alpha for n in range(B_VEXT // LANES)], axis=1) + pv
            m_ref[...] = m_new

    scores(0, sa_ref)

    def pair(t, carry):
        jc = 2 * t
        scores(jc + 1, sb_ref)
        consume(jc, sa_ref)
        scores(jc + 2, sa_ref)
        consume(jc + 1, sb_ref)
        return carry

    lax.fori_loop(0, i * (nd // 2), pair, 0)
    first = nd * i
    bufs = (sa_ref, sb_ref)
    for dg in range(nd):
        if dg + 1 < nd:
            scores(first + dg + 1, bufs[(dg + 1) % 2], diag=dg + 1)
        consume(first + dg, bufs[dg % 2], diag=dg)

    lp = lam_ref[...]
    lam = (jnp.exp(jnp.sum(lp[0:1] * lp[1:2], axis=-1, keepdims=True))
           - jnp.exp(jnp.sum(lp[2:3] * lp[3:4], axis=-1, keepdims=True)) + lambda_init)
    for hd in range(nh):
        for rb in range(tq // rows):
            acc1 = acc_refs[2 * hd, rb][...]
            acc2 = acc_refs[2 * hd + 1, rb][...]
            o1 = acc1[:, :B_HEAD_WIDTH] / acc1[:, B_HEAD_WIDTH:B_HEAD_WIDTH + 1]
            o2 = acc2[:, :B_HEAD_WIDTH] / acc2[:, B_HEAD_WIDTH:B_HEAD_WIDTH + 1]
            o = o1 - lam * o2
            o_ref[rb * rows:(rb + 1) * rows, hd * B_HEAD_WIDTH:(hd + 1) * B_HEAD_WIDTH] = (
                _rms(o, g_ref[...]) * (1.0 - lambda_init)).astype(o_ref.dtype)


def _diff_attn(qb, kb, vb, lam_p, subln_g, b, s, tq, tk, rows, nh, lambda_init):
    hw = nh * B_HEAD_WIDTH
    q3 = qb.reshape(b, s, B_WIDTH)
    k3 = kb.reshape(b, s, B_WIDTH)
    v3 = vb.reshape(b, s, B_HEADS * B_VEXT)
    assert tq % (2 * tk) == 0 and tk % rows == 0 and B_HEADS % nh == 0
    nmap = 2 * nh
    scratch = ([pltpu.VMEM((nmap, tq // rows, rows, B_VEXT), F32), pltpu.VMEM((nmap, tq // rows, rows, LANES), F32)]
               + [pltpu.VMEM((nmap, tq // rows, rows, tk), F32)] * 2)
    out = pl.pallas_call(
        functools.partial(_diff_kernel, tq=tq, tk=tk, rows=rows, nh=nh, lambda_init=lambda_init),
        grid=(b, B_HEADS // nh, s // tq),
        in_specs=[
            pl.BlockSpec(lam_p.shape, lambda bi, h, i: (0, 0)),
            pl.BlockSpec((1, B_HEAD_WIDTH), lambda bi, h, i: (0, 0)),
            pl.BlockSpec((None, tq, hw), lambda bi, h, i: (bi, i, h)),
            pl.BlockSpec((None, s, hw), lambda bi, h, i: (bi, 0, h)),
            pl.BlockSpec((None, s, nh * B_VEXT), lambda bi, h, i: (bi, 0, h)),
        ],
        out_specs=pl.BlockSpec((None, tq, hw), lambda bi, h, i: (bi, i, h)),
        out_shape=jax.ShapeDtypeStruct((b, s, B_WIDTH), BF16),
        scratch_shapes=scratch,
        compiler_params=_params(("arbitrary", "arbitrary", "arbitrary")),
        name="diff_attn",
    )(lam_p, subln_g, q3, k3, v3)
    return out.reshape(b * s, B_WIDTH)


def _merge_kernel(*refs, tq, rows, tiles_per_seq, route):
    (x_ref, g1_ref, wc_ref, wg_ref, bg_ref, oa0, oa1, oa2, ls0, ls1, ls2, ob_ref,
     pw_ref, ps_ref, wpa_ref, wpb_ref, wpc_ref, wo_ref, g2_ref) = refs[:19]
    if route:
        wrc_ref = refs[19]
        xo_ref, h2_ref, ri_ref, rp_ref, cnt_ref, zc_ref, tm_ref, carry_ref = refs[20:]
    else:
        xo_ref, h2_ref, zc_ref, tm_ref = refs[19:]
    step = pl.program_id(0)
    seq_tile = step % tiles_per_seq
    d = x_ref.shape[-1]
    dils = [dil for (_, dil) in DILATED_PATTERNS]
    slabs = A_GROUP_WIDTH // LANES

    @pl.when(seq_tile == 0)
    def _():
        zc_ref[0:POOL_HALO, :] = jnp.zeros((POOL_HALO, C_WIDTH), F32)

    if route:
        @pl.when(step == 0)
        def _():
            carry_ref[...] = jnp.zeros_like(carry_ref)

    for part in range(tq // rows):
        r0 = part * rows
        rs = slice(r0, r0 + rows)
        x = x_ref[rs, :]
        hb = _rms(x, g1_ref[...]).astype(BF16)

        zc_ref[POOL_HALO + r0:POOL_HALO + r0 + rows, :] = _dot(hb, wc_ref[...])
        pos = seq_tile * tq + r0 + lax.broadcasted_iota(jnp.int32, (rows, C_GROUP_DIM), 0)
        pooled = []
        for gi, win in enumerate(POOL_WINDOWS):
            cols = slice(gi * C_GROUP_DIM, (gi + 1) * C_GROUP_DIM)
            tok = zc_ref[POOL_HALO + r0:POOL_HALO + r0 + rows, cols]
            tot = tok
            for back in range(1, win):
                tot = tot + zc_ref[POOL_HALO + r0 - back:POOL_HALO + r0 - back + rows, cols]
            cnt = jnp.minimum(pos + 1, win).astype(F32)
            dmean = tot / cnt - tok
            pooled.append(_dot(dmean.astype(BF16), pw_ref[gi]))
        out_c = (jnp.concatenate(pooled, axis=-1) * ps_ref[...]).astype(BF16)

        def token_major(src_ref, slot, dil):
            if dil == 1:
                return src_ref[rs, :].astype(F32)
            src_rows = slice(r0 // dil, (r0 + rows) // dil)
            for r in range(dil):
                for sl in range(slabs):
                    col = r * A_GROUP_WIDTH + sl * LANES
                    tm_ref[slot * slabs + sl, pl.ds(r0 + r, rows // dil, stride=dil), :] = (
                        src_ref[src_rows, col:col + LANES].astype(F32))
            return jnp.concatenate([tm_ref[slot * slabs + sl, rs, :] for sl in range(slabs)], axis=1)

        l0, l1, l2 = (token_major(ref, gi, dil) for gi, (ref, dil) in enumerate(zip((ls0, ls1, ls2), dils)))
        o0, o1, o2 = (token_major(ref, A_GROUPS + gi, dil)
                      for gi, (ref, dil) in enumerate(zip((oa0, oa1, oa2), dils)))
        lm = jnp.maximum(jnp.maximum(l0, l1), l2)
        e0, e1, e2 = jnp.exp2(l0 - lm), jnp.exp2(l1 - lm), jnp.exp2(l2 - lm)
        out_a = ((e0 * o0 + e1 * o1 + e2 * o2) / (e0 + e1 + e2)).astype(BF16)

        branches = (_dot(out_a, wpa_ref[...]), _dot(ob_ref[rs, :], wpb_ref[...]), _dot(out_c, wpc_ref[...]))
        mixed = None
        for bi, proj in enumerate(branches):
            zg = _dot(hb, wg_ref[:, bi * d:(bi + 1) * d]) + bg_ref[:, bi * d:(bi + 1) * d]
            term = jax.nn.sigmoid(zg) * proj
            mixed = term if mixed is None else mixed + term
        xn = x + _dot(mixed.astype(BF16), wo_ref[...])
        xo_ref[rs, :] = xn
        h2 = _rms(xn, g2_ref[...])
        if route:
            for cb in range(d // LANES):
                h2_ref[pl.ds(r0 * ROW_TILE + cb, rows, stride=ROW_TILE), :] = h2[:, cb * LANES:(cb + 1) * LANES]
        else:
            h2_ref[rs, :] = h2.astype(h2_ref.dtype)

        if route:
            h_hi = h2.astype(BF16)
            h_lo = (h2 - h_hi.astype(F32)).astype(BF16)
            hi_both = _dot(h_hi, wrc_ref[...])
            logits = hi_both[:, :LANES] + (_dot(h_lo, wrc_ref[:, :LANES]) + hi_both[:, LANES:])
            lane = lax.broadcasted_iota(jnp.int32, (rows, LANES), 1)
            lane_f = lane.astype(F32)
            logits = jnp.where(lane < N_EXPERTS, logits, -jnp.inf)
            m1 = jnp.max(logits, axis=-1, keepdims=True)
            i1 = jnp.min(jnp.where(logits == m1, lane_f, float(LANES)), axis=-1, keepdims=True)
            rest = jnp.where(lane_f == i1, -jnp.inf, logits)
            m2 = jnp.max(rest, axis=-1, keepdims=True)
            i2 = jnp.min(jnp.where(rest == m2, lane_f, float(LANES)), axis=-1, keepdims=True)
            e21 = jnp.exp(m2 - m1)
            p1 = 1.0 / (1.0 + e21)
            p2 = e21 * p1
            sel1 = lane_f == i1
            sel2 = lane_f == i2
            chosen = jnp.where(sel1 | sel2, 1.0, 0.0)

            r = lax.broadcasted_iota(jnp.int32, (rows, rows), 0)
            c = lax.broadcasted_iota(jnp.int32, (rows, rows), 1)
            before = _dot(jnp.where(c < r, 1.0, 0.0).astype(BF16), chosen.astype(BF16)) + carry_ref[...]
            rank1 = jnp.sum(jnp.where(sel1, before, 0.0), axis=-1, keepdims=True)
            rank2 = jnp.sum(jnp.where(sel2, before, 0.0), axis=-1, keepdims=True)
            carry_ref[...] = carry_ref[...] + jnp.sum(chosen, axis=0, keepdims=True)
            packed = jnp.where(lane == 0, i1, jnp.where(lane == 1, i2,
                               jnp.where(lane == 2, rank1, jnp.where(lane == 3, rank2, 0.0))))
            ri_ref[rs, :] = packed.astype(jnp.int32)
            rp_ref[rs, :] = jnp.where(lane == 0, p1, jnp.where(lane == 1, p2, 0.0))

    zc_ref[0:POOL_HALO, :] = zc_ref[tq:tq + POOL_HALO, :]
    if route:
        cnt_ref[...] = jnp.broadcast_to(carry_ref[...], cnt_ref.shape)


def _merge(x2, g1, w_c, w_g, b_g, oa, lse, ob, pool_w, pool_scale, wpa, wpb, wpc, wo, g2,
           router, s, tq, h2_dtype):
    t, d = x2.shape
    route = router is not None
    row = lambda i: (i, 0)
    const = lambda i: (0, 0)
    resident = lambda a: pl.BlockSpec(a.shape, lambda i: (0,) * a.ndim, pipeline_mode=pl.Buffered(1))
    args = [x2, g1, w_c, w_g, b_g, *oa, *lse, ob, pool_w, pool_scale, wpa, wpb, wpc, wo, g2]
    in_specs = [pl.BlockSpec((tq, d), row), pl.BlockSpec((1, d), const), resident(w_c), resident(w_g),
                pl.BlockSpec(b_g.shape, const)]
    in_specs += [pl.BlockSpec((tq // dil, dil * A_GROUP_WIDTH), row) for (_, dil) in DILATED_PATTERNS] * 2
    in_specs += [pl.BlockSpec((tq, B_WIDTH), row), resident(pool_w), pl.BlockSpec(pool_scale.shape, const),
                 resident(wpa), resident(wpb), resident(wpc), resident(wo), pl.BlockSpec((1, d), const)]
    if route:
        assert d == ROW_TILE * LANES, "a token row must fill exactly one (8,128) f32 tile"
        h2_spec = pl.BlockSpec((tq * ROW_TILE, LANES), row)
        h2_shape = jax.ShapeDtypeStruct((t * ROW_TILE, LANES), F32)
    else:
        h2_spec = pl.BlockSpec((tq, d), row)
        h2_shape = jax.ShapeDtypeStruct((t, d), h2_dtype)
    out_specs = [pl.BlockSpec((tq, d), row), h2_spec]
    out_shape = [jax.ShapeDtypeStruct((t, d), F32), h2_shape]
    scratch = [pltpu.VMEM((POOL_HALO + tq, C_WIDTH), F32), pltpu.VMEM((2 * A_GROUPS * (A_GROUP_WIDTH // LANES), tq, LANES), F32)]
    if route:
        args += [router]
        in_specs += [resident(router)]
        out_specs += [pl.BlockSpec((tq, LANES), row), pl.BlockSpec((tq, LANES), row),
                      pl.BlockSpec((8, LANES), const)]
        out_shape += [jax.ShapeDtypeStruct((t, LANES), jnp.int32), jax.ShapeDtypeStruct((t, LANES), F32),
                      jax.ShapeDtypeStruct((8, LANES), F32)]
        scratch += [pltpu.VMEM((1, LANES), F32)]
    return pl.pallas_call(
        functools.partial(_merge_kernel, tq=tq, rows=min(256, tq), tiles_per_seq=s // tq, route=route),
        grid=(t // tq,),
        in_specs=in_specs,
        out_specs=out_specs,
        out_shape=out_shape,
        scratch_shapes=scratch,
        compiler_params=_params(("arbitrary",)),
        name="merge_route" if route else "merge",
    )(*args)


def _swiglu_kernel(te_ref, tv_ref, ts_ref, *refs, residual, nsteps):
    if residual:
        x_ref, res_ref, wg_ref, wu_ref, wd_ref, o_ref, acc_ref = refs
    else:
        x_ref, wg_ref, wu_ref, wd_ref, o_ref, acc_ref = refs
    j = pl.program_id(0)
    c = pl.program_id(1)
    tm, d = acc_ref.shape
    nchunk = d // LANES

    @pl.when((tv_ref[j] == 0) & (c == 0))
    def _():
        o_ref[...] = jnp.zeros_like(o_ref)

    def step(first, last):
        if residual:
            xb = x_ref[...].astype(BF16)
        else:
            xb = jnp.concatenate([x_ref[pl.ds(cb, tm, stride=ROW_TILE), :] for cb in range(nchunk)],
                                 axis=1).astype(BF16)
        gate = _dot(xb, wg_ref[...])
        up = _dot(xb, wu_ref[...])
        mid = (gate * jax.nn.sigmoid(gate) * up).astype(BF16)
        part = _dot(mid, wd_ref[...])
        if not first:
            part = acc_ref[...] + part
        if not last:
            acc_ref[...] = part
        elif residual:
            o_ref[...] = res_ref[...] + part
        else:
            for cb in range(nchunk):
                o_ref[pl.ds(cb, tm, stride=ROW_TILE), :] = part[:, cb * LANES:(cb + 1) * LANES]

    valid = tv_ref[j] > 0
    if nsteps == 1:
        pl.when(valid)(lambda: step(True, True))
    else:
        pl.when(valid & (c == 0))(lambda: step(True, False))
        pl.when(valid & (c == nsteps - 1))(lambda: step(False, True))
        if nsteps > 2:
            pl.when(valid & (c > 0) & (c < nsteps - 1))(lambda: step(False, False))


def _grouped_swiglu(xs, res, wg, wu, wd, tile_expert, tile_valid, tile_src, tm, cf):
    d = wg.shape[1]
    ff = wg.shape[-1]
    residual = res is not None
    xmap = lambda j, c, te, tv, ts: (ts[j], 0)
    if residual:
        n = xs.shape[0]
        row_block = (tm, d)
        in_specs = [pl.BlockSpec(row_block, xmap), pl.BlockSpec(row_block, xmap)]
        args = [xs, res]
    else:
        n = xs.shape[0] // ROW_TILE
        row_block = (tm * ROW_TILE, LANES)
        in_specs = [pl.BlockSpec(row_block, xmap)]
        args = [xs]
    in_specs += [
        pl.BlockSpec((None, d, cf), lambda j, c, te, tv, ts: (te[j], 0, c * tv[j])),
        pl.BlockSpec((None, d, cf), lambda j, c, te, tv, ts: (te[j], 0, c * tv[j])),
        pl.BlockSpec((None, cf, d), lambda j, c, te, tv, ts: (te[j], c * tv[j], 0)),
    ]
    args += [wg, wu, wd]
    return pl.pallas_call(
        functools.partial(_swiglu_kernel, residual=residual, nsteps=ff // cf),
        grid_spec=pltpu.PrefetchScalarGridSpec(
            num_scalar_prefetch=3,
            grid=(n // tm, ff // cf),
            in_specs=in_specs,
            out_specs=pl.BlockSpec(row_block, lambda j, c, te, tv, ts: (j, 0)),
            scratch_shapes=[pltpu.VMEM((tm, d), F32)],
        ),
        out_shape=jax.ShapeDtypeStruct(xs.shape, F32),
        compiler_params=_params(("arbitrary", "arbitrary")),
        name="grouped_swiglu_res" if residual else "grouped_swiglu",
    )(tile_expert, tile_valid, tile_src, *args)


def _dispatch_kernel(zt_ref, slot_ref, h_ref, xs_ref, zero_ref, sem, zsem, *, tq, tm):
    @pl.when(pl.program_id(0) == 0)
    def _():
        zero_ref[...] = jnp.zeros_like(zero_ref)
        for z in range(zt_ref.shape[0]):
            start = pl.multiple_of(zt_ref[z] * (tm * ROW_TILE), tm * ROW_TILE)
            fill = pltpu.make_async_copy(zero_ref, xs_ref.at[pl.ds(start, tm * ROW_TILE), :], zsem.at[0])
            fill.start()
            fill.wait()

    def row_copy(g, u, k):
        src = pl.multiple_of(g * (ISSUE_GROUP * ROW_TILE), ISSUE_GROUP * ROW_TILE) + u * ROW_TILE
        dst = pl.multiple_of(slot_ref[0, k, g, u] * ROW_TILE, ROW_TILE)
        return pltpu.make_async_copy(h_ref.at[pl.ds(src, ROW_TILE), :],
                                     xs_ref.at[pl.ds(dst, ROW_TILE), :], sem.at[k])

    def issue(g, carry):
        for u in range(ISSUE_GROUP):
            row_copy(g, u, 0).start(priority=0)
            row_copy(g, u, 1).start(priority=1)
        return carry

    lax.fori_loop(0, tq // ISSUE_GROUP, issue, 0)
    for k in range(TOP_K):
        pltpu.make_async_copy(h_ref, xs_ref.at[pl.ds(0, tq * ROW_TILE), :], sem.at[k]).wait()


def _moe_dispatch(h2, slots, zero_tiles, nslot, tq, tm):
    d = LANES
    return pl.pallas_call(
        functools.partial(_dispatch_kernel, tq=tq, tm=tm),
        grid_spec=pltpu.PrefetchScalarGridSpec(
            num_scalar_prefetch=1,
            grid=(h2.shape[0] // (tq * ROW_TILE),),
            in_specs=[
                pl.BlockSpec((1, TOP_K, tq // ISSUE_GROUP, ISSUE_GROUP), lambda i, zt: (i, 0, 0, 0),
                             memory_space=pltpu.SMEM),
                pl.BlockSpec((tq * ROW_TILE, d), lambda i, zt: (i, 0)),
            ],
            out_specs=pl.BlockSpec(memory_space=pl.ANY),
            scratch_shapes=[pltpu.VMEM((tm * ROW_TILE, d), h2.dtype), pltpu.SemaphoreType.DMA((TOP_K,)),
                            pltpu.SemaphoreType.DMA((1,))],
        ),
        out_shape=jax.ShapeDtypeStruct((nslot * ROW_TILE, d), h2.dtype),
        compiler_params=_params(("arbitrary",)),
        name="moe_dispatch",
    )(zero_tiles, slots, h2)


def _combine_kernel(slot_ref, next_slot_ref, x_ref, rp_ref, g_ref, ys_ref, o_ref, y_ref, sem,
                    *, tq, normalize):
    i = pl.program_id(0)
    cur = i % 2

    def gather(slots, buf):
        def row_copy(g, u, k):
            src = pl.multiple_of(slots[0, k, g, u] * ROW_TILE, ROW_TILE)
            dst = pl.multiple_of(g * (ISSUE_GROUP * ROW_TILE), ISSUE_GROUP * ROW_TILE) + u * ROW_TILE
            return pltpu.make_async_copy(ys_ref.at[pl.ds(src, ROW_TILE), :],
                                         y_ref.at[buf, k, pl.ds(dst, ROW_TILE), :], sem.at[buf, k])

        def issue(g, carry):
            for u in range(ISSUE_GROUP):
                row_copy(g, u, 0).start(priority=0)
                row_copy(g, u, 1).start(priority=1)
            return carry

        lax.fori_loop(0, tq // ISSUE_GROUP, issue, 0)

    @pl.when(i == 0)
    def _():
        gather(slot_ref, 0)

    @pl.when(i + 1 < pl.num_programs(0))
    def _():
        gather(next_slot_ref, 1 - cur)

    for k in range(TOP_K):
        pltpu.make_async_copy(ys_ref.at[pl.ds(0, tq * ROW_TILE), :], y_ref.at[cur, k], sem.at[cur, k]).wait()
    rp = rp_ref[...]
    y = [jnp.concatenate([y_ref[cur, k, pl.ds(cb, tq, stride=ROW_TILE), :]
                          for cb in range(x_ref.shape[1] // LANES)], axis=1) for k in range(TOP_K)]
    xn = x_ref[...] + rp[:, 0:1] * y[0] + rp[:, 1:2] * y[1]
    o_ref[...] = _rms(xn, g_ref[...]) if normalize else xn


def _moe_combine(x2, route_p, slots, ys, g, tq, normalize):
    t, d = x2.shape
    slot_block = (1, TOP_K, tq // ISSUE_GROUP, ISSUE_GROUP)
    last = t // tq - 1
    return pl.pallas_call(
        functools.partial(_combine_kernel, tq=tq, normalize=normalize),
        grid=(t // tq,),
        in_specs=[
            pl.BlockSpec(slot_block, lambda i: (i, 0, 0, 0), memory_space=pltpu.SMEM),
            pl.BlockSpec(slot_block, lambda i: (jnp.minimum(i + 1, last), 0, 0, 0), memory_space=pltpu.SMEM),
            pl.BlockSpec((tq, d), lambda i: (i, 0)),
            pl.BlockSpec((tq, LANES), lambda i: (i, 0)),
            pl.BlockSpec((1, d), lambda i: (0, 0)),
            pl.BlockSpec(memory_space=pl.ANY),
        ],
        out_specs=pl.BlockSpec((tq, d), lambda i: (i, 0)),
        out_shape=jax.ShapeDtypeStruct((t, d), F32),
        scratch_shapes=[pltpu.VMEM((2, TOP_K, tq * ROW_TILE, LANES), F32), pltpu.SemaphoreType.DMA((2, TOP_K))],
        compiler_params=_params(("arbitrary",)),
        name="moe_combine",
    )(slots, slots, x2, route_p, g, ys)


def _final_norm_kernel(x_ref, g_ref, o_ref):
    o_ref[...] = _rms(x_ref[...], g_ref[...])


def _final_norm(x2, g, tq):
    t, d = x2.shape
    return pl.pallas_call(
        _final_norm_kernel,
        grid=(t // tq,),
        in_specs=[pl.BlockSpec((tq, d), lambda i: (i, 0)), pl.BlockSpec((1, d), lambda i: (0, 0))],
        out_specs=pl.BlockSpec((tq, d), lambda i: (i, 0)),
        out_shape=jax.ShapeDtypeStruct((t, d), F32),
        compiler_params=_params(("arbitrary",)),
        name="final_norm",
    )(x2, g)


def _rope_lane_tables(positions):
    inv_freq = ROPE_THETA ** (-jnp.arange(0, ROT_DIM, 2, dtype=F32) / ROT_DIM)
    ang = positions.astype(F32).reshape(-1, 1) * inv_freq
    cos, sin = jnp.cos(ang), jnp.sin(ang)
    t = ang.shape[0]
    rest = HEAD_DIM - ROT_DIM
    cos_h = jnp.concatenate([cos, cos, jnp.ones((t, rest), F32)], axis=-1)
    sin_h = jnp.concatenate([sin, sin, jnp.zeros((t, rest), F32)], axis=-1)
    rep = LANES // HEAD_DIM
    return jnp.tile(cos_h, (1, rep)), jnp.tile(sin_h, (1, rep))


def _moe_plan(route_i, counts, t, tm):
    cnt = counts[0, :N_EXPERTS].astype(jnp.int32)
    padded = ((cnt + tm - 1) // tm) * tm
    ends = jnp.cumsum(padded)
    offs = ends - padded
    slot = offs[route_i[:, 0:TOP_K]] + route_i[:, TOP_K:2 * TOP_K]
    ntile = (TOP_K * t) // tm + N_EXPERTS
    starts = jnp.arange(ntile, dtype=jnp.int32) * tm
    valid = (starts < ends[-1]).astype(jnp.int32)
    last = jnp.maximum(ends[-1] // tm - 1, 0)
    src = jnp.minimum(jnp.arange(ntile, dtype=jnp.int32), last)
    expert = jnp.sum(((src * tm)[:, None] >= ends[None, :]).astype(jnp.int32), axis=1)
    expert = jnp.minimum(expert, N_EXPERTS - 1)
    last_tile = jnp.where(padded > 0, ends // tm - 1, ntile - 1).astype(jnp.int32)
    tail = jnp.arange((TOP_K * t) // tm, ntile, dtype=jnp.int32)
    return slot, expert, valid, src, jnp.concatenate([last_tile, tail]), ntile * tm


def kernel(x, positions, norm1_g, w_in, b_gate, diff_lambda, diff_subln_g, pool_w, pool_scale,
           w_proj_a, w_proj_b, w_proj_c, w_out, norm2_g, ffn_w_gate, ffn_w_up, ffn_w_down,
           moe_router, moe_w_gate, moe_w_up, moe_w_down, final_norm_g):
    b, s, d = x.shape
    t = b * s
    depth = w_in.shape[0]
    tq = min(512, s)
    tm = min(512, s)
    ff = ffn_w_gate.shape[-1]
    cf = ff // 2 if ff % 512 == 0 and ff >= 1024 else ff
    x2 = x.reshape(t, d)
    cos_t, sin_t = _rope_lane_tables(positions)
    out = None
    for l in range(depth):
        w_l = w_in[l]
        qkv = _qkv_proj(x2, norm1_g[l].reshape(1, d), w_l[:, :OFF_C].astype(BF16),
                        cos_t, sin_t, tq)
        qa, ka, va = qkv[0:3], qkv[3:6], qkv[6:9]
        qb, kb, vb = qkv[9:12]
        oa, lse = [], []
        for g, (window, dil) in enumerate(DILATED_PATTERNS):
            assert window // dil == BLOCK
            o_g, lse_g = _dilated_attn(qa[g], ka[g], va[g], b, s, dil, 1024)
            oa.append(o_g)
            lse.append(lse_g)
        lambda_init = 0.8 - 0.6 * math.exp(-0.3 * l)
        tq_b = min(1024, s)
        ob = _diff_attn(qb, kb, vb, diff_lambda[l], diff_subln_g[l].reshape(1, B_HEAD_WIDTH),
                        b, s, tq_b, min(512, tq_b // 2), min(256, tq_b // 2), 1, lambda_init)

        dense = l % 2 == 0
        router = None
        if not dense:
            wr = jnp.zeros((d, LANES), F32).at[:, :N_EXPERTS].set(moe_router[l // 2])
            wr_hi = wr.astype(BF16)
            router = jnp.concatenate([wr_hi, (wr - wr_hi.astype(F32)).astype(BF16)], axis=1)
        merged = _merge(
            x2, norm1_g[l].reshape(1, d), w_l[:, OFF_C:OFF_G].astype(BF16), w_l[:, OFF_G:].astype(BF16),
            b_gate[l].reshape(1, N_BRANCH * d), oa, lse, ob, pool_w[l].astype(BF16),
            pool_scale[l].reshape(1, C_WIDTH), w_proj_a[l].astype(BF16), w_proj_b[l].astype(BF16),
            w_proj_c[l].astype(BF16), w_out[l].astype(BF16), norm2_g[l].reshape(1, d),
            router, s, tq, BF16 if dense else F32)
        if dense:
            xn, h2 = merged
            i = l // 2
            ntile = t // tm
            ident = jnp.arange(ntile, dtype=jnp.int32)
            x2 = _grouped_swiglu(h2, xn, ffn_w_gate[i:i + 1].astype(BF16), ffn_w_up[i:i + 1].astype(BF16),
                                 ffn_w_down[i:i + 1].astype(BF16), jnp.zeros((ntile,), jnp.int32),
                                 jnp.ones((ntile,), jnp.int32), ident, tm, cf)
            out = None
        else:
            xn, h2, route_i, route_p, counts = merged
            i = l // 2
            slot, expert, valid, src, zero_tiles, nslot = _moe_plan(route_i, counts, t, tm)
            tq_d, tq_c = min(2048, t), min(1024, t)
            tiled = lambda n: slot.reshape(t // n, n, TOP_K).transpose(0, 2, 1).reshape(
                t // n, TOP_K, n // ISSUE_GROUP, ISSUE_GROUP)
            xs = _moe_dispatch(h2, tiled(tq_d), zero_tiles, nslot, tq_d, tm)
            ys = _grouped_swiglu(xs, None, moe_w_gate[i].astype(BF16), moe_w_up[i].astype(BF16),
                                 moe_w_down[i].astype(BF16), expert, valid, src, tm, cf)
            last = l == depth - 1
            res = _moe_combine(xn, route_p, tiled(tq_c), ys, final_norm_g.reshape(1, d), tq_c, last)
            if last:
                out = res
            else:
                x2 = res
    if out is None:
        out = _final_norm(x2, final_norm_g.reshape(1, d), tq)
    return out.reshape(b, s, d)
```

```python
import functools
import math

import jax
import jax.numpy as jnp
from jax import lax
from jax.experimental import pallas as pl
from jax.experimental.pallas import tpu as pltpu

F32 = jnp.float32
BF16 = jnp.bfloat16

HEAD_DIM = 64
ROPE_THETA = 500000.0
ROT_DIM = HEAD_DIM // 4
ROT_HALF = ROT_DIM // 2
BLOCK = 128
EPS = 1e-6
NEG = -1e30

DILATED_PATTERNS = ((128, 1), (512, 4), (2048, 16))
A_GROUPS = len(DILATED_PATTERNS)
A_HEADS_PER_GROUP = 4
A_GROUP_WIDTH = A_HEADS_PER_GROUP * HEAD_DIM
A_WIDTH = A_GROUPS * A_GROUP_WIDTH

B_HEADS = 4
B_HEAD_WIDTH = 2 * HEAD_DIM
B_WIDTH = B_HEADS * B_HEAD_WIDTH
B_VEXT = 2 * B_HEAD_WIDTH

POOL_WINDOWS = (2, 4, 8, 16)
C_GROUP_DIM = 128
C_WIDTH = len(POOL_WINDOWS) * C_GROUP_DIM
POOL_HALO = 16

N_BRANCH = 3
N_EXPERTS = 8
TOP_K = 2

OFF_QA = 0
OFF_KA = OFF_QA + A_WIDTH
OFF_VA = OFF_KA + A_WIDTH
OFF_QB = OFF_VA + A_WIDTH
OFF_KB = OFF_QB + B_WIDTH
OFF_VB = OFF_KB + B_WIDTH
OFF_C = OFF_VB + B_WIDTH
OFF_G = OFF_C + C_WIDTH

LANES = 128
MXU_WIDTH = 256
ROW_TILE = 8
ISSUE_GROUP = LANES
V7X_VMEM_BYTES = 64 * 1024 * 1024
VMEM_LIMIT = V7X_VMEM_BYTES - 8 * 1024 * 1024

QK_SCALE = HEAD_DIM ** -0.5
LOG2_E = math.log2(math.e)


def _params(semantics):
    return pltpu.CompilerParams(dimension_semantics=semantics, vmem_limit_bytes=VMEM_LIMIT)


def _rms(x, g):
    return x * lax.rsqrt(jnp.mean(x * x, axis=-1, keepdims=True) + EPS) * g


def _dot(a, b):
    return jnp.dot(a, b, preferred_element_type=F32)


def _dot_nt(a, b):
    return lax.dot_general(a, b, (((1,), (1,)), ((), ())), preferred_element_type=F32)


def _qkv_kernel(x_ref, g_ref, w_ref, cos_ref, sin_ref,
                qa0, qa1, qa2, ka0, ka1, ka2, va0, va1, va2, qb, kb, vb, zs_ref):
    tq = x_ref.shape[0]
    hb = _rms(x_ref[...], g_ref[...]).astype(BF16)
    cos = cos_ref[...]
    sin = sin_ref[...]
    first_half = (lax.broadcasted_iota(jnp.int32, sin.shape, 1) % HEAD_DIM) < ROT_HALF
    sinlo = jnp.where(first_half, -sin, 0.0)
    sinhi = jnp.where(first_half, 0.0, sin)

    def rope(z):
        return (z * cos + pltpu.roll(z, LANES - ROT_HALF, 1) * sinlo
                + pltpu.roll(z, ROT_HALF, 1) * sinhi)

    def project(off, width, out_ref, rotary, scale, dil=1):
        for c in range(0, width, MXU_WIDTH):
            zz = _dot(hb, w_ref[:, off + c:off + c + MXU_WIDTH])
            for p in range(0, MXU_WIDTH, LANES):
                z = zz[:, p:p + LANES]
                if rotary:
                    z = rope(z)
                if scale != 1.0:
                    z = z * scale
                if dil == 1:
                    out_ref[:, c + p:c + p + LANES] = z.astype(out_ref.dtype)
                else:
                    zs_ref[(c + p) // LANES] = z
        if dil > 1:
            for r in range(dil):
                for sl in range(width // LANES):
                    out_ref[:, r * width + sl * LANES:r * width + (sl + 1) * LANES] = (
                        zs_ref[sl, pl.ds(r, tq // dil, stride=dil), :].astype(out_ref.dtype))

    for g, (q_ref, k_ref, v_ref) in enumerate(((qa0, ka0, va0), (qa1, ka1, va1), (qa2, ka2, va2))):
        dil = DILATED_PATTERNS[g][1]
        project(OFF_QA + g * A_GROUP_WIDTH, A_GROUP_WIDTH, q_ref, True, QK_SCALE * LOG2_E, dil)
        project(OFF_KA + g * A_GROUP_WIDTH, A_GROUP_WIDTH, k_ref, True, 1.0, dil)
        project(OFF_VA + g * A_GROUP_WIDTH, A_GROUP_WIDTH, v_ref, False, 1.0, dil)
    project(OFF_QB, B_WIDTH, qb, True, QK_SCALE * LOG2_E)
    project(OFF_KB, B_WIDTH, kb, True, 1.0)
    lane = lax.broadcasted_iota(jnp.int32, (x_ref.shape[0], B_HEAD_WIDTH), 1)
    ones_col = jnp.where(lane == 0, 1.0, 0.0).astype(vb.dtype)
    heads_per_dot = MXU_WIDTH // B_HEAD_WIDTH
    for h0 in range(0, B_HEADS, heads_per_dot):
        zz = _dot(hb, w_ref[:, OFF_VB + h0 * B_HEAD_WIDTH:OFF_VB + (h0 + heads_per_dot) * B_HEAD_WIDTH])
        for hh in range(heads_per_dot):
            h = h0 + hh
            z = zz[:, hh * B_HEAD_WIDTH:(hh + 1) * B_HEAD_WIDTH]
            vb[:, h * B_VEXT:h * B_VEXT + B_HEAD_WIDTH] = z.astype(vb.dtype)
            vb[:, h * B_VEXT + B_HEAD_WIDTH:(h + 1) * B_VEXT] = ones_col


def _qkv_proj(x2, g, w_qkv, cos_t, sin_t, tq):
    t, d = x2.shape
    row = lambda i: (i, 0)
    const = lambda i: (0, 0)
    dils = [dil for _ in range(3) for (_, dil) in DILATED_PATTERNS]
    shapes = [(t // dil, dil * A_GROUP_WIDTH, tq // dil) for dil in dils]
    shapes += [(t, B_WIDTH, tq), (t, B_WIDTH, tq), (t, B_HEADS * B_VEXT, tq)]
    return pl.pallas_call(
        _qkv_kernel,
        grid=(t // tq,),
        in_specs=[
            pl.BlockSpec((tq, d), row),
            pl.BlockSpec((1, d), const),
            pl.BlockSpec(w_qkv.shape, const, pipeline_mode=pl.Buffered(1)),
            pl.BlockSpec((tq, LANES), row),
            pl.BlockSpec((tq, LANES), row),
        ],
        out_specs=[pl.BlockSpec((rows, w), row) for (_, w, rows) in shapes],
        out_shape=[jax.ShapeDtypeStruct((n, w), BF16) for (n, w, _) in shapes],
        scratch_shapes=[pltpu.VMEM((A_GROUP_WIDTH // LANES, tq, LANES), F32)],
        compiler_params=_params(("arbitrary",)),
        name="qkv_proj",
    )(x2, g, w_qkv, cos_t, sin_t)


def _dilated_kernel(q_ref, kp_ref, kc_ref, vp_ref, vc_ref, o_ref, lse_ref, *, nsub):
    n = pl.program_id(2)
    a = lax.broadcasted_iota(jnp.int32, (BLOCK, 2 * BLOCK), 0)
    j = lax.broadcasted_iota(jnp.int32, (BLOCK, 2 * BLOCK), 1)
    band = (j >= a) & (j <= a + BLOCK)
    band_first = band & ((j >= BLOCK) | (n > 0))
    low_k = lax.broadcasted_iota(jnp.int32, (2 * BLOCK, LANES), 1) < HEAD_DIM
    low_o = lax.broadcasted_iota(jnp.int32, (BLOCK, LANES), 1) < HEAD_DIM
    ones_slab = jnp.ones((2 * BLOCK, LANES), BF16)
    for sb in range(nsub):
        rows = slice(sb * BLOCK, (sb + 1) * BLOCK)
        if sb == 0:
            kcat = jnp.concatenate([kp_ref[...], kc_ref[rows, :]], axis=0)
            vcat = jnp.concatenate([vp_ref[...], vc_ref[rows, :]], axis=0)
            mask = band_first
        else:
            kcat = kc_ref[(sb - 1) * BLOCK:(sb + 1) * BLOCK, :]
            vcat = vc_ref[(sb - 1) * BLOCK:(sb + 1) * BLOCK, :]
            mask = band
        q = q_ref[rows, :]
        for hp in range(q_ref.shape[1] // LANES):
            pair = slice(hp * LANES, (hp + 1) * LANES)
            q_pair = q[:, pair]
            k_pair = kcat[:, pair]
            v_ext = jnp.concatenate([vcat[:, pair], ones_slab], axis=1)
            o_half, lse_half = [], []
            for half in range(2):
                k_h = jnp.where(low_k if half == 0 else ~low_k, k_pair, jnp.zeros_like(k_pair))
                s = jnp.where(mask, _dot_nt(q_pair, k_h), NEG)
                m = jnp.max(jnp.maximum(s[:, :LANES], s[:, LANES:]), axis=-1, keepdims=True)
                p = jnp.concatenate([jnp.exp2(s[:, :LANES] - m), jnp.exp2(s[:, LANES:] - m)], axis=1)
                pv = _dot(p.astype(BF16), v_ext)
                l = pv[:, LANES:]
                o_half.append(pv[:, :LANES] / l)
                lse_half.append(m + jnp.log2(l))
            o_ref[rows, pair] = jnp.where(low_o, o_half[0], o_half[1]).astype(o_ref.dtype)
            lse_ref[rows, pair] = jnp.where(low_o, lse_half[0], lse_half[1])


def _dilated_attn(q, k, v, b, s, dil, qrows):
    l = s // dil
    qrows = min(qrows, l)
    nsub = qrows // BLOCK
    nres = min(dil, max(1, 1024 // qrows))
    w = nres * A_GROUP_WIDTH
    dil, full_dil = dil // nres, dil
    view = lambda t: t.reshape(b, l, full_dil * A_GROUP_WIDTH)
    cur = lambda bi, r, n: (bi, n, r)
    prev = lambda bi, r, n: (bi, jnp.maximum(n * nsub - 1, 0), r)
    o, lse = pl.pallas_call(
        functools.partial(_dilated_kernel, nsub=nsub),
        grid=(b, dil, l // qrows),
        in_specs=[
            pl.BlockSpec((None, qrows, w), cur),
            pl.BlockSpec((None, BLOCK, w), prev),
            pl.BlockSpec((None, qrows, w), cur),
            pl.BlockSpec((None, BLOCK, w), prev),
            pl.BlockSpec((None, qrows, w), cur),
        ],
        out_specs=[pl.BlockSpec((None, qrows, w), cur), pl.BlockSpec((None, qrows, w), cur)],
        out_shape=[jax.ShapeDtypeStruct((b, l, dil * w), BF16),
                   jax.ShapeDtypeStruct((b, l, dil * w), F32)],
        compiler_params=_params(("arbitrary", "arbitrary", "arbitrary")),
        name=f"dilated_attn_d{full_dil}",
    )(view(q), view(k), view(k), view(v), view(v))
    return o.reshape(b * l, dil * w), lse.reshape(b * l, dil * w)


def _diff_kernel(lam_ref, g_ref, q_ref, k_ref, v_ref, o_ref, *scratch, tq, tk, rows, nh, lambda_init):
    i = pl.program_id(2)
    nd = tq // tk
    q = q_ref[...]
    streams = [(mp, rb) for rb in range(tq // rows) for mp in range(2 * nh)]
    acc_all, m_all = scratch[0], scratch[1]
    acc_all[...] = jnp.zeros_like(acc_all)
    m_all[...] = jnp.full_like(m_all, NEG)
    acc_refs = {(mp, rb): acc_all.at[mp, rb] for mp, rb in streams}
    m_refs = {(mp, rb): m_all.at[mp, rb] for mp, rb in streams}
    sa_ref = {(mp, rb): scratch[2].at[mp, rb] for mp, rb in streams}
    sb_ref = {(mp, rb): scratch[3].at[mp, rb] for mp, rb in streams}

    def visible(rb, diag):
        if diag is None:
            return tk
        return max(0, min(tk, (rb + 1) * rows - diag * tk))

    def scores(jc, dst, diag=None):
        k = k_ref[pl.ds(pl.multiple_of(jc * tk, tk), tk), :]
        for mp, rb in streams:
            if visible(rb, diag) == 0:
                continue
            cols = slice(mp * HEAD_DIM, (mp + 1) * HEAD_DIM)
            rs = slice(rb * rows, (rb + 1) * rows)
            dst[mp, rb][...] = _dot_nt(q[rs, cols], k[:, cols])

    def consume(jc, src, diag=None):
        start = pl.multiple_of(jc * tk, tk)
        for mp, rb in streams:
            vis = visible(rb, diag)
            if vis == 0:
                continue
            acc_ref, m_ref = acc_refs[mp, rb], m_refs[mp, rb]
            s = src[mp, rb][:, :vis]
            if diag is not None and diag * tk + vis > rb * rows + 1:
                r = lax.broadcasted_iota(jnp.int32, (rows, vis), 0) + rb * rows
                c = lax.broadcasted_iota(jnp.int32, (rows, vis), 1) + diag * tk
                s = jnp.where(c <= r, s, NEG)
            slabs = [s[:, n * LANES:(n + 1) * LANES] for n in range(vis // LANES)]
            smax = slabs[0]
            for sl in slabs[1:]:
                smax = jnp.maximum(smax, sl)
            m_old = m_ref[...]
            m_new = jnp.maximum(m_old, jnp.max(smax, axis=-1, keepdims=True))
            alpha = jnp.exp2(m_old - m_new)
            p = jnp.concatenate([jnp.exp2(sl - m_new) for sl in slabs], axis=1).astype(BF16)
            vcols = slice((mp // 2) * B_VEXT, (mp // 2 + 1) * B_VEXT)
            pv = _dot(p, v_ref[pl.ds(start, vis), vcols])
            acc = acc_ref[...]
            acc_ref[...] = jnp.concatenate(
                [acc[:, n * LANES:(n + 1) * LANES] * alpha for n in range(B_VEXT // LANES)], axis=1) + pv
            m_ref[...] = m_new

    scores(0, sa_ref)

    def pair(t, carry):
        jc = 2 * t
        scores(jc + 1, sb_ref)
        consume(jc, sa_ref)
        scores(jc + 2, sa_ref)
        consume(jc + 1, sb_ref)
        return carry

    lax.fori_loop(0, i * (nd // 2), pair, 0)
    first = nd * i
    bufs = (sa_ref, sb_ref)
    for dg in range(nd):
        if dg + 1 < nd:
            scores(first + dg + 1, bufs[(dg + 1) % 2], diag=dg + 1)
        consume(first + dg, bufs[dg % 2], diag=dg)

    lp = lam_ref[...]
    lam = (jnp.exp(jnp.sum(lp[0:1] * lp[1:2], axis=-1, keepdims=True))
           - jnp.exp(jnp.sum(lp[2:3] * lp[3:4], axis=-1, keepdims=True)) + lambda_init)
    for hd in range(nh):
        for rb in range(tq // rows):
            acc1 = acc_refs[2 * hd, rb][...]
            acc2 = acc_refs[2 * hd + 1, rb][...]
            o1 = acc1[:, :B_HEAD_WIDTH] / acc1[:, B_HEAD_WIDTH:B_HEAD_WIDTH + 1]
            o2 = acc2[:, :B_HEAD_WIDTH] / acc2[:, B_HEAD_WIDTH:B_HEAD_WIDTH + 1]
            o = o1 - lam * o2
            o_ref[rb * rows:(rb + 1) * rows, hd * B_HEAD_WIDTH:(hd + 1) * B_HEAD_WIDTH] = (
                _rms(o, g_ref[...]) * (1.0 - lambda_init)).astype(o_ref.dtype)


def _diff_attn(qb, kb, vb, lam_p, subln_g, b, s, tq, tk, rows, nh, lambda_init):
    hw = nh * B_HEAD_WIDTH
    q3 = qb.reshape(b, s, B_WIDTH)
    k3 = kb.reshape(b, s, B_WIDTH)
    v3 = vb.reshape(b, s, B_HEADS * B_VEXT)
    assert tq % (2 * tk) == 0 and tk % rows == 0 and B_HEADS % nh == 0
    nmap = 2 * nh
    scratch = ([pltpu.VMEM((nmap, tq // rows, rows, B_VEXT), F32), pltpu.VMEM((nmap, tq // rows, rows, LANES), F32)]
               + [pltpu.VMEM((nmap, tq // rows, rows, tk), F32)] * 2)
    out = pl.pallas_call(
        functools.partial(_diff_kernel, tq=tq, tk=tk, rows=rows, nh=nh, lambda_init=lambda_init),
        grid=(b, B_HEADS // nh, s // tq),
        in_specs=[
            pl.BlockSpec(lam_p.shape, lambda bi, h, i: (0, 0)),
            pl.BlockSpec((1, B_HEAD_WIDTH), lambda bi, h, i: (0, 0)),
            pl.BlockSpec((None, tq, hw), lambda bi, h, i: (bi, i, h)),
            pl.BlockSpec((None, s, hw), lambda bi, h, i: (bi, 0, h)),
            pl.BlockSpec((None, s, nh * B_VEXT), lambda bi, h, i: (bi, 0, h)),
        ],
        out_specs=pl.BlockSpec((None, tq, hw), lambda bi, h, i: (bi, i, h)),
        out_shape=jax.ShapeDtypeStruct((b, s, B_WIDTH), BF16),
        scratch_shapes=scratch,
        compiler_params=_params(("arbitrary", "arbitrary", "arbitrary")),
        name="diff_attn",
    )(lam_p, subln_g, q3, k3, v3)
    return out.reshape(b * s, B_WIDTH)


def _merge_kernel(*refs, tq, rows, tiles_per_seq, route):
    (x_ref, g1_ref, wc_ref, wg_ref, bg_ref, oa0, oa1, oa2, ls0, ls1, ls2, ob_ref,
     pw_ref, ps_ref, wpa_ref, wpb_ref, wpc_ref, wo_ref, g2_ref) = refs[:19]
    if route:
        wrc_ref = refs[19]
        xo_ref, h2_ref, ri_ref, rp_ref, cnt_ref, zc_ref, tm_ref, carry_ref = refs[20:]
    else:
        xo_ref, h2_ref, zc_ref, tm_ref = refs[19:]
    step = pl.program_id(0)
    seq_tile = step % tiles_per_seq
    d = x_ref.shape[-1]
    dils = [dil for (_, dil) in DILATED_PATTERNS]
    slabs = A_GROUP_WIDTH // LANES

    @pl.when(seq_tile == 0)
    def _():
        zc_ref[0:POOL_HALO, :] = jnp.zeros((POOL_HALO, C_WIDTH), F32)

    if route:
        @pl.when(step == 0)
        def _():
            carry_ref[...] = jnp.zeros_like(carry_ref)

    for part in range(tq // rows):
        r0 = part * rows
        rs = slice(r0, r0 + rows)
        x = x_ref[rs, :]
        hb = _rms(x, g1_ref[...]).astype(BF16)

        zc_ref[POOL_HALO + r0:POOL_HALO + r0 + rows, :] = _dot(hb, wc_ref[...])
        pos = seq_tile * tq + r0 + lax.broadcasted_iota(jnp.int32, (rows, C_GROUP_DIM), 0)
        pooled = []
        for gi, win in enumerate(POOL_WINDOWS):
            cols = slice(gi * C_GROUP_DIM, (gi + 1) * C_GROUP_DIM)
            tok = zc_ref[POOL_HALO + r0:POOL_HALO + r0 + rows, cols]
            tot = tok
            for back in range(1, win):
                tot = tot + zc_ref[POOL_HALO + r0 - back:POOL_HALO + r0 - back + rows, cols]
            cnt = jnp.minimum(pos + 1, win).astype(F32)
            dmean = tot / cnt - tok
            pooled.append(_dot(dmean.astype(BF16), pw_ref[gi]))
        out_c = (jnp.concatenate(pooled, axis=-1) * ps_ref[...]).astype(BF16)

        def token_major(src_ref, slot, dil):
            if dil == 1:
                return src_ref[rs, :].astype(F32)
            src_rows = slice(r0 // dil, (r0 + rows) // dil)
            for r in range(dil):
                for sl in range(slabs):
                    col = r * A_GROUP_WIDTH + sl * LANES
                    tm_ref[slot * slabs + sl, pl.ds(r0 + r, rows // dil, stride=dil), :] = (
                        src_ref[src_rows, col:col + LANES].astype(F32))
            return jnp.concatenate([tm_ref[slot * slabs + sl, rs, :] for sl in range(slabs)], axis=1)

        l0, l1, l2 = (token_major(ref, gi, dil) for gi, (ref, dil) in enumerate(zip((ls0, ls1, ls2), dils)))
        o0, o1, o2 = (token_major(ref, A_GROUPS + gi, dil)
                      for gi, (ref, dil) in enumerate(zip((oa0, oa1, oa2), dils)))
        lm = jnp.maximum(jnp.maximum(l0, l1), l2)
        e0, e1, e2 = jnp.exp2(l0 - lm), jnp.exp2(l1 - lm), jnp.exp2(l2 - lm)
        out_a = ((e0 * o0 + e1 * o1 + e2 * o2) / (e0 + e1 + e2)).astype(BF16)

        branches = (_dot(out_a, wpa_ref[...]), _dot(ob_ref[rs, :], wpb_ref[...]), _dot(out_c, wpc_ref[...]))
        mixed = None
        for bi, proj in enumerate(branches):
            zg = _dot(hb, wg_ref[:, bi * d:(bi + 1) * d]) + bg_ref[:, bi * d:(bi + 1) * d]
            term = jax.nn.sigmoid(zg) * proj
            mixed = term if mixed is None else mixed + term
        xn = x + _dot(mixed.astype(BF16), wo_ref[...])
        xo_ref[rs, :] = xn
        h2 = _rms(xn, g2_ref[...])
        if route:
            for cb in range(d // LANES):
                h2_ref[pl.ds(r0 * ROW_TILE + cb, rows, stride=ROW_TILE), :] = h2[:, cb * LANES:(cb + 1) * LANES]
        else:
            h2_ref[rs, :] = h2.astype(h2_ref.dtype)

        if route:
            h_hi = h2.astype(BF16)
            h_lo = (h2 - h_hi.astype(F32)).astype(BF16)
            hi_both = _dot(h_hi, wrc_ref[...])
            logits = hi_both[:, :LANES] + (_dot(h_lo, wrc_ref[:, :LANES]) + hi_both[:, LANES:])
            lane = lax.broadcasted_iota(jnp.int32, (rows, LANES), 1)
            lane_f = lane.astype(F32)
            logits = jnp.where(lane < N_EXPERTS, logits, -jnp.inf)
            m1 = jnp.max(logits, axis=-1, keepdims=True)
            i1 = jnp.min(jnp.where(logits == m1, lane_f, float(LANES)), axis=-1, keepdims=True)
            rest = jnp.where(lane_f == i1, -jnp.inf, logits)
            m2 = jnp.max(rest, axis=-1, keepdims=True)
            i2 = jnp.min(jnp.where(rest == m2, lane_f, float(LANES)), axis=-1, keepdims=True)
            e21 = jnp.exp(m2 - m1)
            p1 = 1.0 / (1.0 + e21)
            p2 = e21 * p1
            sel1 = lane_f == i1
            sel2 = lane_f == i2
            chosen = jnp.where(sel1 | sel2, 1.0, 0.0)

            r = lax.broadcasted_iota(jnp.int32, (rows, rows), 0)
            c = lax.broadcasted_iota(jnp.int32, (rows, rows), 1)
            before = _dot(jnp.where(c < r, 1.0, 0.0).astype(BF16), chosen.astype(BF16)) + carry_ref[...]
            rank1 = jnp.sum(jnp.where(sel1, before, 0.0), axis=-1, keepdims=True)
            rank2 = jnp.sum(jnp.where(sel2, before, 0.0), axis=-1, keepdims=True)
            carry_ref[...] = carry_ref[...] + jnp.sum(chosen, axis=0, keepdims=True)
            packed = jnp.where(lane == 0, i1, jnp.where(lane == 1, i2,
                               jnp.where(lane == 2, rank1, jnp.where(lane == 3, rank2, 0.0))))
            ri_ref[rs, :] = packed.astype(jnp.int32)
            rp_ref[rs, :] = jnp.where(lane == 0, p1, jnp.where(lane == 1, p2, 0.0))

    zc_ref[0:POOL_HALO, :] = zc_ref[tq:tq + POOL_HALO, :]
    if route:
        cnt_ref[...] = jnp.broadcast_to(carry_ref[...], cnt_ref.shape)


def _merge(x2, g1, w_c, w_g, b_g, oa, lse, ob, pool_w, pool_scale, wpa, wpb, wpc, wo, g2,
           router, s, tq, h2_dtype):
    t, d = x2.shape
    route = router is not None
    row = lambda i: (i, 0)
    const = lambda i: (0, 0)
    resident = lambda a: pl.BlockSpec(a.shape, lambda i: (0,) * a.ndim, pipeline_mode=pl.Buffered(1))
    args = [x2, g1, w_c, w_g, b_g, *oa, *lse, ob, pool_w, pool_scale, wpa, wpb, wpc, wo, g2]
    in_specs = [pl.BlockSpec((tq, d), row), pl.BlockSpec((1, d), const), resident(w_c), resident(w_g),
                pl.BlockSpec(b_g.shape, const)]
    in_specs += [pl.BlockSpec((tq // dil, dil * A_GROUP_WIDTH), row) for (_, dil) in DILATED_PATTERNS] * 2
    in_specs += [pl.BlockSpec((tq, B_WIDTH), row), resident(pool_w), pl.BlockSpec(pool_scale.shape, const),
                 resident(wpa), resident(wpb), resident(wpc), resident(wo), pl.BlockSpec((1, d), const)]
    if route:
        assert d == ROW_TILE * LANES, "a token row must fill exactly one (8,128) f32 tile"
        h2_spec = pl.BlockSpec((tq * ROW_TILE, LANES), row)
        h2_shape = jax.ShapeDtypeStruct((t * ROW_TILE, LANES), F32)
    else:
        h2_spec = pl.BlockSpec((tq, d), row)
        h2_shape = jax.ShapeDtypeStruct((t, d), h2_dtype)
    out_specs = [pl.BlockSpec((tq, d), row), h2_spec]
    out_shape = [jax.ShapeDtypeStruct((t, d), F32), h2_shape]
    scratch = [pltpu.VMEM((POOL_HALO + tq, C_WIDTH), F32), pltpu.VMEM((2 * A_GROUPS * (A_GROUP_WIDTH // LANES), tq, LANES), F32)]
    if route:
        args += [router]
        in_specs += [resident(router)]
        out_specs += [pl.BlockSpec((tq, LANES), row), pl.BlockSpec((tq, LANES), row),
                      pl.BlockSpec((8, LANES), const)]
        out_shape += [jax.ShapeDtypeStruct((t, LANES), jnp.int32), jax.ShapeDtypeStruct((t, LANES), F32),
                      jax.ShapeDtypeStruct((8, LANES), F32)]
        scratch += [pltpu.VMEM((1, LANES), F32)]
    return pl.pallas_call(
        functools.partial(_merge_kernel, tq=tq, rows=min(256, tq), tiles_per_seq=s // tq, route=route),
        grid=(t // tq,),
        in_specs=in_specs,
        out_specs=out_specs,
        out_shape=out_shape,
        scratch_shapes=scratch,
        compiler_params=_params(("arbitrary",)),
        name="merge_route" if route else "merge",
    )(*args)


def _swiglu_kernel(te_ref, tv_ref, ts_ref, *refs, residual, nsteps):
    if residual:
        x_ref, res_ref, wg_ref, wu_ref, wd_ref, o_ref, acc_ref = refs
    else:
        x_ref, wg_ref, wu_ref, wd_ref, o_ref, acc_ref = refs
    j = pl.program_id(0)
    c = pl.program_id(1)
    tm, d = acc_ref.shape
    nchunk = d // LANES

    @pl.when((tv_ref[j] == 0) & (c == 0))
    def _():
        o_ref[...] = jnp.zeros_like(o_ref)

    def step(first, last):
        if residual:
            xb = x_ref[...].astype(BF16)
        else:
            xb = jnp.concatenate([x_ref[pl.ds(cb, tm, stride=ROW_TILE), :] for cb in range(nchunk)],
                                 axis=1).astype(BF16)
        gate = _dot(xb, wg_ref[...])
        up = _dot(xb, wu_ref[...])
        mid = (gate * jax.nn.sigmoid(gate) * up).astype(BF16)
        part = _dot(mid, wd_ref[...])
        if not first:
            part = acc_ref[...] + part
        if not last:
            acc_ref[...] = part
        elif residual:
            o_ref[...] = res_ref[...] + part
        else:
            for cb in range(nchunk):
                o_ref[pl.ds(cb, tm, stride=ROW_TILE), :] = part[:, cb * LANES:(cb + 1) * LANES]

    valid = tv_ref[j] > 0
    if nsteps == 1:
        pl.when(valid)(lambda: step(True, True))
    else:
        pl.when(valid & (c == 0))(lambda: step(True, False))
        pl.when(valid & (c == nsteps - 1))(lambda: step(False, True))
        if nsteps > 2:
            pl.when(valid & (c > 0) & (c < nsteps - 1))(lambda: step(False, False))


def _grouped_swiglu(xs, res, wg, wu, wd, tile_expert, tile_valid, tile_src, tm, cf):
    d = wg.shape[1]
    ff = wg.shape[-1]
    residual = res is not None
    xmap = lambda j, c, te, tv, ts: (ts[j], 0)
    if residual:
        n = xs.shape[0]
        row_block = (tm, d)
        in_specs = [pl.BlockSpec(row_block, xmap), pl.BlockSpec(row_block, xmap)]
        args = [xs, res]
    else:
        n = xs.shape[0] // ROW_TILE
        row_block = (tm * ROW_TILE, LANES)
        in_specs = [pl.BlockSpec(row_block, xmap)]
        args = [xs]
    in_specs += [
        pl.BlockSpec((None, d, cf), lambda j, c, te, tv, ts: (te[j], 0, c * tv[j])),
        pl.BlockSpec((None, d, cf), lambda j, c, te, tv, ts: (te[j], 0, c * tv[j])),
        pl.BlockSpec((None, cf, d), lambda j, c, te, tv, ts: (te[j], c * tv[j], 0)),
    ]
    args += [wg, wu, wd]
    return pl.pallas_call(
        functools.partial(_swiglu_kernel, residual=residual, nsteps=ff // cf),
        grid_spec=pltpu.PrefetchScalarGridSpec(
            num_scalar_prefetch=3,
            grid=(n // tm, ff // cf),
            in_specs=in_specs,
            out_specs=pl.BlockSpec(row_block, lambda j, c, te, tv, ts: (j, 0)),
            scratch_shapes=[pltpu.VMEM((tm, d), F32)],
        ),
        out_shape=jax.ShapeDtypeStruct(xs.shape, F32),
        compiler_params=_params(("arbitrary", "arbitrary")),
        name="grouped_swiglu_res" if residual else "grouped_swiglu",
    )(tile_expert, tile_valid, tile_src, *args)


def _dispatch_kernel(zt_ref, slot_ref, h_ref, xs_ref, zero_ref, sem, zsem, *, tq, tm):
    @pl.when(pl.program_id(0) == 0)
    def _():
        zero_ref[...] = jnp.zeros_like(zero_ref)
        for z in range(zt_ref.shape[0]):
            start = pl.multiple_of(zt_ref[z] * (tm * ROW_TILE), tm * ROW_TILE)
            fill = pltpu.make_async_copy(zero_ref, xs_ref.at[pl.ds(start, tm * ROW_TILE), :], zsem.at[0])
            fill.start()
            fill.wait()

    def row_copy(g, u, k):
        src = pl.multiple_of(g * (ISSUE_GROUP * ROW_TILE), ISSUE_GROUP * ROW_TILE) + u * ROW_TILE
        dst = pl.multiple_of(slot_ref[0, k, g, u] * ROW_TILE, ROW_TILE)
        return pltpu.make_async_copy(h_ref.at[pl.ds(src, ROW_TILE), :],
                                     xs_ref.at[pl.ds(dst, ROW_TILE), :], sem.at[k])

    def issue(g, carry):
        for u in range(ISSUE_GROUP):
            row_copy(g, u, 0).start(priority=0)
            row_copy(g, u, 1).start(priority=1)
        return carry

    lax.fori_loop(0, tq // ISSUE_GROUP, issue, 0)
    for k in range(TOP_K):
        pltpu.make_async_copy(h_ref, xs_ref.at[pl.ds(0, tq * ROW_TILE), :], sem.at[k]).wait()


def _moe_dispatch(h2, slots, zero_tiles, nslot, tq, tm):
    d = LANES
    return pl.pallas_call(
        functools.partial(_dispatch_kernel, tq=tq, tm=tm),
        grid_spec=pltpu.PrefetchScalarGridSpec(
            num_scalar_prefetch=1,
            grid=(h2.shape[0] // (tq * ROW_TILE),),
            in_specs=[
                pl.BlockSpec((1, TOP_K, tq // ISSUE_GROUP, ISSUE_GROUP), lambda i, zt: (i, 0, 0, 0),
                             memory_space=pltpu.SMEM),
                pl.BlockSpec((tq * ROW_TILE, d), lambda i, zt: (i, 0)),
            ],
            out_specs=pl.BlockSpec(memory_space=pl.ANY),
            scratch_shapes=[pltpu.VMEM((tm * ROW_TILE, d), h2.dtype), pltpu.SemaphoreType.DMA((TOP_K,)),
                            pltpu.SemaphoreType.DMA((1,))],
        ),
        out_shape=jax.ShapeDtypeStruct((nslot * ROW_TILE, d), h2.dtype),
        compiler_params=_params(("arbitrary",)),
        name="moe_dispatch",
    )(zero_tiles, slots, h2)


def _combine_kernel(slot_ref, next_slot_ref, x_ref, rp_ref, g_ref, ys_ref, o_ref, y_ref, sem,
                    *, tq, normalize):
    i = pl.program_id(0)
    cur = i % 2

    def gather(slots, buf):
        def row_copy(g, u, k):
            src = pl.multiple_of(slots[0, k, g, u] * ROW_TILE, ROW_TILE)
            dst = pl.multiple_of(g * (ISSUE_GROUP * ROW_TILE), ISSUE_GROUP * ROW_TILE) + u * ROW_TILE
            return pltpu.make_async_copy(ys_ref.at[pl.ds(src, ROW_TILE), :],
                                         y_ref.at[buf, k, pl.ds(dst, ROW_TILE), :], sem.at[buf, k])

        def issue(g, carry):
            for u in range(ISSUE_GROUP):
                row_copy(g, u, 0).start(priority=0)
                row_copy(g, u, 1).start(priority=1)
            return carry

        lax.fori_loop(0, tq // ISSUE_GROUP, issue, 0)

    @pl.when(i == 0)
    def _():
        gather(slot_ref, 0)

    @pl.when(i + 1 < pl.num_programs(0))
    def _():
        gather(next_slot_ref, 1 - cur)

    for k in range(TOP_K):
        pltpu.make_async_copy(ys_ref.at[pl.ds(0, tq * ROW_TILE), :], y_ref.at[cur, k], sem.at[cur, k]).wait()
    rp = rp_ref[...]
    y = [jnp.concatenate([y_ref[cur, k, pl.ds(cb, tq, stride=ROW_TILE), :]
                          for cb in range(x_ref.shape[1] // LANES)], axis=1) for k in range(TOP_K)]
    xn = x_ref[...] + rp[:, 0:1] * y[0] + rp[:, 1:2] * y[1]
    o_ref[...] = _rms(xn, g_ref[...]) if normalize else xn


def _moe_combine(x2, route_p, slots, ys, g, tq, normalize):
    t, d = x2.shape
    slot_block = (1, TOP_K, tq // ISSUE_GROUP, ISSUE_GROUP)
    last = t // tq - 1
    return pl.pallas_call(
        functools.partial(_combine_kernel, tq=tq, normalize=normalize),
        grid=(t // tq,),
        in_specs=[
            pl.BlockSpec(slot_block, lambda i: (i, 0, 0, 0), memory_space=pltpu.SMEM),
            pl.BlockSpec(slot_block, lambda i: (jnp.minimum(i + 1, last), 0, 0, 0), memory_space=pltpu.SMEM),
            pl.BlockSpec((tq, d), lambda i: (i, 0)),
            pl.BlockSpec((tq, LANES), lambda i: (i, 0)),
            pl.BlockSpec((1, d), lambda i: (0, 0)),
            pl.BlockSpec(memory_space=pl.ANY),
        ],
        out_specs=pl.BlockSpec((tq, d), lambda i: (i, 0)),
        out_shape=jax.ShapeDtypeStruct((t, d), F32),
        scratch_shapes=[pltpu.VMEM((2, TOP_K, tq * ROW_TILE, LANES), F32), pltpu.SemaphoreType.DMA((2, TOP_K))],
        compiler_params=_params(("arbitrary",)),
        name="moe_combine",
    )(slots, slots, x2, route_p, g, ys)


def _final_norm_kernel(x_ref, g_ref, o_ref):
    o_ref[...] = _rms(x_ref[...], g_ref[...])


def _final_norm(x2, g, tq):
    t, d = x2.shape
    return pl.pallas_call(
        _final_norm_kernel,
        grid=(t // tq,),
        in_specs=[pl.BlockSpec((tq, d), lambda i: (i, 0)), pl.BlockSpec((1, d), lambda i: (0, 0))],
        out_specs=pl.BlockSpec((tq, d), lambda i: (i, 0)),
        out_shape=jax.ShapeDtypeStruct((t, d), F32),
        compiler_params=_params(("arbitrary",)),
        name="final_norm",
    )(x2, g)


def _rope_lane_tables(positions):
    inv_freq = ROPE_THETA ** (-jnp.arange(0, ROT_DIM, 2, dtype=F32) / ROT_DIM)
    ang = positions.astype(F32).reshape(-1, 1) * inv_freq
    cos, sin = jnp.cos(ang), jnp.sin(ang)
    t = ang.shape[0]
    rest = HEAD_DIM - ROT_DIM
    cos_h = jnp.concatenate([cos, cos, jnp.ones((t, rest), F32)], axis=-1)
    sin_h = jnp.concatenate([sin, sin, jnp.zeros((t, rest), F32)], axis=-1)
    rep = LANES // HEAD_DIM
    return jnp.tile(cos_h, (1, rep)), jnp.tile(sin_h, (1, rep))


def _moe_plan(route_i, counts, t, tm):
    cnt = counts[0, :N_EXPERTS].astype(jnp.int32)
    padded = ((cnt + tm - 1) // tm) * tm
    ends = jnp.cumsum(padded)
    offs = ends - padded
    slot = offs[route_i[:, 0:TOP_K]] + route_i[:, TOP_K:2 * TOP_K]
    ntile = (TOP_K * t) // tm + N_EXPERTS
    starts = jnp.arange(ntile, dtype=jnp.int32) * tm
    valid = (starts < ends[-1]).astype(jnp.int32)
    last = jnp.maximum(ends[-1] // tm - 1, 0)
    src = jnp.minimum(jnp.arange(ntile, dtype=jnp.int32), last)
    expert = jnp.sum(((src * tm)[:, None] >= ends[None, :]).astype(jnp.int32), axis=1)
    expert = jnp.minimum(expert, N_EXPERTS - 1)
    last_tile = jnp.where(padded > 0, ends // tm - 1, ntile - 1).astype(jnp.int32)
    tail = jnp.arange((TOP_K * t) // tm, ntile, dtype=jnp.int32)
    return slot, expert, valid, src, jnp.concatenate([last_tile, tail]), ntile * tm


def kernel(x, positions, norm1_g, w_in, b_gate, diff_lambda, diff_subln_g, pool_w, pool_scale,
           w_proj_a, w_proj_b, w_proj_c, w_out, norm2_g, ffn_w_gate, ffn_w_up, ffn_w_down,
           moe_router, moe_w_gate, moe_w_up, moe_w_down, final_norm_g):
    b, s, d = x.shape
    t = b * s
    depth = w_in.shape[0]
    tq = min(512, s)
    tm = min(512, s)
    ff = ffn_w_gate.shape[-1]
    cf = ff // 2 if ff % 512 == 0 and ff >= 1024 else ff
    x2 = x.reshape(t, d)
    cos_t, sin_t = _rope_lane_tables(positions)
    out = None
    for l in range(depth):
        w_l = w_in[l]
        qkv = _qkv_proj(x2, norm1_g[l].reshape(1, d), w_l[:, :OFF_C].astype(BF16),
                        cos_t, sin_t, tq)
        qa, ka, va = qkv[0:3], qkv[3:6], qkv[6:9]
        qb, kb, vb = qkv[9:12]
        oa, lse = [], []
        for g, (window, dil) in enumerate(DILATED_PATTERNS):
            assert window // dil == BLOCK
            o_g, lse_g = _dilated_attn(qa[g], ka[g], va[g], b, s, dil, 1024)
            oa.append(o_g)
            lse.append(lse_g)
        lambda_init = 0.8 - 0.6 * math.exp(-0.3 * l)
        tq_b = min(1024, s)
        ob = _diff_attn(qb, kb, vb, diff_lambda[l], diff_subln_g[l].reshape(1, B_HEAD_WIDTH),
                        b, s, tq_b, min(512, tq_b // 2), min(256, tq_b // 2), 1, lambda_init)

        dense = l % 2 == 0
        router = None
        if not dense:
            wr = jnp.zeros((d, LANES), F32).at[:, :N_EXPERTS].set(moe_router[l // 2])
            wr_hi = wr.astype(BF16)
            router = jnp.concatenate([wr_hi, (wr - wr_hi.astype(F32)).astype(BF16)], axis=1)
        merged = _merge(
            x2, norm1_g[l].reshape(1, d), w_l[:, OFF_C:OFF_G].astype(BF16), w_l[:, OFF_G:].astype(BF16),
            b_gate[l].reshape(1, N_BRANCH * d), oa, lse, ob, pool_w[l].astype(BF16),
            pool_scale[l].reshape(1, C_WIDTH), w_proj_a[l].astype(BF16), w_proj_b[l].astype(BF16),
            w_proj_c[l].astype(BF16), w_out[l].astype(BF16), norm2_g[l].reshape(1, d),
            router, s, tq, BF16 if dense else F32)
        if dense:
            xn, h2 = merged
            i = l // 2
            ntile = t // tm
            ident = jnp.arange(ntile, dtype=jnp.int32)
            x2 = _grouped_swiglu(h2, xn, ffn_w_gate[i:i + 1].astype(BF16), ffn_w_up[i:i + 1].astype(BF16),
                                 ffn_w_down[i:i + 1].astype(BF16), jnp.zeros((ntile,), jnp.int32),
                                 jnp.ones((ntile,), jnp.int32), ident, tm, cf)
            out = None
        else:
            xn, h2, route_i, route_p, counts = merged
            i = l // 2
            slot, expert, valid, src, zero_tiles, nslot = _moe_plan(route_i, counts, t, tm)
            tq_d, tq_c = min(2048, t), min(1024, t)
            tiled = lambda n: slot.reshape(t // n, n, TOP_K).transpose(0, 2, 1).reshape(
                t // n, TOP_K, n // ISSUE_GROUP, ISSUE_GROUP)
            xs = _moe_dispatch(h2, tiled(tq_d), zero_tiles, nslot, tq_d, tm)
            ys = _grouped_swiglu(xs, None, moe_w_gate[i].astype(BF16), moe_w_up[i].astype(BF16),
                                 moe_w_down[i].astype(BF16), expert, valid, src, tm, cf)
            last = l == depth - 1
            res = _moe_combine(xn, route_p, tiled(tq_c), ys, final_norm_g.reshape(1, d), tq_c, last)
            if last:
                out = res
            else:
                x2 = res
    if out is None:
        out = _final_norm(x2, final_norm_g.reshape(1, d), tq)
    return out.reshape(b, s, d)
```
